```python
import math
import jax
import jax.numpy as jnp
from jax import lax
import numpy as np

D_MODEL = 1024
BATCH = 32
SEQ = 2048
DEPTH = 4

GRID_W = 64
CTX_LEN = 256
D_FF = 4 * D_MODEL
NORM_EPS = 1e-6
ROPE_THETA = 10000.0

NA_HEADS = 4
NA_HEAD_DIM = 64
NA_WIDTH = NA_HEADS * NA_HEAD_DIM
NA_WIN_ROWS = 8
NA_WIN_COLS = 16

SSD_HEADS = 8
SSD_HEAD_DIM = 64
SSD_WIDTH = SSD_HEADS * SSD_HEAD_DIM
SSD_GROUPS = 2
SSD_STATE = 128
SSD_CONV = 5
SSD_CHUNK = 128
SSD_CONV_CH = SSD_WIDTH + 2 * SSD_GROUPS * SSD_STATE

MLA_HEADS = 4
MLA_NOPE = 64
MLA_ROPE = 32
MLA_V = 64
MLA_QK = MLA_NOPE + MLA_ROPE
MLA_Q_RANK = 256
MLA_KV_RANK = 128
MLA_WIDTH = MLA_HEADS * MLA_V
MLA_Q_BLOCK = 128

MIX_WIDTH = NA_WIDTH + SSD_WIDTH + MLA_WIDTH
IN_SIZES = (NA_WIDTH, NA_WIDTH, NA_WIDTH, SSD_WIDTH, SSD_CONV_CH, 2 * SSD_HEADS, MLA_Q_RANK, MLA_KV_RANK, MLA_ROPE)
IN_WIDTH = sum(IN_SIZES)
IN_SPLITS = tuple(int(s) for s in np.cumsum(IN_SIZES)[:-1])

kernel_name = 'hybrid_na_ssd_mla_dit_block'


def rmsnorm(t, w):
    tf = t.astype(jnp.float32)
    tf = tf * lax.rsqrt(jnp.mean(tf * tf, axis=-1, keepdims=True) + NORM_EPS)
    return tf.astype(t.dtype) * w


def modulate(t, shift, scale):
    return t * (1 + scale) + shift


def split_heads(t, n_heads):
    return t.reshape(t.shape[0], t.shape[1], n_heads, t.shape[-1] // n_heads)


def squared_relu_mlp(h, w1, w2):
    return jnp.square(jax.nn.relu(h @ w1)) @ w2


def full_attention(q, k, v):
    s = jnp.einsum('bqhd,bkhd->bhqk', q, k).astype(jnp.float32) * (q.shape[-1] ** -0.5)
    p = jax.nn.softmax(s, axis=-1).astype(v.dtype)
    o = jnp.einsum('bhqk,bkhd->bqhd', p, v)
    return o.reshape(o.shape[0], o.shape[1], -1)


def blocked_attention(q, k, v):
    b, l, h, d = q.shape
    blk = min(MLA_Q_BLOCK, l)
    qb = q.reshape(b, l // blk, blk, h, d).swapaxes(0, 1)
    o = lax.map(lambda qi: full_attention(qi, k, v), qb)
    return o.swapaxes(0, 1).reshape(b, l, -1)


def axial_rope_tables(n_tokens):
    pos = jnp.arange(n_tokens)
    axes = jnp.stack([pos // GRID_W, pos % GRID_W], axis=-1).astype(jnp.float32)
    n_freq = MLA_ROPE // 4
    inv_freq = ROPE_THETA ** (-jnp.arange(n_freq, dtype=jnp.float32) / n_freq)
    ang = axes[:, :, None] * inv_freq
    return jnp.cos(ang), jnp.sin(ang)


def axial_rope(t, cos, sin):
    shp = t.shape
    t = t.reshape(shp[:-1] + (2, 2, shp[-1] // 4))
    t1, t2 = t[..., 0, :], t[..., 1, :]
    cos = cos[:, None].astype(t.dtype)
    sin = sin[:, None].astype(t.dtype)
    return jnp.stack([t1 * cos - t2 * sin, t2 * cos + t1 * sin], axis=-2).reshape(shp)


def rotate_rope_dims(t, cos, sin):
    if cos is None:
        return t
    return jnp.concatenate([t[..., :MLA_NOPE], axial_rope(t[..., MLA_NOPE:], cos, sin)], axis=-1)


def neighbourhood_attention(q, k, v, kc, vc, rpb, n_rows):
    b, l, h, d = q.shape
    wr = min(NA_WIN_ROWS, n_rows)
    wc = NA_WIN_COLS
    n_win = wr * GRID_W
    scale = d ** -0.5
    qg = q.reshape(b, n_rows, GRID_W, h, d)
    kg = k.reshape(b, n_rows, GRID_W, h, d)
    vg = v.reshape(b, n_rows, GRID_W, h, d)
    cols = jnp.arange(GRID_W)
    col_start = jnp.clip(cols - wc // 2, 0, GRID_W - wc)
    col_in = (cols[None, :] >= col_start[:, None]) & (cols[None, :] < col_start[:, None] + wc)
    col_idx = jnp.clip(cols[None, :] - cols[:, None], -(wc - 1), wc - 1) + (wc - 1)

    def row_block(r):
        rs = jnp.clip(r - wr // 2, 0, n_rows - wr)
        kw = lax.dynamic_slice_in_dim(kg, rs, wr, axis=1)
        vw = lax.dynamic_slice_in_dim(vg, rs, wr, axis=1)
        qr = lax.dynamic_index_in_dim(qg, r, axis=1, keepdims=False)
        row_idx = rs + jnp.arange(wr) - r + (NA_WIN_ROWS - 1)
        bias = rpb[:, row_idx[None, :, None], col_idx[:, None, :]]
        s_win = jnp.einsum('bqhd,bwkhd->bhqwk', qr, kw).astype(jnp.float32) * scale + bias.astype(jnp.float32)
        s_win = jnp.where(col_in[:, None, :], s_win, -jnp.inf).reshape(b, h, GRID_W, n_win)
        s_ctx = jnp.einsum('bqhd,bchd->bhqc', qr, kc).astype(jnp.float32) * scale
        p = jax.nn.softmax(jnp.concatenate([s_win, s_ctx], axis=-1), axis=-1).astype(v.dtype)
        p_win = p[..., :n_win].reshape(b, h, GRID_W, wr, GRID_W)
        p_ctx = p[..., n_win:]
        return jnp.einsum('bhqwk,bwkhd->bqhd', p_win, vw) + jnp.einsum('bhqc,bchd->bqhd', p_ctx, vc)

    out = lax.map(row_block, jnp.arange(n_rows))
    return out.transpose(1, 0, 2, 3, 4).reshape(b, l, h * d)


def depthwise_conv_centred(t, w, bias):
    pad = w.shape[0] // 2
    y = lax.conv_general_dilated(t, w[:, None, :], window_strides=(1,), padding=[(pad, pad)],
                                 dimension_numbers=('NWC', 'WIO', 'NWC'), feature_group_count=t.shape[-1])
    return y + bias


def segsum(t):
    n = t.shape[-1]
    tr = jnp.broadcast_to(t[..., :, None], t.shape + (n,))
    tr = jnp.where(jnp.tril(jnp.ones((n, n), bool), -1), tr, 0)
    cs = jnp.cumsum(tr, axis=-2)
    return jnp.where(jnp.tril(jnp.ones((n, n), bool)), cs, -jnp.inf)


def ssd_scan(x, dt, a, bm, cm, h0, return_y):
    b, l, h, p = x.shape
    g, n = bm.shape[-2], bm.shape[-1]
    e = h // g
    q = min(SSD_CHUNK, l)
    nc = l // q
    xd = (x * dt[..., None]).reshape(b, nc, q, g, e, p)
    adt = (dt * a).reshape(b, nc, q, g, e).transpose(0, 3, 4, 1, 2)
    bm = bm.reshape(b, nc, q, g, n)
    cm = cm.reshape(b, nc, q, g, n)
    a_cum = jnp.cumsum(adt, axis=-1)
    decay_to_end = jnp.exp(a_cum[..., -1:] - a_cum)
    chunk_states = jnp.einsum('bcsgn,bgecs,bcsgep->bcgepn', bm, decay_to_end, xd)
    states = jnp.concatenate([h0.reshape(b, 1, g, e, p, n), chunk_states], axis=1)
    chunk_decay = jnp.exp(segsum(jnp.pad(a_cum[..., -1], ((0, 0), (0, 0), (0, 0), (1, 0)))))
    states = jnp.einsum('bgezc,bcgepn->bzgepn', chunk_decay, states)
    final = states[:, -1].reshape(b, h, p, n)
    if not return_y:
        return None, final
    lmat = jnp.exp(segsum(adt))
    cb = jnp.einsum('bclgn,bcsgn->bgcls', cm, bm)
    y_diag = jnp.einsum('bgcls,bgecls,bcsgep->bclgep', cb, lmat, xd)
    y_off = jnp.einsum('bclgn,bcgepn,bgecl->bclgep', cm, states[:, :-1], jnp.exp(a_cum))
    return (y_diag + y_off).reshape(b, l, h, p), final


def ssd_mixer(z, xbc, dtr, zc, xbcc, dtrc, conv_w, conv_b, dt_bias, a_log, d_skip, norm_w, need_ctx):
    def conv_split(t):
        t = jax.nn.silu(depthwise_conv_centred(t, conv_w, conv_b))
        xs, bs, cs = jnp.split(t, [SSD_WIDTH, SSD_WIDTH + SSD_GROUPS * SSD_STATE], axis=-1)
        bb, ll = t.shape[0], t.shape[1]
        return (xs.reshape(bb, ll, SSD_HEADS, SSD_HEAD_DIM),
                bs.reshape(bb, ll, SSD_GROUPS, SSD_STATE),
                cs.reshape(bb, ll, SSD_GROUPS, SSD_STATE))

    def step_sizes(t):
        return jax.nn.softplus(t.reshape(t.shape[0], t.shape[1], 2, SSD_HEADS) + dt_bias)

    def flip(t):
        return jnp.flip(t, axis=1)

    xl, bl, cl = conv_split(xbc)
    xc, bc, cc = conv_split(xbcc)
    dtl = step_sizes(dtr)
    dtc = step_sizes(dtrc)
    a = -jnp.exp(a_log)
    h0 = jnp.zeros((xc.shape[0], SSD_HEADS, SSD_HEAD_DIM, SSD_STATE), xc.dtype)
    yc_f, hc_f = ssd_scan(xc, dtc[:, :, 0], a[0], bc, cc, h0, need_ctx)
    yl_f, _ = ssd_scan(xl, dtl[:, :, 0], a[0], bl, cl, hc_f, True)
    yc_b, hc_b = ssd_scan(flip(xc), flip(dtc[:, :, 1]), a[1], flip(bc), flip(cc), h0, need_ctx)
    yl_b, _ = ssd_scan(flip(xl), flip(dtl[:, :, 1]), a[1], flip(bl), flip(cl), hc_b, True)

    def gate_norm(y_f, y_b, xs, zz):
        y = y_f + flip(y_b) + d_skip[:, None] * xs
        y = y.reshape(zz.shape) * jax.nn.silu(zz)
        return rmsnorm(y, norm_w)

    out = gate_norm(yl_f, yl_b, xl, z)
    out_c = gate_norm(yc_f, yc_b, xc, zc) if need_ctx else None
    return out, out_c


def mla_queries(cq, cq_norm_w, w_uq, qn_w, cos, sin):
    q = split_heads(rmsnorm(cq, cq_norm_w) @ w_uq, MLA_HEADS)
    return rotate_rope_dims(rmsnorm(q, qn_w), cos, sin)


def mla_keys_values(ckv, k_rope, ckv_norm_w, w_ukv, kn_w, cos, sin):
    kv = split_heads(rmsnorm(ckv, ckv_norm_w) @ w_ukv, MLA_HEADS)
    k_nope, v = jnp.split(kv, [MLA_NOPE], axis=-1)
    b, l = k_rope.shape[0], k_rope.shape[1]
    k_r = jnp.broadcast_to(k_rope[:, :, None, :], (b, l, MLA_HEADS, MLA_ROPE))
    k = jnp.concatenate([k_nope, k_r], axis=-1)
    return rotate_rope_dims(rmsnorm(k, kn_w), cos, sin), v


def token_mixers(u, uc, need_ctx, n_rows, rope_cos, rope_sin,
                 na_qn_w, na_kn_w, na_rpb,
                 conv_w, conv_b, dt_bias, a_log, d_skip, ssd_norm_w,
                 cq_norm_w, ckv_norm_w, w_uq, w_ukv, mla_qn_w, mla_kn_w):
    q_na, k_na, v_na, z, xbc, dtr, cq, ckv, kr = jnp.split(u, IN_SPLITS, axis=-1)
    q_nac, k_nac, v_nac, zc, xbcc, dtrc, cqc, ckvc, krc = jnp.split(uc, IN_SPLITS, axis=-1)

    k_nac_h = rmsnorm(split_heads(k_nac, NA_HEADS), na_kn_w)
    v_nac_h = split_heads(v_nac, NA_HEADS)
    out_na = neighbourhood_attention(rmsnorm(split_heads(q_na, NA_HEADS), na_qn_w),
                                     rmsnorm(split_heads(k_na, NA_HEADS), na_kn_w),
                                     split_heads(v_na, NA_HEADS), k_nac_h, v_nac_h, na_rpb, n_rows)

    out_ssd, out_ssd_c = ssd_mixer(z, xbc, dtr, zc, xbcc, dtrc, conv_w, conv_b, dt_bias, a_log, d_skip,
                                   ssd_norm_w, need_ctx)

    k_mc, v_mc = mla_keys_values(ckvc, krc, ckv_norm_w, w_ukv, mla_kn_w, None, None)
    k_m, v_m = mla_keys_values(ckv, kr, ckv_norm_w, w_ukv, mla_kn_w, rope_cos, rope_sin)
    q_m = mla_queries(cq, cq_norm_w, w_uq, mla_qn_w, rope_cos, rope_sin)
    out_mla = blocked_attention(q_m, jnp.concatenate([k_m, k_mc], axis=1), jnp.concatenate([v_m, v_mc], axis=1))

    mix = jnp.concatenate([out_na, out_ssd, out_mla], axis=-1)
    if not need_ctx:
        return mix, None
    out_na_c = full_attention(rmsnorm(split_heads(q_nac, NA_HEADS), na_qn_w), k_nac_h, v_nac_h)
    out_mla_c = full_attention(mla_queries(cqc, cq_norm_w, w_uq, mla_qn_w, None, None), k_mc, v_mc)
    mix_c = jnp.concatenate([out_na_c, out_ssd_c, out_mla_c], axis=-1)
    return mix, mix_c


def setup_inputs(seed: int = 0) -> dict:
    key = jax.random.key(seed)
    k = jax.random.split(key, 27)
    f32 = jnp.float32
    L = DEPTH

    def normal(i, shape, scale):
        return jax.random.normal(k[i], shape, f32) * scale

    def gain(i, shape):
        return 1.0 + 0.05 * jax.random.normal(k[i], shape, f32)

    u_dt = jax.random.uniform(k[15], (L, 2, SSD_HEADS), f32)
    dt0 = jnp.exp(math.log(1e-3) + u_dt * (math.log(1e-1) - math.log(1e-3)))
    dt_bias = dt0 + jnp.log(-jnp.expm1(-dt0))
    a_log = jnp.log(jax.random.uniform(k[16], (L, 2, SSD_HEADS), f32, minval=1.0, maxval=16.0))
    return {
        'x': normal(0, (BATCH, SEQ, D_MODEL), 1.0),
        'c': normal(1, (BATCH, D_MODEL), 1.0),
        'ctx': normal(2, (BATCH, CTX_LEN, D_MODEL), 1.0),
        'c_ctx': normal(3, (D_MODEL,), 1.0),
        'w_ada': normal(4, (L, D_MODEL, 6 * D_MODEL), 0.5 * D_MODEL ** -0.5),
        'b_ada': normal(5, (L, 6 * D_MODEL), 0.02),
        'norm1_w': gain(6, (L, D_MODEL)),
        'norm2_w': gain(7, (L, D_MODEL)),
        'w_in': normal(8, (L, D_MODEL, IN_WIDTH), D_MODEL ** -0.5),
        'w_out': normal(9, (L, MIX_WIDTH, D_MODEL), MIX_WIDTH ** -0.5),
        'na_qn_w': gain(10, (L, NA_HEAD_DIM)),
        'na_kn_w': gain(11, (L, NA_HEAD_DIM)),
        'na_rpb': normal(12, (L, NA_HEADS, 2 * NA_WIN_ROWS - 1, 2 * NA_WIN_COLS - 1), 0.1),
        'ssd_conv_w': normal(13, (L, SSD_CONV, SSD_CONV_CH), SSD_CONV ** -0.5),
        'ssd_conv_b': normal(14, (L, SSD_CONV_CH), 0.02),
        'ssd_dt_bias': dt_bias,
        'ssd_a_log': a_log,
        'ssd_d': 1.0 + 0.1 * jax.random.normal(k[17], (L, SSD_HEADS), f32),
        'ssd_norm_w': gain(18, (L, SSD_WIDTH)),
        'mla_cq_norm_w': gain(19, (L, MLA_Q_RANK)),
        'mla_ckv_norm_w': gain(20, (L, MLA_KV_RANK)),
        'mla_w_uq': normal(21, (L, MLA_Q_RANK, MLA_HEADS * MLA_QK), MLA_Q_RANK ** -0.5),
        'mla_w_ukv': normal(22, (L, MLA_KV_RANK, MLA_HEADS * (MLA_NOPE + MLA_V)), MLA_KV_RANK ** -0.5),
        'mla_qn_w': gain(23, (L, MLA_QK)),
        'mla_kn_w': gain(24, (L, MLA_QK)),
        'w_ff1': normal(25, (L, D_MODEL, D_FF), D_MODEL ** -0.5),
        'w_ff2': normal(26, (L, D_FF, D_MODEL), D_FF ** -0.5),
    }


def reference(x, c, ctx, c_ctx, w_ada, b_ada, norm1_w, norm2_w, w_in, w_out,
              na_qn_w, na_kn_w, na_rpb, ssd_conv_w, ssd_conv_b, ssd_dt_bias, ssd_a_log, ssd_d, ssd_norm_w,
              mla_cq_norm_w, mla_ckv_norm_w, mla_w_uq, mla_w_ukv, mla_qn_w, mla_kn_w, w_ff1, w_ff2):
    n_tokens = x.shape[1]
    n_rows = n_tokens // GRID_W
    rope_cos, rope_sin = axial_rope_tables(n_tokens)
    c_act = jax.nn.silu(c)
    c_ctx_act = jax.nn.silu(c_ctx)
    for i in range(DEPTH):
        need_ctx = i < DEPTH - 1
        sh1, sc1, g1, sh2, sc2, g2 = jnp.split((c_act @ w_ada[i] + b_ada[i])[:, None, :], 6, axis=-1)
        sh1c, sc1c, g1c, sh2c, sc2c, g2c = jnp.split(c_ctx_act @ w_ada[i] + b_ada[i], 6, axis=-1)
        u = modulate(rmsnorm(x, norm1_w[i]), sh1, sc1) @ w_in[i]
        uc = modulate(rmsnorm(ctx, norm1_w[i]), sh1c, sc1c) @ w_in[i]
        mix, mix_c = token_mixers(u, uc, need_ctx, n_rows, rope_cos, rope_sin,
                                  na_qn_w[i], na_kn_w[i], na_rpb[i],
                                  ssd_conv_w[i], ssd_conv_b[i], ssd_dt_bias[i], ssd_a_log[i], ssd_d[i], ssd_norm_w[i],
                                  mla_cq_norm_w[i], mla_ckv_norm_w[i], mla_w_uq[i], mla_w_ukv[i], mla_qn_w[i], mla_kn_w[i])
        x = x + g1 * (mix @ w_out[i])
        x = x + g2 * squared_relu_mlp(modulate(rmsnorm(x, norm2_w[i]), sh2, sc2), w_ff1[i], w_ff2[i])
        if need_ctx:
            ctx = ctx + g1c * (mix_c @ w_out[i])
            ctx = ctx + g2c * squared_relu_mlp(modulate(rmsnorm(ctx, norm2_w[i]), sh2c, sc2c), w_ff1[i], w_ff2[i])
    return x
```

```python
import functools

import jax
import jax.numpy as jnp
import numpy as np
from jax import lax
from jax.experimental import pallas as pl
from jax.experimental.pallas import tpu as pltpu

F32 = jnp.float32
BF16 = jnp.bfloat16

D_MODEL = 1024
GRID_W = 64
D_FF = 4 * D_MODEL
NORM_EPS = 1e-6
ROPE_THETA = 10000.0
NA_HEADS = 4
NA_HEAD_DIM = 64
NA_WIDTH = 256
NA_WIN_ROWS = 8
NA_WIN_COLS = 16
SSD_HEADS = 8
SSD_HEAD_DIM = 64
SSD_WIDTH = 512
SSD_GROUPS = 2
SSD_STATE = 128
SSD_CONV = 5
SSD_CHUNK = 128
SSD_CONV_CH = 1024
MLA_HEADS = 4
MLA_NOPE = 64
MLA_ROPE = 32
MLA_V = 64
MLA_QK = 96
MLA_Q_RANK = 256
MLA_KV_RANK = 128
MLA_WIDTH = 256
MIX_WIDTH = 1024
IN_WIDTH = 2736

LANES = 128
MLA_HEAD_PAD = 128
IN_WIDTH_PAD = 2816
MISC_WIDTH = 256
DT_LANE0 = 32
DT_COPIES = 3
VMEM_LIMIT = 56 * 1024 * 1024

NT_DIMS = (((1,), (1,)), ((), ()))
TN_DIMS = (((0,), (0,)), ((), ()))


def _dot(a, b):
    return jnp.dot(a, b, preferred_element_type=F32)


def _dot_nt(a, b):
    return lax.dot_general(a, b, NT_DIMS, preferred_element_type=F32)


def _dot_tn(a, b):
    return lax.dot_general(a, b, TN_DIMS, preferred_element_type=F32)


def _sigmoid(x):
    return 1.0 / (1.0 + jnp.exp(-x))


def _silu(x):
    return x * _sigmoid(x)


def _softplus(x):
    return jnp.maximum(x, 0.0) + jnp.log1p(jnp.exp(-jnp.abs(x)))


def _split3(x):
    x0 = x.astype(BF16)
    r = x - x0.astype(F32)
    x1 = r.astype(BF16)
    r = r - x1.astype(F32)
    return x0, x1, r.astype(BF16)


def _lane_parts(x, lane):
    p0 = x.astype(BF16).astype(F32)
    r = x - p0
    p1 = r.astype(BF16).astype(F32)
    p2 = r - p1
    return jnp.where(lane < DT_LANE0 + 16, p0, jnp.where(lane < DT_LANE0 + 32, p1, p2)).astype(BF16)


def _aligned(x, m):
    return x if isinstance(x, int) else pl.multiple_of(x, m)


def _cparams(sem):
    return pltpu.CompilerParams(dimension_semantics=sem, vmem_limit_bytes=VMEM_LIMIT)


def _adaln_kernel(c_ref, w_ref, b_ref, o_ref):
    act = _silu(c_ref[...]).astype(BF16)
    o_ref[0] = _dot(act, w_ref[0].astype(BF16)) + b_ref[0]


def _adaln(cc, w_ada, b_ada):
    depth, d, n = w_ada.shape
    rows = cc.shape[0]
    tn = 1536
    return pl.pallas_call(
        _adaln_kernel,
        grid=(depth, n // tn),
        in_specs=[
            pl.BlockSpec((rows, d), lambda l, j: (0, 0)),
            pl.BlockSpec((1, d, tn), lambda l, j: (l, 0, j)),
            pl.BlockSpec((1, 1, tn), lambda l, j: (l, 0, j)),
        ],
        out_specs=pl.BlockSpec((1, rows, tn), lambda l, j: (l, 0, j)),
        out_shape=jax.ShapeDtypeStruct((depth, rows, n), F32),
        compiler_params=_cparams(("parallel", "parallel")),
        name="adaln",
    )(cc, w_ada, b_ada.reshape(depth, 1, n))


def _inproj_kernel(x_ref, mod_ref, nw_ref, w_ref, qnw_ref, knw_ref, grp_ref,
                   qkv_ref, z_ref, xbc_ref, cq_ref, misc_ref):
    x = x_ref[0]
    ms = jnp.mean(x * x, axis=-1, keepdims=True)
    xn = x * lax.rsqrt(ms + NORM_EPS) * nw_ref[0]
    mod = mod_ref[0, 0]
    h = (xn * (1.0 + mod[1:2, :]) + mod[0:1, :]).astype(BF16)
    u = _dot(h, w_ref[0])

    grp = grp_ref[...]

    def head_norm(t, w):
        t2 = t * t
        hi = t2.astype(BF16)
        lo = (t2 - hi.astype(F32)).astype(BF16)
        ss = _dot(hi, grp) + _dot(lo, grp)
        return t * lax.rsqrt(ss * (1.0 / NA_HEAD_DIM) + NORM_EPS) * w

    q = head_norm(u[:, 0:256], qnw_ref[0]) * (NA_HEAD_DIM ** -0.5)
    k = head_norm(u[:, 256:512], knw_ref[0])
    qkv_ref[0, :, 0:256] = q.astype(BF16)
    qkv_ref[0, :, 256:512] = k.astype(BF16)
    qkv_ref[0, :, 512:768] = u[:, 512:768].astype(BF16)
    z_ref[0] = u[:, 768:1280]
    xbc_ref[0] = u[:, 1280:2304]
    cq_ref[0] = u[:, 2304:2560]
    misc_ref[0] = u[:, 2560:2816]


def _inproj(xs, mod, layer, mod_row_fn, norm_w, w_in_p, qnw, knw, grp, tm):
    b, s, d = xs.shape
    depth = w_in_p.shape[0]
    outs = (
        jax.ShapeDtypeStruct((b, s, 768), BF16),
        jax.ShapeDtypeStruct((b, s, 512), F32),
        jax.ShapeDtypeStruct((b, s, 1024), F32),
        jax.ShapeDtypeStruct((b, s, 256), F32),
        jax.ShapeDtypeStruct((b, s, MISC_WIDTH), F32),
    )
    tile = lambda w: pl.BlockSpec((1, tm, w), lambda bi, ti: (bi, ti, 0))
    return pl.pallas_call(
        _inproj_kernel,
        grid=(b, s // tm),
        in_specs=[
            tile(d),
            pl.BlockSpec((1, 1, 6, d), lambda bi, ti: (layer, mod_row_fn(bi), 0, 0)),
            pl.BlockSpec((1, 1, d), lambda bi, ti: (layer, 0, 0)),
            pl.BlockSpec((1, d, IN_WIDTH_PAD), lambda bi, ti: (layer, 0, 0)),
            pl.BlockSpec((1, 1, 256), lambda bi, ti: (layer, 0, 0)),
            pl.BlockSpec((1, 1, 256), lambda bi, ti: (layer, 0, 0)),
            pl.BlockSpec((256, 256), lambda bi, ti: (0, 0)),
        ],
        out_specs=[tile(768), tile(512), tile(1024), tile(256), tile(MISC_WIDTH)],
        out_shape=outs,
        compiler_params=_cparams(("parallel", "parallel")),
        name="inproj",
    )(xs, mod, norm_w.reshape(depth, 1, d), w_in_p, qnw, knw, grp)


NA_VARIANT_ROWS = (0, 1, 2, 3, 4, -3, -2, -1)


def _na_bias_kernel(rpb_ref, o_ref, t_ref, *, n_rows):
    lh = pl.program_id(0)
    nrow_off = 2 * NA_WIN_ROWS - 1
    ncol_off = 2 * NA_WIN_COLS - 1
    base = lh * (nrow_off * ncol_off)
    qi = lax.broadcasted_iota(jnp.int32, (GRID_W, LANES), 0)
    li = lax.broadcasted_iota(jnp.int32, (GRID_W, LANES), 1)
    kcol = li & (GRID_W - 1)
    colidx = jnp.clip(kcol - qi, -(NA_WIN_COLS - 1), NA_WIN_COLS - 1) + (NA_WIN_COLS - 1)
    cstart = jnp.clip(qi - NA_WIN_COLS // 2, 0, GRID_W - NA_WIN_COLS)
    valid = (kcol >= cstart) & (kcol < cstart + NA_WIN_COLS)
    for d in range(nrow_off):
        acc = jnp.zeros((GRID_W, LANES), F32)
        for j in range(ncol_off):
            acc = jnp.where(colidx == j, rpb_ref[base + d * ncol_off + j], acc)
        t_ref[d] = jnp.where(valid, acc, -jnp.inf)
    for v, r_rep in enumerate(NA_VARIANT_ROWS):
        r = r_rep if r_rep >= 0 else n_rows + r_rep
        rs = min(max(r - NA_WIN_ROWS // 2, 0), n_rows - NA_WIN_ROWS)
        for p in range(NA_WIN_ROWS // 2):
            d0 = rs + 2 * p - r + (NA_WIN_ROWS - 1)
            o_ref[0, v, :, p * LANES:(p + 1) * LANES] = jnp.where(li < GRID_W, t_ref[d0], t_ref[d0 + 1])


def _na_bias(rpb, n_rows):
    depth, heads = rpb.shape[0], rpb.shape[1]
    n_win = NA_WIN_ROWS * GRID_W
    out = pl.pallas_call(
        functools.partial(_na_bias_kernel, n_rows=n_rows),
        grid=(depth * heads,),
        in_specs=[pl.BlockSpec(memory_space=pltpu.SMEM)],
        out_specs=pl.BlockSpec((1, 8, GRID_W, n_win), lambda i: (i, 0, 0, 0)),
        out_shape=jax.ShapeDtypeStruct((depth * heads, 8, GRID_W, n_win), F32),
        scratch_shapes=[pltpu.VMEM((2 * NA_WIN_ROWS - 1, GRID_W, LANES), F32)],
        compiler_params=_cparams(("parallel",)),
        name="na_bias",
    )(rpb.reshape(-1))
    return out.reshape(depth, heads, 8, GRID_W, n_win)


def _masked_heads_attention(q, key_vals, bias_fn, acc):
    lane = lax.broadcasted_iota(jnp.int32, (1, NA_WIDTH), 1)
    for h in range(NA_HEADS):
        hm = (lane >= h * NA_HEAD_DIM) & (lane < (h + 1) * NA_HEAD_DIM)
        qh = q * jnp.where(hm, 1.0, 0.0).astype(BF16)
        scores = []
        for idx, (k, _) in enumerate(key_vals):
            s = _dot_nt(qh, k)
            b = bias_fn(h, idx)
            scores.append(s if b is None else s + b)
        m = scores[0].max(axis=-1, keepdims=True)
        for s in scores[1:]:
            m = jnp.maximum(m, s.max(axis=-1, keepdims=True))
        l = None
        o = None
        for s, (_, v) in zip(scores, key_vals):
            p = jnp.exp(s - m)
            ls = p.sum(axis=-1, keepdims=True)
            os_ = _dot(p.astype(BF16), v)
            l = ls if l is None else l + ls
            o = os_ if o is None else o + os_
        acc = jnp.where(hm, o * (1.0 / l), acc)
    return acc


def _na_kernel(q_ref, k_ref, v_ref, kc_ref, vc_ref, bias_ref, o_ref, *, rows_per_step, n_rows):
    blk = pl.program_id(1)
    kc = kc_ref[0]
    vc = vc_ref[0]
    n_win = NA_WIN_ROWS * GRID_W
    for i in range(rows_per_step):
        r = blk * rows_per_step + i
        rs = jnp.clip(r - NA_WIN_ROWS // 2, 0, n_rows - NA_WIN_ROWS)
        var = jnp.where(r < NA_WIN_ROWS // 2, r,
                        jnp.where(r > n_rows - NA_WIN_ROWS // 2, r - (n_rows - NA_WIN_ROWS), NA_WIN_ROWS // 2))
        start = pl.multiple_of(rs * GRID_W, GRID_W)
        kw = k_ref[0, pl.ds(start, n_win), :]
        vw = v_ref[0, pl.ds(start, n_win), :]
        q = q_ref[0, i * GRID_W:(i + 1) * GRID_W, :]

        def bias_fn(h, idx):
            return bias_ref[0, h, var] if idx == 0 else None

        acc = _masked_heads_attention(q, [(kw, vw), (kc, vc)], bias_fn, jnp.zeros((GRID_W, NA_WIDTH), F32))
        o_ref[0, i * GRID_W:(i + 1) * GRID_W, :] = acc.astype(BF16)


def _na_attention(qkv_l, qkv_c, bias_all, layer, rows_per_step=4):
    b, s, _ = qkv_l.shape
    ctx_len = qkv_c.shape[1]
    n_rows = s // GRID_W
    tq = rows_per_step * GRID_W
    return pl.pallas_call(
        functools.partial(_na_kernel, rows_per_step=rows_per_step, n_rows=n_rows),
        grid=(b, n_rows // rows_per_step),
        in_specs=[
            pl.BlockSpec((1, tq, 256), lambda bi, ti: (bi, ti, 0)),
            pl.BlockSpec((1, s, 256), lambda bi, ti: (bi, 0, 1)),
            pl.BlockSpec((1, s, 256), lambda bi, ti: (bi, 0, 2)),
            pl.BlockSpec((1, ctx_len, 256), lambda bi, ti: (bi, 0, 1)),
            pl.BlockSpec((1, ctx_len, 256), lambda bi, ti: (bi, 0, 2)),
            pl.BlockSpec((1,) + bias_all.shape[1:], lambda bi, ti: (layer, 0, 0, 0, 0)),
        ],
        out_specs=pl.BlockSpec((1, tq, 256), lambda bi, ti: (bi, ti, 0)),
        out_shape=jax.ShapeDtypeStruct((b, s, NA_WIDTH), BF16),
        compiler_params=_cparams(("parallel", "parallel")),
        name="na_attn",
    )(qkv_l, qkv_l, qkv_l, qkv_c, qkv_c, bias_all)


def _na_ctx_kernel(q_ref, k_ref, v_ref, o_ref):
    q = q_ref[0]
    acc = _masked_heads_attention(q, [(k_ref[0], v_ref[0])], lambda h, idx: None,
                                  jnp.zeros((q.shape[0], NA_WIDTH), F32))
    o_ref[0] = acc.astype(BF16)


def _na_ctx_attention(qkv_c):
    b, c, _ = qkv_c.shape
    spec = lambda j: pl.BlockSpec((1, c, 256), lambda bi: (bi, 0, j))
    return pl.pallas_call(
        _na_ctx_kernel,
        grid=(b,),
        in_specs=[spec(0), spec(1), spec(2)],
        out_specs=spec(0),
        out_shape=jax.ShapeDtypeStruct((b, c, NA_WIDTH), BF16),
        compiler_params=_cparams(("parallel",)),
        name="na_ctx_attn",
    )(qkv_c, qkv_c, qkv_c)


def _ssd_kernel(z_l, xbc_l, dt_l, dtt_l, z_c, xbc_c, dt_c, dtt_c,
                convw_ref, convb_ref, bias_row_ref, alog_row_ref, bias_col_ref, alog_col_ref,
                dskip_ref, normw_ref, out_l, out_c,
                xs_s, bc_s, y_s, st_s, *, n_lat, n_ctx, need_ctx):
    q = SSD_CHUNK
    win = q + 16
    lat_row0 = n_ctx * q

    convw = convw_ref[0]
    convb = convb_ref[0]
    rowi = lax.broadcasted_iota(jnp.int32, (q, 1), 0)

    def conv_chunk(xref, start, off, first, last, dst):
        w_in = xref[0, pl.ds(start, win), :]
        acc = jnp.broadcast_to(convb, (q, SSD_CONV_CH))
        for j in range(SSD_CONV):
            sh = (-(off + j - SSD_CONV // 2)) % win
            r = (pltpu.roll(w_in, sh, axis=0) if sh else w_in)[0:q]
            lo = SSD_CONV // 2 - j
            if first and lo > 0:
                r = jnp.where(rowi >= lo, r, 0.0)
            if last and lo < 0:
                r = jnp.where(rowi < q + lo, r, 0.0)
            acc = acc + r * convw[j:j + 1, :]
        act = _silu(acc)
        xs_s[pl.ds(dst, q), :] = act[:, :SSD_WIDTH]
        bc_s[pl.ds(dst, q), :] = act[:, SSD_WIDTH:].astype(BF16)

    def conv_seq(xref, n, row0):
        conv_chunk(xref, 0, 0, True, False, row0)
        conv_chunk(xref, (n - 1) * q - 16, 16, False, True, row0 + (n - 1) * q)
        if n > 2:
            def body(c, carry):
                start = pl.multiple_of(c * q - 8, 8)
                conv_chunk(xref, start, 8, False, False, pl.multiple_of(row0 + c * q, q))
                return carry
            lax.fori_loop(1, n - 1, body, 0)

    conv_seq(xbc_c, n_ctx, 0)
    conv_seq(xbc_l, n_lat, lat_row0)

    lane = lax.broadcasted_iota(jnp.int32, (1, LANES), 1)
    dmask = (lane >= DT_LANE0) & (lane < DT_LANE0 + 16 * DT_COPIES)
    bias_row = bias_row_ref[0]
    a_row = jnp.where(dmask, -jnp.exp(alog_row_ref[0]), 0.0)
    bias_col = bias_col_ref[0]
    a_col = -jnp.exp(alog_col_ref[0])
    ri = lax.broadcasted_iota(jnp.int32, (q, q), 0)
    ci = lax.broadcasted_iota(jnp.int32, (q, q), 1)
    lo_mask = ci <= ri
    up_mask = ri <= ci
    tri_lo = jnp.where(lo_mask, 1.0, 0.0).astype(BF16)
    tri_up = jnp.where(up_mask, 1.0, 0.0).astype(BF16)
    sel_r = lax.broadcasted_iota(jnp.int32, (LANES, SSD_HEADS * LANES), 0)
    sel_c = lax.broadcasted_iota(jnp.int32, (LANES, SSD_HEADS * LANES), 1)
    sel64_r = lax.broadcasted_iota(jnp.int32, (LANES, SSD_WIDTH), 0)
    sel64_c = lax.broadcasted_iota(jnp.int32, (LANES, SSD_WIDTH), 1)
    lane_lt64 = lax.broadcasted_iota(jnp.int32, (1, LANES), 1) < SSD_HEAD_DIM

    def sel_matrices(direction):
        in3 = (sel_r >= DT_LANE0) & (sel_r < DT_LANE0 + 48)
        sel128 = jnp.where(in3 & (((sel_r - DT_LANE0) & 15) == direction * SSD_HEADS + (sel_c >> 7)), 1.0, 0.0)
        in2 = (sel64_r >= DT_LANE0) & (sel64_r < DT_LANE0 + 32)
        sel64 = jnp.where(in2 & (((sel64_r - DT_LANE0) & 15) == direction * SSD_HEADS + (sel64_c >> 6)), 1.0, 0.0)
        return sel128.astype(BF16), sel64.astype(BF16)

    sels = (sel_matrices(0), sel_matrices(1))
    tri3_col = (jnp.concatenate([tri_lo] * 3, axis=1), jnp.concatenate([tri_up] * 3, axis=1))
    tri3_row = (jnp.concatenate([tri_up] * 3, axis=0), jnp.concatenate([tri_lo] * 3, axis=0))
    masks = (lo_mask, up_mask)

    st_s[...] = jnp.zeros_like(st_s)
    y_s[...] = jnp.zeros_like(y_s)

    def chunk(dt_ref, dtt_ref, c, row0, direction, compute_y):
        cq_ = _aligned(c * q, q)
        rows = pl.ds(_aligned(row0 + c * q, q), q)
        x = xs_s[rows, :]
        bc = bc_s[rows, :]
        raw = dt_ref[0, pl.ds(cq_, q), :]
        raw_t = dtt_ref[0, :, pl.ds(cq_, q)]
        sel128, sel64 = sels[direction]

        dt = jnp.where(dmask, _softplus(raw + bias_row), 0.0)
        adt = dt * a_row
        a0, a1, a2 = _split3(adt)
        acum = _dot(tri3_col[direction], jnp.concatenate([a0, a1, a2], axis=0))
        a_exp = _dot(_lane_parts(acum, lane), sel128)
        dt64 = _dot(_lane_parts(dt, lane), sel64)
        ac64 = jnp.concatenate(
            [jnp.where(lane_lt64, a_exp[:, (2 * j) * LANES:(2 * j + 1) * LANES],
                       a_exp[:, (2 * j + 1) * LANES:(2 * j + 2) * LANES]) for j in range(SSD_HEADS // 2)], axis=1)

        dt_t = _softplus(raw_t + bias_col)
        adt_t = dt_t * a_col
        t0, t1, t2 = _split3(adt_t)
        acum_t = _dot(jnp.concatenate([t0, t1, t2], axis=1), tri3_row[direction])

        end = q - 1 if direction == 0 else 0
        ac_end = ac64[end:end + 1, :]
        xd = x * dt64
        xdb = xd.astype(BF16)
        xdw = (xd * jnp.exp(ac_end - ac64)).astype(BF16)
        st = st_s[direction]
        grp_w = SSD_WIDTH // SSD_GROUPS

        if compute_y:
            stb = st.astype(BF16)
            e_ac = jnp.exp(ac64)
            parts = []
            offs = []
            for g in range(SSD_GROUPS):
                bg = bc[:, g * SSD_STATE:(g + 1) * SSD_STATE]
                cg = bc[:, (SSD_GROUPS + g) * SSD_STATE:(SSD_GROUPS + g + 1) * SSD_STATE]
                gmat = _dot_nt(cg, bg)
                offs.append(_dot(cg, stb[:, g * grp_w:(g + 1) * grp_w]))
                for j in range(2):
                    h0 = g * 4 + 2 * j
                    xp = xdb[:, h0 * SSD_HEAD_DIM:(h0 + 2) * SSD_HEAD_DIM]
                    res = []
                    for hh in (h0, h0 + 1):
                        row = direction * SSD_HEADS + hh
                        dmat = a_exp[:, hh * LANES:(hh + 1) * LANES] - acum_t[row:row + 1, :]
                        lmat = jnp.exp(jnp.where(masks[direction], dmat, -jnp.inf))
                        res.append(_dot((gmat * lmat).astype(BF16), xp))
                    parts.append(jnp.where(lane_lt64, res[0], res[1]))
            y = jnp.concatenate(parts, axis=1) + jnp.concatenate(offs, axis=1) * e_ac
            y_s[rows, :] = y_s[rows, :] + y

        new = []
        for g in range(SSD_GROUPS):
            bg = bc[:, g * SSD_STATE:(g + 1) * SSD_STATE]
            new.append(_dot_tn(bg, xdw[:, g * grp_w:(g + 1) * grp_w]))
        st_s[direction] = st * jnp.exp(ac_end) + jnp.concatenate(new, axis=1)

    for i in range(n_ctx):
        chunk(dt_c, dtt_c, i, 0, 0, need_ctx)
        chunk(dt_c, dtt_c, n_ctx - 1 - i, 0, 1, need_ctx)

    def lat_body(i, carry):
        chunk(dt_l, dtt_l, i, lat_row0, 0, True)
        chunk(dt_l, dtt_l, n_lat - 1 - i, lat_row0, 1, True)
        return carry

    lax.fori_loop(0, n_lat, lat_body, 0)

    dskip = dskip_ref[0]
    normw = normw_ref[0]

    def finish(z_ref, o_ref, c, row0):
        rows = pl.ds(_aligned(row0 + c * q, q), q)
        crow = pl.ds(_aligned(c * q, q), q)
        y = y_s[rows, :] + dskip * xs_s[rows, :]
        y = y * _silu(z_ref[0, crow, :])
        ms = jnp.mean(y * y, axis=-1, keepdims=True)
        o_ref[0, crow, :] = (y * lax.rsqrt(ms + NORM_EPS) * normw).astype(BF16)

    if need_ctx:
        for i in range(n_ctx):
            finish(z_c, out_c, i, 0)
    else:
        out_c[...] = jnp.zeros_like(out_c)

    def fin_body(c, carry):
        finish(z_l, out_l, c, lat_row0)
        return carry

    lax.fori_loop(0, n_lat, fin_body, 0)


def _ssd(z_l, xbc_l, misc_l, dtt_l, z_c, xbc_c, misc_c, dtt_c, params, layer, need_ctx):
    b, s, _ = z_l.shape
    c = z_c.shape[1]
    n_lat, n_ctx = s // SSD_CHUNK, c // SSD_CHUNK
    t = s + c
    convw, convb, bias_row, alog_row, bias_col, alog_col, dskip, normw = params

    def seq_specs(n):
        return [
            pl.BlockSpec((1, n, SSD_WIDTH), lambda bi: (bi, 0, 0)),
            pl.BlockSpec((1, n, SSD_CONV_CH), lambda bi: (bi, 0, 0)),
            pl.BlockSpec((1, n, LANES), lambda bi: (bi, 0, 1)),
            pl.BlockSpec((1, 16, n), lambda bi: (bi, 0, 0)),
        ]

    def par(shape):
        return pl.BlockSpec((1,) + shape, lambda bi: (layer,) + (0,) * len(shape))

    return pl.pallas_call(
        functools.partial(_ssd_kernel, n_lat=n_lat, n_ctx=n_ctx, need_ctx=need_ctx),
        grid=(b,),
        in_specs=seq_specs(s) + seq_specs(c) + [
            par((8, SSD_CONV_CH)), par((1, SSD_CONV_CH)), par((1, LANES)), par((1, LANES)),
            par((16, 1)), par((16, 1)), par((1, SSD_WIDTH)), par((1, SSD_WIDTH)),
        ],
        out_specs=[
            pl.BlockSpec((1, s, SSD_WIDTH), lambda bi: (bi, 0, 0)),
            pl.BlockSpec((1, c, SSD_WIDTH), lambda bi: (bi, 0, 0)),
        ],
        out_shape=(jax.ShapeDtypeStruct((b, s, SSD_WIDTH), BF16), jax.ShapeDtypeStruct((b, c, SSD_WIDTH), BF16)),
        scratch_shapes=[
            pltpu.VMEM((t, SSD_WIDTH), F32),
            pltpu.VMEM((t, SSD_WIDTH), BF16),
            pltpu.VMEM((t, SSD_WIDTH), F32),
            pltpu.VMEM((2, SSD_STATE, SSD_WIDTH), F32),
        ],
        compiler_params=_cparams(("parallel",)),
        name="ssd",
    )(z_l, xbc_l, misc_l, dtt_l, z_c, xbc_c, misc_c, dtt_c,
      convw, convb, bias_row, alog_row, bias_col, alog_col, dskip, normw)


def _mlaproj_kernel(cq_ref, misc_ref, cqn_ref, ckvn_ref, wuq_ref, wukv_ref, qnw_ref, knw_ref, *rest, rope):
    if rope:
        cos_ref, s1_ref, s2_ref, q_ref, k_ref, v_ref = rest
    else:
        q_ref, k_ref, v_ref = rest

    def rms(t, w):
        ms = jnp.mean(t * t, axis=-1, keepdims=True)
        return t * lax.rsqrt(ms + NORM_EPS) * w

    def head_finish(t, w):
        ss = jnp.sum(t * t, axis=-1, keepdims=True)
        t = t * lax.rsqrt(ss * (1.0 / MLA_QK) + NORM_EPS) * w
        if rope:
            t = (t * cos_ref[...] + pltpu.roll(t, LANES - 8, axis=1) * s1_ref[...]
                 + pltpu.roll(t, 8, axis=1) * s2_ref[...])
        return t.astype(BF16)

    misc = misc_ref[0]
    cqn = rms(cq_ref[0], cqn_ref[0]).astype(BF16)
    qf = _dot(cqn, wuq_ref[0])
    ckvn = rms(misc[:, 0:MLA_KV_RANK], ckvn_ref[0]).astype(BF16)
    kv = _dot(ckvn, wukv_ref[0])
    lane = lax.broadcasted_iota(jnp.int32, (1, LANES), 1)
    kr = jnp.where((lane >= MLA_NOPE) & (lane < MLA_QK), pltpu.roll(misc[:, LANES:2 * LANES], MLA_NOPE, axis=1), 0.0)
    qnw = qnw_ref[0]
    knw = knw_ref[0]
    for h in range(MLA_HEADS):
        sl = slice(h * MLA_HEAD_PAD, (h + 1) * MLA_HEAD_PAD)
        q_ref[0, :, sl] = head_finish(qf[:, sl], qnw)
        k_ref[0, :, sl] = head_finish(kv[:, sl] + kr, knw)
    v_ref[0] = kv[:, MLA_HEADS * MLA_HEAD_PAD:].astype(BF16)


def _mlaproj(cq, misc, weights, rope_tabs, layer, tm):
    b, s, _ = cq.shape
    cqn_w, ckvn_w, wuq_p, wukv_p, qnw_p, knw_p = weights
    rope = rope_tabs is not None
    tile = lambda w: pl.BlockSpec((1, tm, w), lambda bi, ti: (bi, ti, 0))

    def par(shape):
        return pl.BlockSpec((1,) + shape, lambda bi, ti: (layer,) + (0,) * len(shape))

    in_specs = [
        tile(MLA_Q_RANK), tile(MISC_WIDTH),
        par((1, MLA_Q_RANK)), par((1, MLA_KV_RANK)),
        par((MLA_Q_RANK, MLA_HEADS * MLA_HEAD_PAD)), par((MLA_KV_RANK, MLA_HEADS * MLA_HEAD_PAD + MLA_WIDTH)),
        par((1, LANES)), par((1, LANES)),
    ]
    args = [cq, misc, cqn_w, ckvn_w, wuq_p, wukv_p, qnw_p, knw_p]
    if rope:
        in_specs += [pl.BlockSpec((tm, LANES), lambda bi, ti: (ti, 0))] * 3
        args += list(rope_tabs)
    return pl.pallas_call(
        functools.partial(_mlaproj_kernel, rope=rope),
        grid=(b, s // tm),
        in_specs=in_specs,
        out_specs=[tile(MLA_HEADS * MLA_HEAD_PAD), tile(MLA_HEADS * MLA_HEAD_PAD), tile(MLA_WIDTH)],
        out_shape=(
            jax.ShapeDtypeStruct((b, s, MLA_HEADS * MLA_HEAD_PAD), BF16),
            jax.ShapeDtypeStruct((b, s, MLA_HEADS * MLA_HEAD_PAD), BF16),
            jax.ShapeDtypeStruct((b, s, MLA_WIDTH), BF16),
        ),
        compiler_params=_cparams(("parallel", "parallel")),
        name="mla_proj",
    )(*args)


def _mla_kernel(*refs, with_latent):
    if with_latent:
        q_ref, kl_ref, vl_ref, kc_ref, vc_ref, o_ref = refs
    else:
        q_ref, kc_ref, vc_ref, o_ref = refs
    scale = MLA_QK ** -0.5
    tq = q_ref.shape[1]
    lane = lax.broadcasted_iota(jnp.int32, (1, MLA_WIDTH), 1)
    acc = jnp.zeros((tq, MLA_WIDTH), F32)
    for h in range(MLA_HEADS):
        sl = slice(h * MLA_HEAD_PAD, (h + 1) * MLA_HEAD_PAD)
        qh = q_ref[0, :, sl]
        sc = _dot_nt(qh, kc_ref[0, :, sl]) * scale
        m = sc.max(axis=-1, keepdims=True)
        if with_latent:
            sl_ = _dot_nt(qh, kl_ref[0, :, sl]) * scale
            m = jnp.maximum(m, sl_.max(axis=-1, keepdims=True))
            p_l = jnp.exp(sl_ - m)
            l = p_l.sum(axis=-1, keepdims=True)
            o = _dot(p_l.astype(BF16), vl_ref[0])
        p_c = jnp.exp(sc - m)
        lc = p_c.sum(axis=-1, keepdims=True)
        oc = _dot(p_c.astype(BF16), vc_ref[0])
        if with_latent:
            l = l + lc
            o = o + oc
        else:
            l, o = lc, oc
        hm = (lane >= h * MLA_V) & (lane < (h + 1) * MLA_V)
        acc = jnp.where(hm, o * (1.0 / l), acc)
    o_ref[0] = acc.astype(BF16)


def _mla_attention(q, k_l, v_l, k_c, v_c, tq):
    b, sq, wq = q.shape
    c = k_c.shape[1]
    with_latent = k_l is not None
    in_specs = [pl.BlockSpec((1, tq, wq), lambda bi, ti: (bi, ti, 0))]
    args = [q]
    if with_latent:
        s = k_l.shape[1]
        in_specs += [pl.BlockSpec((1, s, wq), lambda bi, ti: (bi, 0, 0)),
                     pl.BlockSpec((1, s, MLA_WIDTH), lambda bi, ti: (bi, 0, 0))]
        args += [k_l, v_l]
    in_specs += [pl.BlockSpec((1, c, wq), lambda bi, ti: (bi, 0, 0)),
                 pl.BlockSpec((1, c, MLA_WIDTH), lambda bi, ti: (bi, 0, 0))]
    args += [k_c, v_c]
    return pl.pallas_call(
        functools.partial(_mla_kernel, with_latent=with_latent),
        grid=(b, sq // tq),
        in_specs=in_specs,
        out_specs=pl.BlockSpec((1, tq, MLA_WIDTH), lambda bi, ti: (bi, ti, 0)),
        out_shape=jax.ShapeDtypeStruct((b, sq, MLA_WIDTH), BF16),
        compiler_params=_cparams(("parallel", "parallel")),
        name="mla_attn" if with_latent else "mla_ctx_attn",
    )(*args)


def _outffn_kernel(x_ref, na_ref, ssd_ref, mla_ref, mod_ref, nw_ref, wo_ref, w1_ref, w2_ref, o_ref,
                   x1_s, xn_s, acc_s):
    f = pl.program_id(2)
    mod = mod_ref[0, 0]

    @pl.when(f == 0)
    def _():
        mix = (_dot(na_ref[0], wo_ref[0, 0:NA_WIDTH, :])
               + _dot(ssd_ref[0], wo_ref[0, NA_WIDTH:NA_WIDTH + SSD_WIDTH, :])
               + _dot(mla_ref[0], wo_ref[0, NA_WIDTH + SSD_WIDTH:, :]))
        x1 = x_ref[0] + mod[2:3, :] * mix
        x1_s[...] = x1
        ms = jnp.mean(x1 * x1, axis=-1, keepdims=True)
        xn = x1 * lax.rsqrt(ms + NORM_EPS) * nw_ref[0]
        xn_s[...] = (xn * (1.0 + mod[4:5, :]) + mod[3:4, :]).astype(BF16)
        acc_s[...] = jnp.zeros_like(acc_s)

    hmid = _dot(xn_s[...], w1_ref[0])
    hmid = jnp.square(jnp.maximum(hmid, 0.0)).astype(BF16)
    acc_s[...] += _dot(hmid, w2_ref[0])

    @pl.when(f == pl.num_programs(2) - 1)
    def _():
        o_ref[0] = x1_s[...] + mod[5:6, :] * acc_s[...]


def _outffn(xs, na, ssd, mla, mod, layer, mod_row_fn, norm_w, w_out, w1, w2, tm, tf):
    b, s, d = xs.shape
    depth = w_out.shape[0]
    tile = lambda w: pl.BlockSpec((1, tm, w), lambda bi, ti, fi: (bi, ti, 0))
    return pl.pallas_call(
        _outffn_kernel,
        grid=(b, s // tm, D_FF // tf),
        in_specs=[
            tile(d), tile(NA_WIDTH), tile(SSD_WIDTH), tile(MLA_WIDTH),
            pl.BlockSpec((1, 1, 6, d), lambda bi, ti, fi: (layer, mod_row_fn(bi), 0, 0)),
            pl.BlockSpec((1, 1, d), lambda bi, ti, fi: (layer, 0, 0)),
            pl.BlockSpec((1, MIX_WIDTH, d), lambda bi, ti, fi: (layer, 0, 0)),
            pl.BlockSpec((1, d, tf), lambda bi, ti, fi: (layer, 0, fi)),
            pl.BlockSpec((1, tf, d), lambda bi, ti, fi: (layer, fi, 0)),
        ],
        out_specs=tile(d),
        out_shape=jax.ShapeDtypeStruct((b, s, d), F32),
        scratch_shapes=[pltpu.VMEM((tm, d), F32), pltpu.VMEM((tm, d), BF16), pltpu.VMEM((tm, d), F32)],
        compiler_params=_cparams(("parallel", "parallel", "arbitrary")),
        name="outproj_mlp",
    )(xs, na, ssd, mla, mod, norm_w.reshape(depth, 1, d), w_out, w1, w2)


def _rope_tables(n_tokens):
    pos = jnp.arange(n_tokens)
    axes = jnp.stack([pos // GRID_W, pos % GRID_W], axis=-1).astype(F32)
    n_freq = MLA_ROPE // 4
    inv_freq = ROPE_THETA ** (-jnp.arange(n_freq, dtype=F32) / n_freq)
    ang = axes[:, :, None] * inv_freq
    cos, sin = jnp.cos(ang), jnp.sin(ang)
    zeros = jnp.zeros_like(sin)
    cos_l = jnp.stack([cos, cos], axis=2).reshape(n_tokens, MLA_ROPE)
    s1_l = jnp.stack([-sin, zeros], axis=2).reshape(n_tokens, MLA_ROPE)
    s2_l = jnp.stack([zeros, sin], axis=2).reshape(n_tokens, MLA_ROPE)

    def place(t, fill):
        left = jnp.full((n_tokens, MLA_NOPE), fill, F32)
        right = jnp.full((n_tokens, LANES - MLA_QK), fill, F32)
        return jnp.concatenate([left, t, right], axis=1)

    return place(cos_l, 1.0), place(s1_l, 0.0), place(s2_l, 0.0)


def _prep_params(w_in, na_qn_w, na_kn_w, ssd_conv_w, ssd_conv_b, ssd_dt_bias, ssd_a_log, ssd_d, ssd_norm_w,
                 mla_cq_norm_w, mla_ckv_norm_w, mla_w_uq, mla_w_ukv, mla_qn_w, mla_kn_w):
    depth = w_in.shape[0]
    d = w_in.shape[1]
    dtw = w_in[..., 2304:2320]
    w_in_p = jnp.concatenate(
        [w_in[..., :2304], w_in[..., 2320:2576], w_in[..., 2576:2704], w_in[..., 2704:2736]]
        + [dtw] * DT_COPIES + [jnp.zeros((depth, d, IN_WIDTH_PAD - 2720 - 16 * DT_COPIES), F32)], axis=-1).astype(BF16)
    qnw = jnp.tile(na_qn_w, (1, NA_HEADS))[:, None, :]
    knw = jnp.tile(na_kn_w, (1, NA_HEADS))[:, None, :]
    gi = np.arange(NA_WIDTH) // NA_HEAD_DIM
    grp = jnp.asarray((gi[:, None] == gi[None, :]).astype(np.float32), BF16)

    convw = jnp.pad(ssd_conv_w, ((0, 0), (0, 8 - SSD_CONV), (0, 0)))
    convb = ssd_conv_b[:, None, :]

    def dt_row(t):
        flat = jnp.tile(t.reshape(depth, 16), (1, DT_COPIES))
        return jnp.pad(flat, ((0, 0), (DT_LANE0, LANES - DT_LANE0 - 16 * DT_COPIES)))[:, None, :]

    bias_row = dt_row(ssd_dt_bias)
    alog_row = dt_row(ssd_a_log)
    bias_col = ssd_dt_bias.reshape(depth, 16, 1)
    alog_col = ssd_a_log.reshape(depth, 16, 1)
    dskip = jnp.repeat(ssd_d, SSD_HEAD_DIM, axis=-1)[:, None, :]
    normw = ssd_norm_w[:, None, :]
    ssd_params = (convw, convb, bias_row, alog_row, bias_col, alog_col, dskip, normw)

    wuq = mla_w_uq.reshape(depth, MLA_Q_RANK, MLA_HEADS, MLA_QK)
    wuq_p = jnp.pad(wuq, ((0, 0), (0, 0), (0, 0), (0, MLA_HEAD_PAD - MLA_QK))).reshape(depth, MLA_Q_RANK, -1).astype(BF16)
    wukv = mla_w_ukv.reshape(depth, MLA_KV_RANK, MLA_HEADS, MLA_NOPE + MLA_V)
    wk = jnp.pad(wukv[..., :MLA_NOPE], ((0, 0), (0, 0), (0, 0), (0, MLA_HEAD_PAD - MLA_NOPE))).reshape(depth, MLA_KV_RANK, -1)
    wv = wukv[..., MLA_NOPE:].reshape(depth, MLA_KV_RANK, -1)
    wukv_p = jnp.concatenate([wk, wv], axis=-1).astype(BF16)
    pad_head = lambda t: jnp.pad(t, ((0, 0), (0, MLA_HEAD_PAD - MLA_QK)))[:, None, :]
    mla_weights = (mla_cq_norm_w[:, None, :], mla_ckv_norm_w[:, None, :], wuq_p, wukv_p,
                   pad_head(mla_qn_w), pad_head(mla_kn_w))
    return w_in_p, qnw, knw, grp, ssd_params, mla_weights


def kernel(x, c, ctx, c_ctx, w_ada, b_ada, norm1_w, norm2_w, w_in, w_out, na_qn_w, na_kn_w, na_rpb, ssd_conv_w, ssd_conv_b, ssd_dt_bias, ssd_a_log, ssd_d, ssd_norm_w, mla_cq_norm_w, mla_ckv_norm_w, mla_w_uq, mla_w_ukv, mla_qn_w, mla_kn_w, w_ff1, w_ff2):
    b, s, d = x.shape
    ctx_len = ctx.shape[1]
    depth = w_in.shape[0]
    n_rows = s // GRID_W
    assert d == D_MODEL and s % (4 * GRID_W) == 0 and n_rows > NA_WIN_ROWS and ctx_len % SSD_CHUNK == 0

    mod_rows = ((b + 1 + 7) // 8) * 8
    cc = jnp.concatenate([c, c_ctx[None, :], jnp.zeros((mod_rows - b - 1, d), F32)], axis=0)
    mod = _adaln(cc, w_ada, b_ada).reshape(depth, mod_rows, 6, d)

    w_in_p, qnw, knw, grp, ssd_params, mla_weights = _prep_params(
        w_in, na_qn_w, na_kn_w, ssd_conv_w, ssd_conv_b, ssd_dt_bias, ssd_a_log, ssd_d, ssd_norm_w,
        mla_cq_norm_w, mla_ckv_norm_w, mla_w_uq, mla_w_ukv, mla_qn_w, mla_kn_w)
    w_out_b = w_out.astype(BF16)
    w1_b = w_ff1.astype(BF16)
    w2_b = w_ff2.astype(BF16)
    bias_all = _na_bias(na_rpb, n_rows)
    rope_tabs = _rope_tables(s)

    lat_row = lambda bi: bi
    ctx_row = lambda bi: b
    tm = 512
    tm_c = min(tm, b * ctx_len)
    ctx_flat = ctx.reshape(1, b * ctx_len, d)
    dt_lo = LANES + DT_LANE0 + 32

    for i in range(depth):
        need_ctx = i < depth - 1
        qkv_l, z_l, xbc_l, cq_l, misc_l = _inproj(x, mod, i, lat_row, norm1_w, w_in_p, qnw, knw, grp, tm)
        outs_c = _inproj(ctx_flat, mod, i, ctx_row, norm1_w, w_in_p, qnw, knw, grp, tm_c)
        qkv_c, z_c, xbc_c, cq_c, misc_c = [t.reshape(b, ctx_len, t.shape[-1]) for t in outs_c]

        na_l = _na_attention(qkv_l, qkv_c, bias_all, i)

        dtt_l = jnp.swapaxes(misc_l[:, :, dt_lo:dt_lo + 16], 1, 2)
        dtt_c = jnp.swapaxes(misc_c[:, :, dt_lo:dt_lo + 16], 1, 2)
        ssd_l, ssd_c = _ssd(z_l, xbc_l, misc_l, dtt_l, z_c, xbc_c, misc_c, dtt_c, ssd_params, i, need_ctx)

        qm_l, km_l, vm_l = _mlaproj(cq_l, misc_l, mla_weights, rope_tabs, i, tm)
        qm_c, km_c, vm_c = _mlaproj(cq_c, misc_c, mla_weights, None, i, ctx_len)
        mla_l = _mla_attention(qm_l, km_l, vm_l, km_c, vm_c, 256)

        x = _outffn(x, na_l, ssd_l, mla_l, mod, i, lat_row, norm2_w, w_out_b, w1_b, w2_b, tm, 1024)
        if need_ctx:
            na_c = _na_ctx_attention(qkv_c)
            mla_c = _mla_attention(qm_c, None, None, km_c, vm_c, ctx_len)
            flat = lambda t: t.reshape(1, b * ctx_len, t.shape[-1])
            ctx_flat = _outffn(ctx_flat, flat(na_c), flat(ssd_c), flat(mla_c), mod, i, ctx_row, norm2_w,
                               w_out_b, w1_b, w2_b, tm_c, 1024)
    return x
```

```python
import functools

import jax
import jax.numpy as jnp
import numpy as np
from jax import lax
from jax.experimental import pallas as pl
from jax.experimental.pallas import tpu as pltpu

F32 = jnp.float32
BF16 = jnp.bfloat16

D_MODEL = 1024
GRID_W = 64
D_FF = 4 * D_MODEL
NORM_EPS = 1e-6
ROPE_THETA = 10000.0
LOG2E = 1.4426950408889634
NA_HEADS = 4
NA_HEAD_DIM = 64
NA_WIDTH = 256
NA_WIN_ROWS = 8
NA_WIN_COLS = 16
SSD_HEADS = 8
SSD_HEAD_DIM = 64
SSD_WIDTH = 512
SSD_GROUPS = 2
SSD_STATE = 128
SSD_CONV = 5
SSD_CHUNK = 128
SSD_CONV_CH = 1024
MLA_HEADS = 4
MLA_NOPE = 64
MLA_ROPE = 32
MLA_V = 64
MLA_QK = 96
MLA_Q_RANK = 256
MLA_KV_RANK = 128
MLA_WIDTH = 256
MIX_WIDTH = 1024
IN_WIDTH = 2736

LANES = 128
MLA_HEAD_PAD = 128
IN_WIDTH_PAD = 2816
MISC_WIDTH = 256
DT_LANE0 = 32
DT_COPIES = 3
VMEM_LIMIT = 56 * 1024 * 1024

NT_DIMS = (((1,), (1,)), ((), ()))
TN_DIMS = (((0,), (0,)), ((), ()))


def _dot(a, b):
    return jnp.dot(a, b, preferred_element_type=F32)


def _dot_nt(a, b):
    return lax.dot_general(a, b, NT_DIMS, preferred_element_type=F32)


def _dot_tn(a, b):
    return lax.dot_general(a, b, TN_DIMS, preferred_element_type=F32)


def _sigmoid(x):
    return 1.0 / (1.0 + jnp.exp(-x))


def _silu(x):
    return x * _sigmoid(x)


def _softplus(x):
    return jnp.maximum(x, 0.0) + jnp.log1p(jnp.exp(-jnp.abs(x)))


def _split3(x):
    x0 = x.astype(BF16)
    r = x - x0.astype(F32)
    x1 = r.astype(BF16)
    r = r - x1.astype(F32)
    return x0, x1, r.astype(BF16)


def _lane_parts(x, lane):
    p0 = x.astype(BF16).astype(F32)
    r = x - p0
    p1 = r.astype(BF16).astype(F32)
    p2 = r - p1
    return jnp.where(lane < DT_LANE0 + 16, p0, jnp.where(lane < DT_LANE0 + 32, p1, p2)).astype(BF16)


def _aligned(x, m):
    return x if isinstance(x, int) else pl.multiple_of(x, m)


def _cparams(sem):
    return pltpu.CompilerParams(dimension_semantics=sem, vmem_limit_bytes=VMEM_LIMIT)


def _adaln_kernel(c_ref, w_ref, b_ref, o_ref):
    act = _silu(c_ref[...]).astype(BF16)
    o_ref[0] = _dot(act, w_ref[0].astype(BF16)) + b_ref[0]


def _adaln(cc, w_ada, b_ada):
    depth, d, n = w_ada.shape
    rows = cc.shape[0]
    tn = 1536
    return pl.pallas_call(
        _adaln_kernel,
        grid=(depth, n // tn),
        in_specs=[
            pl.BlockSpec((rows, d), lambda l, j: (0, 0)),
            pl.BlockSpec((1, d, tn), lambda l, j: (l, 0, j)),
            pl.BlockSpec((1, 1, tn), lambda l, j: (l, 0, j)),
        ],
        out_specs=pl.BlockSpec((1, rows, tn), lambda l, j: (l, 0, j)),
        out_shape=jax.ShapeDtypeStruct((depth, rows, n), F32),
        compiler_params=_cparams(("parallel", "parallel")),
        name="adaln",
    )(cc, w_ada, b_ada.reshape(depth, 1, n))


def _inproj_kernel(x_ref, mod_ref, nw_ref, w_ref, qnw_ref, knw_ref, grp_ref,
                   qkv_ref, z_ref, xbc_ref, cq_ref, misc_ref):
    x = x_ref[0]
    ms = jnp.mean(x * x, axis=-1, keepdims=True)
    xn = x * lax.rsqrt(ms + NORM_EPS) * nw_ref[0]
    mod = mod_ref[0, 0]
    h = (xn * (1.0 + mod[1:2, :]) + mod[0:1, :]).astype(BF16)
    u = _dot(h, w_ref[0])

    grp = grp_ref[...]

    def head_norm(t, w):
        t2 = t * t
        hi = t2.astype(BF16)
        lo = (t2 - hi.astype(F32)).astype(BF16)
        ss = _dot(hi, grp) + _dot(lo, grp)
        return t * lax.rsqrt(ss * (1.0 / NA_HEAD_DIM) + NORM_EPS) * w

    q = head_norm(u[:, 0:256], qnw_ref[0]) * (NA_HEAD_DIM ** -0.5 * LOG2E)
    k = head_norm(u[:, 256:512], knw_ref[0])
    qkv_ref[0, :, 0:256] = q.astype(BF16)
    qkv_ref[0, :, 256:512] = k.astype(BF16)
    qkv_ref[0, :, 512:768] = u[:, 512:768].astype(BF16)
    z_ref[0] = u[:, 768:1280]
    xbc_ref[0] = u[:, 1280:2304]
    cq_ref[0] = u[:, 2304:2560]
    misc_ref[0] = u[:, 2560:2816]


def _inproj(xs, mod, layer, mod_row_fn, norm_w, w_in_p, qnw, knw, grp, tm):
    b, s, d = xs.shape
    depth = w_in_p.shape[0]
    outs = (
        jax.ShapeDtypeStruct((b, s, 768), BF16),
        jax.ShapeDtypeStruct((b, s, 512), F32),
        jax.ShapeDtypeStruct((b, s, 1024), F32),
        jax.ShapeDtypeStruct((b, s, 256), F32),
        jax.ShapeDtypeStruct((b, s, MISC_WIDTH), F32),
    )
    tile = lambda w: pl.BlockSpec((1, tm, w), lambda bi, ti: (bi, ti, 0))
    return pl.pallas_call(
        _inproj_kernel,
        grid=(b, s // tm),
        in_specs=[
            tile(d),
            pl.BlockSpec((1, 1, 6, d), lambda bi, ti: (layer, mod_row_fn(bi), 0, 0)),
            pl.BlockSpec((1, 1, d), lambda bi, ti: (layer, 0, 0)),
            pl.BlockSpec((1, d, IN_WIDTH_PAD), lambda bi, ti: (layer, 0, 0)),
            pl.BlockSpec((1, 1, 256), lambda bi, ti: (layer, 0, 0)),
            pl.BlockSpec((1, 1, 256), lambda bi, ti: (layer, 0, 0)),
            pl.BlockSpec((256, 256), lambda bi, ti: (0, 0)),
        ],
        out_specs=[tile(768), tile(512), tile(1024), tile(256), tile(MISC_WIDTH)],
        out_shape=outs,
        compiler_params=_cparams(("parallel", "parallel")),
        name="inproj",
    )(xs, mod, norm_w.reshape(depth, 1, d), w_in_p, qnw, knw, grp)


NA_VARIANT_ROWS = (0, 1, 2, 3, 4, -3, -2, -1)


def _na_bias_kernel(rpb_ref, o_ref, t_ref, *, n_rows):
    nrow_off = 2 * NA_WIN_ROWS - 1
    ncol_off = 2 * NA_WIN_COLS - 1
    base = (pl.program_id(0) * NA_HEADS + pl.program_id(1)) * (nrow_off * ncol_off)
    qi = lax.broadcasted_iota(jnp.int32, (GRID_W, LANES), 0)
    li = lax.broadcasted_iota(jnp.int32, (GRID_W, LANES), 1)
    kcol = li & (GRID_W - 1)
    colidx = jnp.clip(kcol - qi, -(NA_WIN_COLS - 1), NA_WIN_COLS - 1) + (NA_WIN_COLS - 1)
    cstart = jnp.clip(qi - NA_WIN_COLS // 2, 0, GRID_W - NA_WIN_COLS)
    valid = (kcol >= cstart) & (kcol < cstart + NA_WIN_COLS)
    for d in range(nrow_off):
        acc = jnp.zeros((GRID_W, LANES), F32)
        for j in range(ncol_off):
            acc = jnp.where(colidx == j, rpb_ref[base + d * ncol_off + j], acc)
        t_ref[d] = jnp.where(valid, acc * LOG2E, -jnp.inf)
    for v, r_rep in enumerate(NA_VARIANT_ROWS):
        r = r_rep if r_rep >= 0 else n_rows + r_rep
        rs = min(max(r - NA_WIN_ROWS // 2, 0), n_rows - NA_WIN_ROWS)
        for p in range(NA_WIN_ROWS // 2):
            d0 = rs + 2 * p - r + (NA_WIN_ROWS - 1)
            o_ref[0, v, :, p * LANES:(p + 1) * LANES] = jnp.where(li < GRID_W, t_ref[d0], t_ref[d0 + 1])


def _na_bias(rpb, n_rows):
    depth, heads = rpb.shape[0], rpb.shape[1]
    n_win = NA_WIN_ROWS * GRID_W
    return pl.pallas_call(
        functools.partial(_na_bias_kernel, n_rows=n_rows),
        grid=(depth, heads),
        in_specs=[pl.BlockSpec(memory_space=pltpu.SMEM)],
        out_specs=pl.BlockSpec((1, 8, GRID_W, n_win), lambda l, h: (l, 0, h, 0)),
        out_shape=jax.ShapeDtypeStruct((depth, 8, heads * GRID_W, n_win), F32),
        scratch_shapes=[pltpu.VMEM((2 * NA_WIN_ROWS - 1, GRID_W, LANES), F32)],
        compiler_params=_cparams(("parallel", "parallel")),
        name="na_bias",
    )(rpb.reshape(-1))


def _stacked_heads_attention(q, key_vals, biases):
    m_rows = q.shape[0]
    lane = lax.broadcasted_iota(jnp.int32, (1, NA_WIDTH), 1)
    hms = [jnp.where((lane >= h * NA_HEAD_DIM) & (lane < (h + 1) * NA_HEAD_DIM), 1.0, 0.0) for h in range(NA_HEADS)]
    qs = jnp.concatenate([q * hm.astype(BF16) for hm in hms], axis=0)
    scores = []
    for (k, _), b in zip(key_vals, biases):
        s = _dot_nt(qs, k)
        scores.append(s if b is None else s + b)
    m = scores[0].max(axis=-1, keepdims=True)
    for s in scores[1:]:
        m = jnp.maximum(m, s.max(axis=-1, keepdims=True))
    l = None
    o = None
    for s, (_, v) in zip(scores, key_vals):
        p = jnp.exp2(s - m)
        ls = p.sum(axis=-1, keepdims=True)
        os_ = _dot(p.astype(BF16), v)
        l = ls if l is None else l + ls
        o = os_ if o is None else o + os_
    o = o * (1.0 / l)
    out = o[0:m_rows] * hms[0]
    for h in range(1, NA_HEADS):
        out = out + o[h * m_rows:(h + 1) * m_rows] * hms[h]
    return out


def _na_kernel(q_ref, k_ref, v_ref, kc_ref, vc_ref, bias_ref, o_ref, *, rows_per_step, n_rows):
    blk = pl.program_id(1)
    kc = kc_ref[0]
    vc = vc_ref[0]
    n_win = NA_WIN_ROWS * GRID_W
    for i in range(rows_per_step):
        r = blk * rows_per_step + i
        rs = jnp.clip(r - NA_WIN_ROWS // 2, 0, n_rows - NA_WIN_ROWS)
        var = jnp.where(r < NA_WIN_ROWS // 2, r,
                        jnp.where(r > n_rows - NA_WIN_ROWS // 2, r - (n_rows - NA_WIN_ROWS), NA_WIN_ROWS // 2))
        start = pl.multiple_of(rs * GRID_W, GRID_W)
        kw = k_ref[0, pl.ds(start, n_win), :]
        vw = v_ref[0, pl.ds(start, n_win), :]
        q = q_ref[0, i * GRID_W:(i + 1) * GRID_W, :]
        out = _stacked_heads_attention(q, [(kw, vw), (kc, vc)], [bias_ref[0, var], None])
        o_ref[0, i * GRID_W:(i + 1) * GRID_W, :] = out.astype(BF16)


def _na_attention(qkv_l, qkv_c, bias_all, layer, rows_per_step=4):
    b, s, _ = qkv_l.shape
    ctx_len = qkv_c.shape[1]
    n_rows = s // GRID_W
    tq = rows_per_step * GRID_W
    return pl.pallas_call(
        functools.partial(_na_kernel, rows_per_step=rows_per_step, n_rows=n_rows),
        grid=(b, n_rows // rows_per_step),
        in_specs=[
            pl.BlockSpec((1, tq, 256), lambda bi, ti: (bi, ti, 0)),
            pl.BlockSpec((1, s, 256), lambda bi, ti: (bi, 0, 1)),
            pl.BlockSpec((1, s, 256), lambda bi, ti: (bi, 0, 2)),
            pl.BlockSpec((1, ctx_len, 256), lambda bi, ti: (bi, 0, 1)),
            pl.BlockSpec((1, ctx_len, 256), lambda bi, ti: (bi, 0, 2)),
            pl.BlockSpec((1,) + bias_all.shape[1:], lambda bi, ti: (layer, 0, 0, 0)),
        ],
        out_specs=pl.BlockSpec((1, tq, 256), lambda bi, ti: (bi, ti, 0)),
        out_shape=jax.ShapeDtypeStruct((b, s, NA_WIDTH), BF16),
        compiler_params=_cparams(("parallel", "parallel")),
        name="na_attn",
    )(qkv_l, qkv_l, qkv_l, qkv_c, qkv_c, bias_all)


def _na_ctx_kernel(q_ref, k_ref, v_ref, o_ref):
    o_ref[0] = _stacked_heads_attention(q_ref[0], [(k_ref[0], v_ref[0])], [None]).astype(BF16)


def _na_ctx_attention(qkv_c):
    b, c, _ = qkv_c.shape
    spec = lambda j: pl.BlockSpec((1, c, 256), lambda bi: (bi, 0, j))
    return pl.pallas_call(
        _na_ctx_kernel,
        grid=(b,),
        in_specs=[spec(0), spec(1), spec(2)],
        out_specs=spec(0),
        out_shape=jax.ShapeDtypeStruct((b, c, NA_WIDTH), BF16),
        compiler_params=_cparams(("parallel",)),
        name="na_ctx_attn",
    )(qkv_c, qkv_c, qkv_c)


def _ssd_kernel(z_l, xbc_l, dt_l, dtt_l, z_c, xbc_c, dt_c, dtt_c,
                convw_ref, convb_ref, bias_row_ref, alog_row_ref, bias_col_ref, alog_col_ref,
                dskip_ref, normw_ref, out_l, out_c,
                xs_s, bc_s, yf_s, yb_s, stf_s, stb_s, *, n_lat, n_ctx, need_ctx):
    q = SSD_CHUNK
    lat_row0 = n_ctx * q

    convw = convw_ref[0]
    convb = convb_ref[0]
    rowi = lax.broadcasted_iota(jnp.int32, (q, 1), 0)

    win = q + 16

    def conv_chunk(xref, start, off, first, last, dst):
        w_in = xref[0, pl.ds(start, win), :]
        acc = jnp.broadcast_to(convb, (q, SSD_CONV_CH))
        for j in range(SSD_CONV):
            sh = (-(off + j - SSD_CONV // 2)) % win
            r = (pltpu.roll(w_in, sh, axis=0) if sh else w_in)[0:q]
            lo = SSD_CONV // 2 - j
            if first and lo > 0:
                r = jnp.where(rowi >= lo, r, 0.0)
            if last and lo < 0:
                r = jnp.where(rowi < q + lo, r, 0.0)
            acc = acc + r * convw[j:j + 1, :]
        act = _silu(acc)
        xs_s[pl.ds(dst, q), :] = act[:, :SSD_WIDTH]
        bc_s[pl.ds(dst, q), :] = act[:, SSD_WIDTH:].astype(BF16)

    def conv_seq(xref, n, row0):
        conv_chunk(xref, 0, 0, True, False, row0)
        conv_chunk(xref, (n - 1) * q - 16, 16, False, True, row0 + (n - 1) * q)
        if n > 2:
            def body(c, carry):
                start = pl.multiple_of(c * q - 8, 8)
                conv_chunk(xref, start, 8, False, False, pl.multiple_of(row0 + c * q, q))
                return carry
            lax.fori_loop(1, n - 1, body, 0)

    conv_seq(xbc_c, n_ctx, 0)
    conv_seq(xbc_l, n_lat, lat_row0)

    lane = lax.broadcasted_iota(jnp.int32, (1, LANES), 1)
    dmask = (lane >= DT_LANE0) & (lane < DT_LANE0 + 16 * DT_COPIES)
    bias_row = bias_row_ref[0]
    a_row = jnp.where(dmask, -jnp.exp(alog_row_ref[0]), 0.0)
    bias_col = bias_col_ref[0]
    a_col = -jnp.exp(alog_col_ref[0])
    ri = lax.broadcasted_iota(jnp.int32, (q, q), 0)
    ci = lax.broadcasted_iota(jnp.int32, (q, q), 1)
    lo_mask = ci <= ri
    up_mask = ri <= ci
    tri_lo = jnp.where(lo_mask, 1.0, 0.0).astype(BF16)
    tri_up = jnp.where(up_mask, 1.0, 0.0).astype(BF16)
    sel_r = lax.broadcasted_iota(jnp.int32, (LANES, SSD_HEADS * LANES), 0)
    sel_c = lax.broadcasted_iota(jnp.int32, (LANES, SSD_HEADS * LANES), 1)
    sel64_r = lax.broadcasted_iota(jnp.int32, (LANES, SSD_WIDTH), 0)
    sel64_c = lax.broadcasted_iota(jnp.int32, (LANES, SSD_WIDTH), 1)
    lane_lt64 = lax.broadcasted_iota(jnp.int32, (1, LANES), 1) < SSD_HEAD_DIM

    def sel_matrices(direction):
        in3 = (sel_r >= DT_LANE0) & (sel_r < DT_LANE0 + 48)
        sel128 = jnp.where(in3 & (((sel_r - DT_LANE0) & 15) == direction * SSD_HEADS + (sel_c >> 7)), 1.0, 0.0)
        in2 = (sel64_r >= DT_LANE0) & (sel64_r < DT_LANE0 + 32)
        sel64 = jnp.where(in2 & (((sel64_r - DT_LANE0) & 15) == direction * SSD_HEADS + (sel64_c >> 6)), 1.0, 0.0)
        return sel128.astype(BF16), sel64.astype(BF16)

    sels = (sel_matrices(0), sel_matrices(1))
    tri3_col = (jnp.concatenate([tri_lo] * 3, axis=1), jnp.concatenate([tri_up] * 3, axis=1))
    tri3_row = (jnp.concatenate([tri_up] * 3, axis=0), jnp.concatenate([tri_lo] * 3, axis=0))
    masks = (lo_mask, up_mask)

    y_refs = (yf_s, yb_s)
    st_refs = (stf_s, stb_s)
    stf_s[...] = jnp.zeros_like(stf_s)
    stb_s[...] = jnp.zeros_like(stb_s)

    def chunk(dt_ref, dtt_ref, c, row0, direction, compute_y):
        cq_ = _aligned(c * q, q)
        rows = pl.ds(_aligned(row0 + c * q, q), q)
        x = xs_s[rows, :]
        bc = bc_s[rows, :]
        raw = dt_ref[0, pl.ds(cq_, q), :]
        raw_t = dtt_ref[0, :, pl.ds(cq_, q)]
        sel128, sel64 = sels[direction]

        dt = jnp.where(dmask, _softplus(raw + bias_row), 0.0)
        adt = dt * a_row
        a0, a1, a2 = _split3(adt)
        acum = _dot(tri3_col[direction], jnp.concatenate([a0, a1, a2], axis=0))
        a_exp = _dot(_lane_parts(acum, lane), sel128)
        dt64 = _dot(_lane_parts(dt, lane), sel64)
        ac64 = jnp.concatenate(
            [jnp.where(lane_lt64, a_exp[:, (2 * j) * LANES:(2 * j + 1) * LANES],
                       a_exp[:, (2 * j + 1) * LANES:(2 * j + 2) * LANES]) for j in range(SSD_HEADS // 2)], axis=1)

        dt_t = _softplus(raw_t + bias_col)
        adt_t = dt_t * a_col
        t0, t1, t2 = _split3(adt_t)
        acum_t = _dot(jnp.concatenate([t0, t1, t2], axis=1), tri3_row[direction])

        end = q - 1 if direction == 0 else 0
        ac_end = ac64[end:end + 1, :]
        xd = x * dt64
        xdb = xd.astype(BF16)
        xdw = (xd * jnp.exp(ac_end - ac64)).astype(BF16)
        st = st_refs[direction][...]
        grp_w = SSD_WIDTH // SSD_GROUPS

        if compute_y:
            stb = st.astype(BF16)
            e_ac = jnp.exp(ac64)
            parts = []
            offs = []
            for g in range(SSD_GROUPS):
                bg = bc[:, g * SSD_STATE:(g + 1) * SSD_STATE]
                cg = bc[:, (SSD_GROUPS + g) * SSD_STATE:(SSD_GROUPS + g + 1) * SSD_STATE]
                gmat = _dot_nt(cg, bg)
                offs.append(_dot(cg, stb[:, g * grp_w:(g + 1) * grp_w]))
                for j in range(2):
                    h0 = g * 4 + 2 * j
                    xp = xdb[:, h0 * SSD_HEAD_DIM:(h0 + 2) * SSD_HEAD_DIM]
                    res = []
                    for hh in (h0, h0 + 1):
                        row = direction * SSD_HEADS + hh
                        dmat = a_exp[:, hh * LANES:(hh + 1) * LANES] - acum_t[row:row + 1, :]
                        lmat = jnp.exp(jnp.where(masks[direction], dmat, -jnp.inf))
                        res.append(_dot((gmat * lmat).astype(BF16), xp))
                    parts.append(jnp.where(lane_lt64, res[0], res[1]))
            y = jnp.concatenate(parts, axis=1) + jnp.concatenate(offs, axis=1) * e_ac
            y_refs[direction][rows, :] = y

        new = []
        for g in range(SSD_GROUPS):
            bg = bc[:, g * SSD_STATE:(g + 1) * SSD_STATE]
            new.append(_dot_tn(bg, xdw[:, g * grp_w:(g + 1) * grp_w]))
        st_refs[direction][...] = st * jnp.exp(ac_end) + jnp.concatenate(new, axis=1)

    for i in range(n_ctx):
        chunk(dt_c, dtt_c, i, 0, 0, need_ctx)
        chunk(dt_c, dtt_c, n_ctx - 1 - i, 0, 1, need_ctx)

    def lat_body(i, carry):
        chunk(dt_l, dtt_l, i, lat_row0, 0, True)
        chunk(dt_l, dtt_l, n_lat - 1 - i, lat_row0, 1, True)
        return carry

    lax.fori_loop(0, n_lat, lat_body, 0, unroll=2)

    dskip = dskip_ref[0]
    normw = normw_ref[0]

    def finish(z_ref, o_ref, c, row0):
        rows = pl.ds(_aligned(row0 + c * q, q), q)
        crow = pl.ds(_aligned(c * q, q), q)
        y = yf_s[rows, :] + yb_s[rows, :] + dskip * xs_s[rows, :]
        y = y * _silu(z_ref[0, crow, :])
        ms = jnp.mean(y * y, axis=-1, keepdims=True)
        o_ref[0, crow, :] = (y * lax.rsqrt(ms + NORM_EPS) * normw).astype(BF16)

    if need_ctx:
        for i in range(n_ctx):
            finish(z_c, out_c, i, 0)
    else:
        out_c[...] = jnp.zeros_like(out_c)

    def fin_body(c, carry):
        finish(z_l, out_l, c, lat_row0)
        return carry

    lax.fori_loop(0, n_lat, fin_body, 0)


def _ssd(z_l, xbc_l, misc_l, dtt_l, z_c, xbc_c, misc_c, dtt_c, params, layer, need_ctx):
    b, s, _ = z_l.shape
    c = z_c.shape[1]
    n_lat, n_ctx = s // SSD_CHUNK, c // SSD_CHUNK
    t = s + c
    convw, convb, bias_row, alog_row, bias_col, alog_col, dskip, normw = params

    def seq_specs(n):
        return [
            pl.BlockSpec((1, n, SSD_WIDTH), lambda bi: (bi, 0, 0)),
            pl.BlockSpec((1, n, SSD_CONV_CH), lambda bi: (bi, 0, 0)),
            pl.BlockSpec((1, n, LANES), lambda bi: (bi, 0, 1)),
            pl.BlockSpec((1, 16, n), lambda bi: (bi, 0, 0)),
        ]

    def par(shape):
        return pl.BlockSpec((1,) + shape, lambda bi: (layer,) + (0,) * len(shape))

    return pl.pallas_call(
        functools.partial(_ssd_kernel, n_lat=n_lat, n_ctx=n_ctx, need_ctx=need_ctx),
        grid=(b,),
        in_specs=seq_specs(s) + seq_specs(c) + [
            par((8, SSD_CONV_CH)), par((1, SSD_CONV_CH)), par((1, LANES)), par((1, LANES)),
            par((16, 1)), par((16, 1)), par((1, SSD_WIDTH)), par((1, SSD_WIDTH)),
        ],
        out_specs=[
            pl.BlockSpec((1, s, SSD_WIDTH), lambda bi: (bi, 0, 0)),
            pl.BlockSpec((1, c, SSD_WIDTH), lambda bi: (bi, 0, 0)),
        ],
        out_shape=(jax.ShapeDtypeStruct((b, s, SSD_WIDTH), BF16), jax.ShapeDtypeStruct((b, c, SSD_WIDTH), BF16)),
        scratch_shapes=[
            pltpu.VMEM((t, SSD_WIDTH), F32),
            pltpu.VMEM((t, SSD_WIDTH), BF16),
            pltpu.VMEM((t, SSD_WIDTH), F32),
            pltpu.VMEM((t, SSD_WIDTH), F32),
            pltpu.VMEM((SSD_STATE, SSD_WIDTH), F32),
            pltpu.VMEM((SSD_STATE, SSD_WIDTH), F32),
        ],
        compiler_params=_cparams(("parallel",)),
        name="ssd",
    )(z_l, xbc_l, misc_l, dtt_l, z_c, xbc_c, misc_c, dtt_c,
      convw, convb, bias_row, alog_row, bias_col, alog_col, dskip, normw)


def _mlaproj_kernel(cq_ref, misc_ref, cqn_ref, ckvn_ref, wuq_ref, wukv_ref, qnw_ref, knw_ref, *rest, rope):
    if rope:
        cos_ref, s1_ref, s2_ref, q_ref, k_ref, v_ref = rest
    else:
        q_ref, k_ref, v_ref = rest

    def rms(t, w):
        ms = jnp.mean(t * t, axis=-1, keepdims=True)
        return t * lax.rsqrt(ms + NORM_EPS) * w

    def head_finish(t, w, scale=None):
        ss = jnp.sum(t * t, axis=-1, keepdims=True)
        t = t * lax.rsqrt(ss * (1.0 / MLA_QK) + NORM_EPS) * w
        if rope:
            t = (t * cos_ref[...] + pltpu.roll(t, LANES - 8, axis=1) * s1_ref[...]
                 + pltpu.roll(t, 8, axis=1) * s2_ref[...])
        if scale is not None:
            t = t * scale
        return t.astype(BF16)

    misc = misc_ref[0]
    cqn = rms(cq_ref[0], cqn_ref[0]).astype(BF16)
    qf = _dot(cqn, wuq_ref[0])
    ckvn = rms(misc[:, 0:MLA_KV_RANK], ckvn_ref[0]).astype(BF16)
    kv = _dot(ckvn, wukv_ref[0])
    lane = lax.broadcasted_iota(jnp.int32, (1, LANES), 1)
    kr = jnp.where((lane >= MLA_NOPE) & (lane < MLA_QK), pltpu.roll(misc[:, LANES:2 * LANES], MLA_NOPE, axis=1), 0.0)
    qnw = qnw_ref[0]
    knw = knw_ref[0]
    for h in range(MLA_HEADS):
        sl = slice(h * MLA_HEAD_PAD, (h + 1) * MLA_HEAD_PAD)
        q_ref[0, :, sl] = head_finish(qf[:, sl], qnw, MLA_QK ** -0.5 * LOG2E)
        k_ref[0, :, sl] = head_finish(kv[:, sl] + kr, knw)
    v_ref[0] = kv[:, MLA_HEADS * MLA_HEAD_PAD:].astype(BF16)


def _mlaproj(cq, misc, weights, rope_tabs, layer, tm):
    b, s, _ = cq.shape
    cqn_w, ckvn_w, wuq_p, wukv_p, qnw_p, knw_p = weights
    rope = rope_tabs is not None
    tile = lambda w: pl.BlockSpec((1, tm, w), lambda bi, ti: (bi, ti, 0))

    def par(shape):
        return pl.BlockSpec((1,) + shape, lambda bi, ti: (layer,) + (0,) * len(shape))

    in_specs = [
        tile(MLA_Q_RANK), tile(MISC_WIDTH),
        par((1, MLA_Q_RANK)), par((1, MLA_KV_RANK)),
        par((MLA_Q_RANK, MLA_HEADS * MLA_HEAD_PAD)), par((MLA_KV_RANK, MLA_HEADS * MLA_HEAD_PAD + MLA_WIDTH)),
        par((1, LANES)), par((1, LANES)),
    ]
    args = [cq, misc, cqn_w, ckvn_w, wuq_p, wukv_p, qnw_p, knw_p]
    if rope:
        in_specs += [pl.BlockSpec((tm, LANES), lambda bi, ti: (ti, 0))] * 3
        args += list(rope_tabs)
    return pl.pallas_call(
        functools.partial(_mlaproj_kernel, rope=rope),
        grid=(b, s // tm),
        in_specs=in_specs,
        out_specs=[tile(MLA_HEADS * MLA_HEAD_PAD), tile(MLA_HEADS * MLA_HEAD_PAD), tile(MLA_WIDTH)],
        out_shape=(
            jax.ShapeDtypeStruct((b, s, MLA_HEADS * MLA_HEAD_PAD), BF16),
            jax.ShapeDtypeStruct((b, s, MLA_HEADS * MLA_HEAD_PAD), BF16),
            jax.ShapeDtypeStruct((b, s, MLA_WIDTH), BF16),
        ),
        compiler_params=_cparams(("parallel", "parallel")),
        name="mla_proj",
    )(*args)


def _mla_kernel(*refs, with_latent):
    if with_latent:
        q_ref, kl_ref, vl_ref, kc_ref, vc_ref, o_ref = refs
    else:
        q_ref, kc_ref, vc_ref, o_ref = refs
    tq = q_ref.shape[1]
    lane = lax.broadcasted_iota(jnp.int32, (1, MLA_WIDTH), 1)
    acc = jnp.zeros((tq, MLA_WIDTH), F32)
    for h in range(MLA_HEADS):
        sl = slice(h * MLA_HEAD_PAD, (h + 1) * MLA_HEAD_PAD)
        qh = q_ref[0, :, sl]
        sc = _dot_nt(qh, kc_ref[0, :, sl])
        m = sc.max(axis=-1, keepdims=True)
        if with_latent:
            sl_ = _dot_nt(qh, kl_ref[0, :, sl])
            m = jnp.maximum(m, sl_.max(axis=-1, keepdims=True))
            p_l = jnp.exp2(sl_ - m)
            l = p_l.sum(axis=-1, keepdims=True)
            o = _dot(p_l.astype(BF16), vl_ref[0])
        p_c = jnp.exp2(sc - m)
        lc = p_c.sum(axis=-1, keepdims=True)
        oc = _dot(p_c.astype(BF16), vc_ref[0])
        if with_latent:
            l = l + lc
            o = o + oc
        else:
            l, o = lc, oc
        hm = (lane >= h * MLA_V) & (lane < (h + 1) * MLA_V)
        acc = jnp.where(hm, o * (1.0 / l), acc)
    o_ref[0] = acc.astype(BF16)


def _mla_attention(q, k_l, v_l, k_c, v_c, tq):
    b, sq, wq = q.shape
    c = k_c.shape[1]
    with_latent = k_l is not None
    in_specs = [pl.BlockSpec((1, tq, wq), lambda bi, ti: (bi, ti, 0))]
    args = [q]
    if with_latent:
        s = k_l.shape[1]
        in_specs += [pl.BlockSpec((1, s, wq), lambda bi, ti: (bi, 0, 0)),
                     pl.BlockSpec((1, s, MLA_WIDTH), lambda bi, ti: (bi, 0, 0))]
        args += [k_l, v_l]
    in_specs += [pl.BlockSpec((1, c, wq), lambda bi, ti: (bi, 0, 0)),
                 pl.BlockSpec((1, c, MLA_WIDTH), lambda bi, ti: (bi, 0, 0))]
    args += [k_c, v_c]
    return pl.pallas_call(
        functools.partial(_mla_kernel, with_latent=with_latent),
        grid=(b, sq // tq),
        in_specs=in_specs,
        out_specs=pl.BlockSpec((1, tq, MLA_WIDTH), lambda bi, ti: (bi, ti, 0)),
        out_shape=jax.ShapeDtypeStruct((b, sq, MLA_WIDTH), BF16),
        compiler_params=_cparams(("parallel", "parallel")),
        name="mla_attn" if with_latent else "mla_ctx_attn",
    )(*args)


def _outffn_kernel(x_ref, na_ref, ssd_ref, mla_ref, mod_ref, nw_ref, wo_ref, w1_ref, w2_ref, o_ref,
                   x1_s, xn_s, acc_s):
    f = pl.program_id(2)
    mod = mod_ref[0, 0]

    @pl.when(f == 0)
    def _():
        mix = (_dot(na_ref[0], wo_ref[0, 0:NA_WIDTH, :])
               + _dot(ssd_ref[0], wo_ref[0, NA_WIDTH:NA_WIDTH + SSD_WIDTH, :])
               + _dot(mla_ref[0], wo_ref[0, NA_WIDTH + SSD_WIDTH:, :]))
        x1 = x_ref[0] + mod[2:3, :] * mix
        x1_s[...] = x1
        ms = jnp.mean(x1 * x1, axis=-1, keepdims=True)
        xn = x1 * lax.rsqrt(ms + NORM_EPS) * nw_ref[0]
        xn_s[...] = (xn * (1.0 + mod[4:5, :]) + mod[3:4, :]).astype(BF16)
        acc_s[...] = jnp.zeros_like(acc_s)

    hmid = _dot(xn_s[...], w1_ref[0])
    hmid = jnp.square(jnp.maximum(hmid, 0.0)).astype(BF16)
    acc_s[...] += _dot(hmid, w2_ref[0])

    @pl.when(f == pl.num_programs(2) - 1)
    def _():
        o_ref[0] = x1_s[...] + mod[5:6, :] * acc_s[...]


def _outffn(xs, na, ssd, mla, mod, layer, mod_row_fn, norm_w, w_out, w1, w2, tm, tf):
    b, s, d = xs.shape
    depth = w_out.shape[0]
    tile = lambda w: pl.BlockSpec((1, tm, w), lambda bi, ti, fi: (bi, ti, 0))
    return pl.pallas_call(
        _outffn_kernel,
        grid=(b, s // tm, D_FF // tf),
        in_specs=[
            tile(d), tile(NA_WIDTH), tile(SSD_WIDTH), tile(MLA_WIDTH),
            pl.BlockSpec((1, 1, 6, d), lambda bi, ti, fi: (layer, mod_row_fn(bi), 0, 0)),
            pl.BlockSpec((1, 1, d), lambda bi, ti, fi: (layer, 0, 0)),
            pl.BlockSpec((1, MIX_WIDTH, d), lambda bi, ti, fi: (layer, 0, 0)),
            pl.BlockSpec((1, d, tf), lambda bi, ti, fi: (layer, 0, fi)),
            pl.BlockSpec((1, tf, d), lambda bi, ti, fi: (layer, fi, 0)),
        ],
        out_specs=tile(d),
        out_shape=jax.ShapeDtypeStruct((b, s, d), F32),
        scratch_shapes=[pltpu.VMEM((tm, d), F32), pltpu.VMEM((tm, d), BF16), pltpu.VMEM((tm, d), F32)],
        compiler_params=_cparams(("parallel", "parallel", "arbitrary")),
        name="outproj_mlp",
    )(xs, na, ssd, mla, mod, norm_w.reshape(depth, 1, d), w_out, w1, w2)


def _rope_tables(n_tokens):
    pos = jnp.arange(n_tokens)
    axes = jnp.stack([pos // GRID_W, pos % GRID_W], axis=-1).astype(F32)
    n_freq = MLA_ROPE // 4
    inv_freq = ROPE_THETA ** (-jnp.arange(n_freq, dtype=F32) / n_freq)
    ang = axes[:, :, None] * inv_freq
    cos, sin = jnp.cos(ang), jnp.sin(ang)
    zeros = jnp.zeros_like(sin)
    cos_l = jnp.stack([cos, cos], axis=2).reshape(n_tokens, MLA_ROPE)
    s1_l = jnp.stack([-sin, zeros], axis=2).reshape(n_tokens, MLA_ROPE)
    s2_l = jnp.stack([zeros, sin], axis=2).reshape(n_tokens, MLA_ROPE)

    def place(t, fill):
        left = jnp.full((n_tokens, MLA_NOPE), fill, F32)
        right = jnp.full((n_tokens, LANES - MLA_QK), fill, F32)
        return jnp.concatenate([left, t, right], axis=1)

    return place(cos_l, 1.0), place(s1_l, 0.0), place(s2_l, 0.0)


def _prep_params(w_in, na_qn_w, na_kn_w, ssd_conv_w, ssd_conv_b, ssd_dt_bias, ssd_a_log, ssd_d, ssd_norm_w,
                 mla_cq_norm_w, mla_ckv_norm_w, mla_w_uq, mla_w_ukv, mla_qn_w, mla_kn_w):
    depth = w_in.shape[0]
    d = w_in.shape[1]
    dtw = w_in[..., 2304:2320]
    w_in_p = jnp.concatenate(
        [w_in[..., :2304], w_in[..., 2320:2576], w_in[..., 2576:2704], w_in[..., 2704:2736]]
        + [dtw] * DT_COPIES + [jnp.zeros((depth, d, IN_WIDTH_PAD - 2720 - 16 * DT_COPIES), F32)], axis=-1).astype(BF16)
    qnw = jnp.tile(na_qn_w, (1, NA_HEADS))[:, None, :]
    knw = jnp.tile(na_kn_w, (1, NA_HEADS))[:, None, :]
    gi = np.arange(NA_WIDTH) // NA_HEAD_DIM
    grp = jnp.asarray((gi[:, None] == gi[None, :]).astype(np.float32), BF16)

    convw = jnp.pad(ssd_conv_w, ((0, 0), (0, 8 - SSD_CONV), (0, 0)))
    convb = ssd_conv_b[:, None, :]

    def dt_row(t):
        flat = jnp.tile(t.reshape(depth, 16), (1, DT_COPIES))
        return jnp.pad(flat, ((0, 0), (DT_LANE0, LANES - DT_LANE0 - 16 * DT_COPIES)))[:, None, :]

    bias_row = dt_row(ssd_dt_bias)
    alog_row = dt_row(ssd_a_log)
    bias_col = ssd_dt_bias.reshape(depth, 16, 1)
    alog_col = ssd_a_log.reshape(depth, 16, 1)
    dskip = jnp.repeat(ssd_d, SSD_HEAD_DIM, axis=-1)[:, None, :]
    normw = ssd_norm_w[:, None, :]
    ssd_params = (convw, convb, bias_row, alog_row, bias_col, alog_col, dskip, normw)

    wuq = mla_w_uq.reshape(depth, MLA_Q_RANK, MLA_HEADS, MLA_QK)
    wuq_p = jnp.pad(wuq, ((0, 0), (0, 0), (0, 0), (0, MLA_HEAD_PAD - MLA_QK))).reshape(depth, MLA_Q_RANK, -1).astype(BF16)
    wukv = mla_w_ukv.reshape(depth, MLA_KV_RANK, MLA_HEADS, MLA_NOPE + MLA_V)
    wk = jnp.pad(wukv[..., :MLA_NOPE], ((0, 0), (0, 0), (0, 0), (0, MLA_HEAD_PAD - MLA_NOPE))).reshape(depth, MLA_KV_RANK, -1)
    wv = wukv[..., MLA_NOPE:].reshape(depth, MLA_KV_RANK, -1)
    wukv_p = jnp.concatenate([wk, wv], axis=-1).astype(BF16)
    pad_head = lambda t: jnp.pad(t, ((0, 0), (0, MLA_HEAD_PAD - MLA_QK)))[:, None, :]
    mla_weights = (mla_cq_norm_w[:, None, :], mla_ckv_norm_w[:, None, :], wuq_p, wukv_p,
                   pad_head(mla_qn_w), pad_head(mla_kn_w))
    return w_in_p, qnw, knw, grp, ssd_params, mla_weights


def kernel(x, c, ctx, c_ctx, w_ada, b_ada, norm1_w, norm2_w, w_in, w_out, na_qn_w, na_kn_w, na_rpb, ssd_conv_w, ssd_conv_b, ssd_dt_bias, ssd_a_log, ssd_d, ssd_norm_w, mla_cq_norm_w, mla_ckv_norm_w, mla_w_uq, mla_w_ukv, mla_qn_w, mla_kn_w, w_ff1, w_ff2):
    b, s, d = x.shape
    ctx_len = ctx.shape[1]
    depth = w_in.shape[0]
    n_rows = s // GRID_W
    assert d == D_MODEL and s % (4 * GRID_W) == 0 and n_rows > NA_WIN_ROWS and ctx_len % SSD_CHUNK == 0

    mod_rows = ((b + 1 + 7) // 8) * 8
    cc = jnp.concatenate([c, c_ctx[None, :], jnp.zeros((mod_rows - b - 1, d), F32)], axis=0)
    mod = _adaln(cc, w_ada, b_ada).reshape(depth, mod_rows, 6, d)

    w_in_p, qnw, knw, grp, ssd_params, mla_weights = _prep_params(
        w_in, na_qn_w, na_kn_w, ssd_conv_w, ssd_conv_b, ssd_dt_bias, ssd_a_log, ssd_d, ssd_norm_w,
        mla_cq_norm_w, mla_ckv_norm_w, mla_w_uq, mla_w_ukv, mla_qn_w, mla_kn_w)
    w_out_b = w_out.astype(BF16)
    w1_b = w_ff1.astype(BF16)
    w2_b = w_ff2.astype(BF16)
    bias_all = _na_bias(na_rpb, n_rows)
    rope_tabs = _rope_tables(s)

    lat_row = lambda bi: bi
    ctx_row = lambda bi: b
    tm = 512
    tm_c = min(tm, b * ctx_len)
    ctx_flat = ctx.reshape(1, b * ctx_len, d)
    dt_lo = LANES + DT_LANE0 + 32

    for i in range(depth):
        need_ctx = i < depth - 1
        qkv_l, z_l, xbc_l, cq_l, misc_l = _inproj(x, mod, i, lat_row, norm1_w, w_in_p, qnw, knw, grp, tm)
        outs_c = _inproj(ctx_flat, mod, i, ctx_row, norm1_w, w_in_p, qnw, knw, grp, tm_c)
        qkv_c, z_c, xbc_c, cq_c, misc_c = [t.reshape(b, ctx_len, t.shape[-1]) for t in outs_c]

        na_l = _na_attention(qkv_l, qkv_c, bias_all, i)

        dtt_l = jnp.swapaxes(misc_l[:, :, dt_lo:dt_lo + 16], 1, 2)
        dtt_c = jnp.swapaxes(misc_c[:, :, dt_lo:dt_lo + 16], 1, 2)
        ssd_l, ssd_c = _ssd(z_l, xbc_l, misc_l, dtt_l, z_c, xbc_c, misc_c, dtt_c, ssd_params, i, need_ctx)

        qm_l, km_l, vm_l = _mlaproj(cq_l, misc_l, mla_weights, rope_tabs, i, tm)
        qm_c, km_c, vm_c = _mlaproj(cq_c, misc_c, mla_weights, None, i, ctx_len)
        mla_l = _mla_attention(qm_l, km_l, vm_l, km_c, vm_c, 512)

        x = _outffn(x, na_l, ssd_l, mla_l, mod, i, lat_row, norm2_w, w_out_b, w1_b, w2_b, 2 * tm, 1024)
        if need_ctx:
            na_c = _na_ctx_attention(qkv_c)
            mla_c = _mla_attention(qm_c, None, None, km_c, vm_c, ctx_len)
            flat = lambda t: t.reshape(1, b * ctx_len, t.shape[-1])
            ctx_flat = _outffn(ctx_flat, flat(na_c), flat(ssd_c), flat(mla_c), mod, i, ctx_row, norm2_w,
                               w_out_b, w1_b, w2_b, tm_c, 1024)
    return x
```

```python
import functools

import jax
import jax.numpy as jnp
import numpy as np
from jax import lax
from jax.experimental import pallas as pl
from jax.experimental.pallas import tpu as pltpu

F32 = jnp.float32
BF16 = jnp.bfloat16

D_MODEL = 1024
GRID_W = 64
D_FF = 4 * D_MODEL
NORM_EPS = 1e-6
ROPE_THETA = 10000.0
LOG2E = 1.4426950408889634
NA_HEADS = 4
NA_HEAD_DIM = 64
NA_WIDTH = 256
NA_WIN_ROWS = 8
NA_WIN_COLS = 16
SSD_HEADS = 8
SSD_HEAD_DIM = 64
SSD_WIDTH = 512
SSD_GROUPS = 2
SSD_STATE = 128
SSD_CONV = 5
SSD_CHUNK = 128
SSD_CONV_CH = 1024
MLA_HEADS = 4
MLA_NOPE = 64
MLA_ROPE = 32
MLA_V = 64
MLA_QK = 96
MLA_Q_RANK = 256
MLA_KV_RANK = 128
MLA_WIDTH = 256
MIX_WIDTH = 1024
IN_WIDTH = 2736

LANES = 128
MLA_HEAD_PAD = 128
IN_WIDTH_PAD = 2816
DT_LANE0 = 32
DT_COPIES = 3
MLA_KEY_CHUNK = 256
VMEM_LIMIT = 56 * 1024 * 1024

NT_DIMS = (((1,), (1,)), ((), ()))
TN_DIMS = (((0,), (0,)), ((), ()))


def _dot(a, b):
    return jnp.dot(a, b, preferred_element_type=F32)


def _dot_nt(a, b):
    return lax.dot_general(a, b, NT_DIMS, preferred_element_type=F32)


def _dot_tn(a, b):
    return lax.dot_general(a, b, TN_DIMS, preferred_element_type=F32)


def _sigmoid(x):
    return 1.0 / (1.0 + jnp.exp(-x))


def _silu(x):
    return x * _sigmoid(x)


def _softplus(x):
    return jnp.maximum(x, 0.0) + jnp.log1p(jnp.exp(-jnp.abs(x)))


def _split3(x):
    x0 = x.astype(BF16)
    r = x - x0.astype(F32)
    x1 = r.astype(BF16)
    r = r - x1.astype(F32)
    return x0, x1, r.astype(BF16)


def _lane_parts(x, lane):
    p0 = x.astype(BF16).astype(F32)
    r = x - p0
    p1 = r.astype(BF16).astype(F32)
    p2 = r - p1
    return jnp.where(lane < DT_LANE0 + 16, p0, jnp.where(lane < DT_LANE0 + 32, p1, p2)).astype(BF16)


def _aligned(x, m):
    return x if isinstance(x, int) else pl.multiple_of(x, m)


def _cparams(sem):
    return pltpu.CompilerParams(dimension_semantics=sem, vmem_limit_bytes=VMEM_LIMIT)


def _adaln_kernel(c_ref, w_ref, b_ref, o_ref):
    act = _silu(c_ref[...]).astype(BF16)
    o_ref[0] = _dot(act, w_ref[0].astype(BF16)) + b_ref[0]


def _adaln(cc, w_ada, b_ada):
    depth, d, n = w_ada.shape
    rows = cc.shape[0]
    tn = 1536
    return pl.pallas_call(
        _adaln_kernel,
        grid=(depth, n // tn),
        in_specs=[
            pl.BlockSpec((rows, d), lambda l, j: (0, 0)),
            pl.BlockSpec((1, d, tn), lambda l, j: (l, 0, j)),
            pl.BlockSpec((1, 1, tn), lambda l, j: (l, 0, j)),
        ],
        out_specs=pl.BlockSpec((1, rows, tn), lambda l, j: (l, 0, j)),
        out_shape=jax.ShapeDtypeStruct((depth, rows, n), F32),
        compiler_params=_cparams(("parallel", "parallel")),
        name="adaln",
    )(cc, w_ada, b_ada.reshape(depth, 1, n))


def _inproj_kernel(x_ref, mod_ref, nw_ref, w_ref, qnw_ref, knw_ref, grp_ref,
                   cqn_ref, ckvn_ref, wuq_ref, wuk_ref, wuvt_ref, mqnw_ref, mknw_ref, g96_ref, swp_ref, *rest, rope):
    if rope:
        cos_ref, sin_ref = rest[:2]
        rest = rest[2:]
    qkv_ref, z_ref, xbc_ref, dt_ref, qm_ref, km_ref, vt_ref = rest

    def rms(t, w):
        ms = jnp.mean(t * t, axis=-1, keepdims=True)
        return t * lax.rsqrt(ms + NORM_EPS) * w

    mod = mod_ref[0, 0]
    h = (rms(x_ref[0], nw_ref[0]) * (1.0 + mod[1:2, :]) + mod[0:1, :]).astype(BF16)
    u = _dot(h, w_ref[0])

    grp = grp_ref[...]

    def head_norm(t, w):
        t2 = t * t
        hi = t2.astype(BF16)
        lo = (t2 - hi.astype(F32)).astype(BF16)
        ss = _dot(hi, grp) + _dot(lo, grp)
        return t * lax.rsqrt(ss * (1.0 / NA_HEAD_DIM) + NORM_EPS) * w

    q = head_norm(u[:, 0:256], qnw_ref[0]) * (NA_HEAD_DIM ** -0.5 * LOG2E)
    k = head_norm(u[:, 256:512], knw_ref[0])
    qkv_ref[0, :, 0:256] = q.astype(BF16)
    qkv_ref[0, :, 256:512] = k.astype(BF16)
    qkv_ref[0, :, 512:768] = u[:, 512:768].astype(BF16)
    z_ref[0] = u[:, 768:1280].astype(BF16)
    xbc_ref[0] = u[:, 1280:2304].astype(BF16)
    tail = u[:, 2688:2816]
    dt_ref[0] = tail

    g96 = g96_ref[...]
    swp = swp_ref[...]
    nh = MLA_HEADS

    def inv_rms96(ss):
        return lax.rsqrt(ss * (1.0 / MLA_QK) + NORM_EPS)

    def rotate(t):
        if not rope:
            return t
        reps = t.shape[1] // LANES
        sw = swp[:t.shape[1], :t.shape[1]]
        hi = t.astype(BF16)
        lo = (t - hi.astype(F32)).astype(BF16)
        partner = _dot(hi, sw) + _dot(lo, sw)
        return t * jnp.concatenate([cos_ref[...]] * reps, axis=1) + partner * jnp.concatenate([sin_ref[...]] * reps, axis=1)

    qf = _dot(rms(u[:, 2304:2560], cqn_ref[0]).astype(BF16), wuq_ref[0])
    ckvn = rms(u[:, 2560:2688], ckvn_ref[0]).astype(BF16)
    kn = _dot(ckvn, wuk_ref[0])
    vt_ref[0] = _dot_nt(wuvt_ref[0], ckvn).astype(BF16)
    lane = lax.broadcasted_iota(jnp.int32, (1, LANES), 1)
    kr = jnp.where((lane >= MLA_NOPE) & (lane < MLA_QK), pltpu.roll(tail, MLA_NOPE, axis=1), 0.0)
    mqnw = mqnw_ref[0]
    mknw = mknw_ref[0]

    q_ss = _dot((qf * qf).astype(BF16), g96)
    qn = rotate(qf * inv_rms96(q_ss) * mqnw)
    qm_ref[0] = (qn * (MLA_QK ** -0.5 * LOG2E)).astype(BF16)
    kr_ss = jnp.sum(kr * kr, axis=-1, keepdims=True)
    kr_rot = rotate(kr * mknw[:, :LANES])
    k_ss = _dot((kn * kn).astype(BF16), g96) + kr_ss
    km_ref[0] = ((kn * mknw + jnp.concatenate([kr_rot] * nh, axis=1)) * inv_rms96(k_ss)).astype(BF16)


def _inproj(xs, mod, layer, mod_row_fn, norm_w, w_in_p, qnw, knw, grp, mla_weights, rope_tabs, tm):
    b, s, d = xs.shape
    depth = w_in_p.shape[0]
    cqn_w, ckvn_w, wuq_p, wuk_p, wuvt_p, mqnw_p, mknw_p, g96, swp = mla_weights
    rope = rope_tabs is not None
    hw = MLA_HEADS * MLA_HEAD_PAD
    outs = (
        jax.ShapeDtypeStruct((b, s, 768), BF16),
        jax.ShapeDtypeStruct((b, s, SSD_WIDTH), BF16),
        jax.ShapeDtypeStruct((b, s, SSD_CONV_CH), BF16),
        jax.ShapeDtypeStruct((b, s, LANES), F32),
        jax.ShapeDtypeStruct((b, s, hw), BF16),
        jax.ShapeDtypeStruct((b, s, hw), BF16),
        jax.ShapeDtypeStruct((b, MLA_WIDTH, s), BF16),
    )
    tile = lambda w: pl.BlockSpec((1, tm, w), lambda bi, ti: (bi, ti, 0))
    tile_t = lambda r: pl.BlockSpec((1, r, tm), lambda bi, ti: (bi, 0, ti))

    def par(shape):
        return pl.BlockSpec((1,) + shape, lambda bi, ti: (layer,) + (0,) * len(shape))

    in_specs = [
        tile(d),
        pl.BlockSpec((1, 1, 6, d), lambda bi, ti: (layer, mod_row_fn(bi), 0, 0)),
        par((1, d)), par((d, IN_WIDTH_PAD)), par((1, 256)), par((1, 256)),
        pl.BlockSpec((256, 256), lambda bi, ti: (0, 0)),
        par((1, MLA_Q_RANK)), par((1, MLA_KV_RANK)), par((MLA_Q_RANK, hw)), par((MLA_KV_RANK, hw)),
        par((MLA_WIDTH, MLA_KV_RANK)),
        par((1, hw)), par((1, hw)),
        pl.BlockSpec((hw, hw), lambda bi, ti: (0, 0)), pl.BlockSpec((hw, hw), lambda bi, ti: (0, 0)),
    ]
    args = [xs, mod, norm_w.reshape(depth, 1, d), w_in_p, qnw, knw, grp, cqn_w, ckvn_w, wuq_p, wuk_p, wuvt_p, mqnw_p,
            mknw_p, g96, swp]
    if rope:
        in_specs += [pl.BlockSpec((tm, LANES), lambda bi, ti: (ti, 0))] * 2
        args += list(rope_tabs)
    return pl.pallas_call(
        functools.partial(_inproj_kernel, rope=rope),
        grid=(b, s // tm),
        in_specs=in_specs,
        out_specs=[tile(768), tile(SSD_WIDTH), tile(SSD_CONV_CH), tile(LANES), tile(hw), tile(hw), tile_t(MLA_WIDTH)],
        out_shape=outs,
        compiler_params=_cparams(("parallel", "parallel")),
        name="inproj",
    )(*args)


NA_VARIANT_ROWS = (0, 1, 2, 3, 4, -3, -2, -1)


def _na_bias_kernel(rpb_ref, o_ref, t_ref, *, n_rows):
    nrow_off = 2 * NA_WIN_ROWS - 1
    ncol_off = 2 * NA_WIN_COLS - 1
    base = (pl.program_id(0) * NA_HEADS + pl.program_id(1)) * (nrow_off * ncol_off)
    qi = lax.broadcasted_iota(jnp.int32, (GRID_W, LANES), 0)
    li = lax.broadcasted_iota(jnp.int32, (GRID_W, LANES), 1)
    kcol = li & (GRID_W - 1)
    colidx = jnp.clip(kcol - qi, -(NA_WIN_COLS - 1), NA_WIN_COLS - 1) + (NA_WIN_COLS - 1)
    cstart = jnp.clip(qi - NA_WIN_COLS // 2, 0, GRID_W - NA_WIN_COLS)
    valid = (kcol >= cstart) & (kcol < cstart + NA_WIN_COLS)
    for d in range(nrow_off):
        acc = jnp.zeros((GRID_W, LANES), F32)
        for j in range(ncol_off):
            acc = jnp.where(colidx == j, rpb_ref[base + d * ncol_off + j], acc)
        t_ref[d] = jnp.where(valid, acc * LOG2E, -jnp.inf)
    for v, r_rep in enumerate(NA_VARIANT_ROWS):
        r = r_rep if r_rep >= 0 else n_rows + r_rep
        rs = min(max(r - NA_WIN_ROWS // 2, 0), n_rows - NA_WIN_ROWS)
        for p in range(NA_WIN_ROWS // 2):
            d0 = rs + 2 * p - r + (NA_WIN_ROWS - 1)
            o_ref[0, v, :, p * LANES:(p + 1) * LANES] = jnp.where(li < GRID_W, t_ref[d0], t_ref[d0 + 1])


def _na_bias(rpb, n_rows):
    depth, heads = rpb.shape[0], rpb.shape[1]
    n_win = NA_WIN_ROWS * GRID_W
    return pl.pallas_call(
        functools.partial(_na_bias_kernel, n_rows=n_rows),
        grid=(depth, heads),
        in_specs=[pl.BlockSpec(memory_space=pltpu.SMEM)],
        out_specs=pl.BlockSpec((1, 8, GRID_W, n_win), lambda l, h: (l, 0, h, 0)),
        out_shape=jax.ShapeDtypeStruct((depth, 8, heads * GRID_W, n_win), F32),
        scratch_shapes=[pltpu.VMEM((2 * NA_WIN_ROWS - 1, GRID_W, LANES), F32)],
        compiler_params=_cparams(("parallel", "parallel")),
        name="na_bias",
    )(rpb.reshape(-1))


def _stacked_heads_attention(q, key_vals, biases):
    m_rows = q.shape[0]
    lane = lax.broadcasted_iota(jnp.int32, (1, NA_WIDTH), 1)
    hms = [jnp.where((lane >= h * NA_HEAD_DIM) & (lane < (h + 1) * NA_HEAD_DIM), 1.0, 0.0) for h in range(NA_HEADS)]
    qs = jnp.concatenate([q * hm.astype(BF16) for hm in hms], axis=0)
    scores = []
    for (k, _), b in zip(key_vals, biases):
        s = _dot_nt(qs, k)
        scores.append(s if b is None else s + b)
    m = scores[0].max(axis=-1, keepdims=True)
    for s in scores[1:]:
        m = jnp.maximum(m, s.max(axis=-1, keepdims=True))
    l = None
    o = None
    for s, (_, v) in zip(scores, key_vals):
        p = jnp.exp2(s - m)
        ls = p.sum(axis=-1, keepdims=True)
        os_ = _dot(p.astype(BF16), v)
        l = ls if l is None else l + ls
        o = os_ if o is None else o + os_
    o = o * (1.0 / l)
    out = o[0:m_rows] * hms[0]
    for h in range(1, NA_HEADS):
        out = out + o[h * m_rows:(h + 1) * m_rows] * hms[h]
    return out


def _na_kernel(q_ref, k_ref, v_ref, kc_ref, vc_ref, bias_ref, o_ref, *, rows_per_step, n_rows):
    blk = pl.program_id(1)
    kc = kc_ref[0]
    vc = vc_ref[0]
    n_win = NA_WIN_ROWS * GRID_W
    for i in range(rows_per_step):
        r = blk * rows_per_step + i
        rs = jnp.clip(r - NA_WIN_ROWS // 2, 0, n_rows - NA_WIN_ROWS)
        var = jnp.where(r < NA_WIN_ROWS // 2, r,
                        jnp.where(r > n_rows - NA_WIN_ROWS // 2, r - (n_rows - NA_WIN_ROWS), NA_WIN_ROWS // 2))
        start = pl.multiple_of(rs * GRID_W, GRID_W)
        kw = k_ref[0, pl.ds(start, n_win), :]
        vw = v_ref[0, pl.ds(start, n_win), :]
        q = q_ref[0, i * GRID_W:(i + 1) * GRID_W, :]
        out = _stacked_heads_attention(q, [(kw, vw), (kc, vc)], [bias_ref[0, var], None])
        o_ref[0, i * GRID_W:(i + 1) * GRID_W, :] = out.astype(BF16)


def _na_attention(qkv_l, qkv_c, bias_all, layer, rows_per_step=4):
    b, s, _ = qkv_l.shape
    ctx_len = qkv_c.shape[1]
    n_rows = s // GRID_W
    tq = rows_per_step * GRID_W
    return pl.pallas_call(
        functools.partial(_na_kernel, rows_per_step=rows_per_step, n_rows=n_rows),
        grid=(b, n_rows // rows_per_step),
        in_specs=[
            pl.BlockSpec((1, tq, 256), lambda bi, ti: (bi, ti, 0)),
            pl.BlockSpec((1, s, 256), lambda bi, ti: (bi, 0, 1)),
            pl.BlockSpec((1, s, 256), lambda bi, ti: (bi, 0, 2)),
            pl.BlockSpec((1, ctx_len, 256), lambda bi, ti: (bi, 0, 1)),
            pl.BlockSpec((1, ctx_len, 256), lambda bi, ti: (bi, 0, 2)),
            pl.BlockSpec((1,) + bias_all.shape[1:], lambda bi, ti: (layer, 0, 0, 0)),
        ],
        out_specs=pl.BlockSpec((1, tq, 256), lambda bi, ti: (bi, ti, 0)),
        out_shape=jax.ShapeDtypeStruct((b, s, NA_WIDTH), BF16),
        compiler_params=_cparams(("parallel", "parallel")),
        name="na_attn",
    )(qkv_l, qkv_l, qkv_l, qkv_c, qkv_c, bias_all)


def _na_ctx_kernel(q_ref, k_ref, v_ref, o_ref):
    o_ref[0] = _stacked_heads_attention(q_ref[0], [(k_ref[0], v_ref[0])], [None]).astype(BF16)


def _na_ctx_attention(qkv_c):
    b, c, _ = qkv_c.shape
    spec = lambda j: pl.BlockSpec((1, c, 256), lambda bi: (bi, 0, j))
    return pl.pallas_call(
        _na_ctx_kernel,
        grid=(b,),
        in_specs=[spec(0), spec(1), spec(2)],
        out_specs=spec(0),
        out_shape=jax.ShapeDtypeStruct((b, c, NA_WIDTH), BF16),
        compiler_params=_cparams(("parallel",)),
        name="na_ctx_attn",
    )(qkv_c, qkv_c, qkv_c)


def _ssd_kernel(z_l, xbc_l, dt_l, z_c, xbc_c, dt_c,
                convw_ref, convb_ref, bias_row_ref, alog_row_ref, dskip_ref, normw_ref, out_l, out_c,
                xs_s, bc_s, yf_s, yb_s, stf_s, stb_s, *, n_lat, n_ctx, need_ctx):
    q = SSD_CHUNK
    lat_row0 = n_ctx * q

    convw = convw_ref[0]
    convb = convb_ref[0]
    rowi = lax.broadcasted_iota(jnp.int32, (q, 1), 0)
    win = q + 16

    def conv_chunk(xref, lo, skip, off, first, last, dst):
        w_in = xref[0, pl.ds(lo, q + 32), :].astype(F32)[skip:skip + win]
        acc = jnp.broadcast_to(convb, (q, SSD_CONV_CH))
        for j in range(SSD_CONV):
            sh = (-(off + j - SSD_CONV // 2)) % win
            r = (pltpu.roll(w_in, sh, axis=0) if sh else w_in)[0:q]
            lo_rows = SSD_CONV // 2 - j
            if first and lo_rows > 0:
                r = jnp.where(rowi >= lo_rows, r, 0.0)
            if last and lo_rows < 0:
                r = jnp.where(rowi < q + lo_rows, r, 0.0)
            acc = acc + r * convw[j:j + 1, :]
        act = _silu(acc)
        xs_s[pl.ds(dst, q), :] = act[:, :SSD_WIDTH]
        bc_s[pl.ds(dst, q), :] = act[:, SSD_WIDTH:].astype(BF16)

    def conv_seq(xref, n, row0):
        conv_chunk(xref, 0, 0, 0, True, False, row0)
        conv_chunk(xref, (n - 1) * q - 32, 16, 16, False, True, row0 + (n - 1) * q)
        if n > 2:
            def body(c, carry):
                conv_chunk(xref, pl.multiple_of(c * q - 16, 16), 8, 8, False, False, pl.multiple_of(row0 + c * q, q))
                return carry
            lax.fori_loop(1, n - 1, body, 0)

    conv_seq(xbc_c, n_ctx, 0)
    conv_seq(xbc_l, n_lat, lat_row0)

    lane = lax.broadcasted_iota(jnp.int32, (1, LANES), 1)
    dmask = (lane >= DT_LANE0) & (lane < DT_LANE0 + 16 * DT_COPIES)
    bias_row = bias_row_ref[0]
    a_row = jnp.where(dmask, -jnp.exp(alog_row_ref[0]), 0.0)
    ri = lax.broadcasted_iota(jnp.int32, (q, q), 0)
    ci = lax.broadcasted_iota(jnp.int32, (q, q), 1)
    lo_mask = ci <= ri
    up_mask = ri <= ci
    tri_lo = jnp.where(lo_mask, 1.0, 0.0).astype(BF16)
    tri_up = jnp.where(up_mask, 1.0, 0.0).astype(BF16)
    sel_r = lax.broadcasted_iota(jnp.int32, (LANES, SSD_HEADS * LANES), 0)
    sel_c = lax.broadcasted_iota(jnp.int32, (LANES, SSD_HEADS * LANES), 1)
    sel64_r = lax.broadcasted_iota(jnp.int32, (LANES, SSD_WIDTH), 0)
    sel64_c = lax.broadcasted_iota(jnp.int32, (LANES, SSD_WIDTH), 1)
    lane_lt64 = lax.broadcasted_iota(jnp.int32, (1, LANES), 1) < SSD_HEAD_DIM

    def sel_matrices(direction):
        in3 = (sel_r >= DT_LANE0) & (sel_r < DT_LANE0 + 48)
        sel128 = jnp.where(in3 & (((sel_r - DT_LANE0) & 15) == direction * SSD_HEADS + (sel_c >> 7)), 1.0, 0.0)
        in2 = (sel64_r >= DT_LANE0) & (sel64_r < DT_LANE0 + 32)
        sel64 = jnp.where(in2 & (((sel64_r - DT_LANE0) & 15) == direction * SSD_HEADS + (sel64_c >> 6)), 1.0, 0.0)
        return sel128.astype(BF16), sel64.astype(BF16)

    sels = (sel_matrices(0), sel_matrices(1))
    tri3_col = (jnp.concatenate([tri_lo] * 3, axis=1), jnp.concatenate([tri_up] * 3, axis=1))
    masks = (lo_mask, up_mask)

    y_refs = (yf_s, yb_s)
    st_refs = (stf_s, stb_s)
    stf_s[...] = jnp.zeros_like(stf_s)
    stb_s[...] = jnp.zeros_like(stb_s)

    def chunk(dt_ref, c, row0, direction, compute_y):
        cq_ = _aligned(c * q, q)
        rows = pl.ds(_aligned(row0 + c * q, q), q)
        x = xs_s[rows, :]
        bc = bc_s[rows, :]
        raw = dt_ref[0, pl.ds(cq_, q), :]
        sel128, sel64 = sels[direction]

        dt = jnp.where(dmask, _softplus(raw + bias_row), 0.0)
        adt = dt * a_row
        a0, a1, a2 = _split3(adt)
        acum = _dot(tri3_col[direction], jnp.concatenate([a0, a1, a2], axis=0))
        a_exp = _dot(_lane_parts(acum, lane), sel128)
        dt64 = _dot(_lane_parts(dt, lane), sel64)
        ac64 = jnp.concatenate(
            [jnp.where(lane_lt64, a_exp[:, (2 * j) * LANES:(2 * j + 1) * LANES],
                       a_exp[:, (2 * j + 1) * LANES:(2 * j + 2) * LANES]) for j in range(SSD_HEADS // 2)], axis=1)

        acum_t = acum.T

        end = q - 1 if direction == 0 else 0
        ac_end = ac64[end:end + 1, :]
        xd = x * dt64
        xdb = xd.astype(BF16)
        xdw = (xd * jnp.exp(ac_end - ac64)).astype(BF16)
        st = st_refs[direction][...]
        grp_w = SSD_WIDTH // SSD_GROUPS

        if compute_y:
            stb = st.astype(BF16)
            e_ac = jnp.exp(ac64)
            parts = []
            offs = []
            for g in range(SSD_GROUPS):
                bg = bc[:, g * SSD_STATE:(g + 1) * SSD_STATE]
                cg = bc[:, (SSD_GROUPS + g) * SSD_STATE:(SSD_GROUPS + g + 1) * SSD_STATE]
                gmat = _dot_nt(cg, bg)
                offs.append(_dot(cg, stb[:, g * grp_w:(g + 1) * grp_w]))
                for j in range(2):
                    h0 = g * 4 + 2 * j
                    xp = xdb[:, h0 * SSD_HEAD_DIM:(h0 + 2) * SSD_HEAD_DIM]
                    res = []
                    for hh in (h0, h0 + 1):
                        row = DT_LANE0 + direction * SSD_HEADS + hh
                        dmat = a_exp[:, hh * LANES:(hh + 1) * LANES] - acum_t[row:row + 1, :]
                        lmat = jnp.exp(jnp.where(masks[direction], dmat, -jnp.inf))
                        res.append(_dot((gmat * lmat).astype(BF16), xp))
                    parts.append(jnp.where(lane_lt64, res[0], res[1]))
            y = jnp.concatenate(parts, axis=1) + jnp.concatenate(offs, axis=1) * e_ac
            y_refs[direction][rows, :] = y

        new = []
        for g in range(SSD_GROUPS):
            bg = bc[:, g * SSD_STATE:(g + 1) * SSD_STATE]
            new.append(_dot_tn(bg, xdw[:, g * grp_w:(g + 1) * grp_w]))
        st_refs[direction][...] = st * jnp.exp(ac_end) + jnp.concatenate(new, axis=1)

    for i in range(n_ctx):
        chunk(dt_c, i, 0, 0, need_ctx)
        chunk(dt_c, n_ctx - 1 - i, 0, 1, need_ctx)

    def lat_body(i, carry):
        chunk(dt_l, i, lat_row0, 0, True)
        chunk(dt_l, n_lat - 1 - i, lat_row0, 1, True)
        return carry

    lax.fori_loop(0, n_lat, lat_body, 0, unroll=2)

    dskip = dskip_ref[0]
    normw = normw_ref[0]

    def finish(z_ref, o_ref, c, row0):
        rows = pl.ds(_aligned(row0 + c * q, q), q)
        crow = pl.ds(_aligned(c * q, q), q)
        y = yf_s[rows, :] + yb_s[rows, :] + dskip * xs_s[rows, :]
        y = y * _silu(z_ref[0, crow, :].astype(F32))
        ms = jnp.mean(y * y, axis=-1, keepdims=True)
        o_ref[0, crow, :] = (y * lax.rsqrt(ms + NORM_EPS) * normw).astype(BF16)

    if need_ctx:
        for i in range(n_ctx):
            finish(z_c, out_c, i, 0)
    else:
        out_c[...] = jnp.zeros_like(out_c)

    def fin_body(c, carry):
        finish(z_l, out_l, c, lat_row0)
        return carry

    lax.fori_loop(0, n_lat, fin_body, 0)


def _ssd(z_l, xbc_l, dt_l, z_c, xbc_c, dt_c, params, layer, need_ctx):
    b, s, _ = z_l.shape
    c = z_c.shape[1]
    n_lat, n_ctx = s // SSD_CHUNK, c // SSD_CHUNK
    t = s + c
    convw, convb, bias_row, alog_row, dskip, normw = params

    def seq_specs(n):
        return [
            pl.BlockSpec((1, n, SSD_WIDTH), lambda bi: (bi, 0, 0)),
            pl.BlockSpec((1, n, SSD_CONV_CH), lambda bi: (bi, 0, 0)),
            pl.BlockSpec((1, n, LANES), lambda bi: (bi, 0, 0)),
        ]

    def par(shape):
        return pl.BlockSpec((1,) + shape, lambda bi: (layer,) + (0,) * len(shape))

    return pl.pallas_call(
        functools.partial(_ssd_kernel, n_lat=n_lat, n_ctx=n_ctx, need_ctx=need_ctx),
        grid=(b,),
        in_specs=seq_specs(s) + seq_specs(c) + [
            par((8, SSD_CONV_CH)), par((1, SSD_CONV_CH)), par((1, LANES)), par((1, LANES)),
            par((1, SSD_WIDTH)), par((1, SSD_WIDTH)),
        ],
        out_specs=[
            pl.BlockSpec((1, s, SSD_WIDTH), lambda bi: (bi, 0, 0)),
            pl.BlockSpec((1, c, SSD_WIDTH), lambda bi: (bi, 0, 0)),
        ],
        out_shape=(jax.ShapeDtypeStruct((b, s, SSD_WIDTH), BF16), jax.ShapeDtypeStruct((b, c, SSD_WIDTH), BF16)),
        scratch_shapes=[
            pltpu.VMEM((t, SSD_WIDTH), F32),
            pltpu.VMEM((t, SSD_WIDTH), BF16),
            pltpu.VMEM((t, SSD_WIDTH), F32),
            pltpu.VMEM((t, SSD_WIDTH), F32),
            pltpu.VMEM((SSD_STATE, SSD_WIDTH), F32),
            pltpu.VMEM((SSD_STATE, SSD_WIDTH), F32),
        ],
        compiler_params=_cparams(("parallel",)),
        name="ssd",
    )(z_l, xbc_l, dt_l, z_c, xbc_c, dt_c, convw, convb, bias_row, alog_row, dskip, normw)


def _mla_t_kernel(*refs, with_latent):
    if with_latent:
        q_ref, kl_ref, vtl_ref, kc_ref, vtc_ref, o_ref, s_scr = refs
        n_lat = kl_ref.shape[1]
    else:
        q_ref, kc_ref, vtc_ref, o_ref, s_scr = refs
        kl_ref = vtl_ref = None
        n_lat = 0
    n_ctx = kc_ref.shape[1]
    kc_ = min(MLA_KEY_CHUNK, n_ctx)
    chunks = [(kl_ref, vtl_ref, j * kc_, j * kc_) for j in range(n_lat // kc_)]
    chunks += [(kc_ref, vtc_ref, j * kc_, n_lat + j * kc_) for j in range(n_ctx // kc_)]

    def scores(h):
        sl = slice(h * MLA_HEAD_PAD, (h + 1) * MLA_HEAD_PAD)
        qh = q_ref[0, :, sl]
        for k_ref, _, src, dst in chunks:
            s_scr[h % 2, dst:dst + kc_, :] = _dot_nt(k_ref[0, src:src + kc_, sl], qh)

    def attend(h):
        vrows = slice(h * MLA_V, (h + 1) * MLA_V)
        m = None
        for _, _, _, dst in chunks:
            mc = s_scr[h % 2, dst:dst + kc_, :].max(axis=0, keepdims=True)
            m = mc if m is None else jnp.maximum(m, mc)
        l = None
        o = None
        for _, vt_ref, src, dst in chunks:
            p = jnp.exp2(s_scr[h % 2, dst:dst + kc_, :] - m)
            lc = p.sum(axis=0, keepdims=True)
            oc = _dot(vt_ref[0, vrows, src:src + kc_], p.astype(BF16))
            l = lc if l is None else l + lc
            o = oc if o is None else o + oc
        return o * (1.0 / l)

    outs = []
    scores(0)
    for h in range(MLA_HEADS):
        if h + 1 < MLA_HEADS:
            scores(h + 1)
        outs.append(attend(h))
    o_ref[0] = jnp.concatenate(outs, axis=0).T.astype(BF16)


def _mla_attention_t(q, k_l, vt_l, k_c, vt_c, tq):
    b, sq, wq = q.shape
    c = k_c.shape[1]
    with_latent = k_l is not None
    s = k_l.shape[1] if with_latent else 0
    in_specs = [pl.BlockSpec((1, tq, wq), lambda bi, ti: (bi, ti, 0))]
    args = [q]
    if with_latent:
        in_specs += [pl.BlockSpec((1, s, wq), lambda bi, ti: (bi, 0, 0)),
                     pl.BlockSpec((1, MLA_WIDTH, s), lambda bi, ti: (bi, 0, 0))]
        args += [k_l, vt_l]
    in_specs += [pl.BlockSpec((1, c, wq), lambda bi, ti: (bi, 0, 0)),
                 pl.BlockSpec((1, MLA_WIDTH, c), lambda bi, ti: (bi, 0, 0))]
    args += [k_c, vt_c]
    return pl.pallas_call(
        functools.partial(_mla_t_kernel, with_latent=with_latent),
        grid=(b, sq // tq),
        in_specs=in_specs,
        out_specs=pl.BlockSpec((1, tq, MLA_WIDTH), lambda bi, ti: (bi, ti, 0)),
        out_shape=jax.ShapeDtypeStruct((b, sq, MLA_WIDTH), BF16),
        scratch_shapes=[pltpu.VMEM((2, s + c, tq), F32)],
        compiler_params=_cparams(("parallel", "parallel")),
        name="mla_attn" if with_latent else "mla_ctx_attn",
    )(*args)


def _outffn_kernel(x_ref, na_ref, ssd_ref, mla_ref, mod_ref, nw_ref, wo_ref, w1_ref, w2_ref, o_ref,
                   x1_s, xn_s, acc_s):
    f = pl.program_id(2)
    mod = mod_ref[0, 0]

    @pl.when(f == 0)
    def _():
        mix = (_dot(na_ref[0], wo_ref[0, 0:NA_WIDTH, :])
               + _dot(ssd_ref[0], wo_ref[0, NA_WIDTH:NA_WIDTH + SSD_WIDTH, :])
               + _dot(mla_ref[0], wo_ref[0, NA_WIDTH + SSD_WIDTH:, :]))
        x1 = x_ref[0] + mod[2:3, :] * mix
        x1_s[...] = x1
        ms = jnp.mean(x1 * x1, axis=-1, keepdims=True)
        xn = x1 * lax.rsqrt(ms + NORM_EPS) * nw_ref[0]
        xn_s[...] = (xn * (1.0 + mod[4:5, :]) + mod[3:4, :]).astype(BF16)
        acc_s[...] = jnp.zeros_like(acc_s)

    hmid = _dot(xn_s[...], w1_ref[0])
    hmid = jnp.square(jnp.maximum(hmid, 0.0)).astype(BF16)
    acc_s[...] += _dot(hmid, w2_ref[0])

    @pl.when(f == pl.num_programs(2) - 1)
    def _():
        o_ref[0] = x1_s[...] + mod[5:6, :] * acc_s[...]


def _outffn(xs, na, ssd, mla, mod, layer, mod_row_fn, norm_w, w_out, w1, w2, tm, tf):
    b, s, d = xs.shape
    depth = w_out.shape[0]
    tile = lambda w: pl.BlockSpec((1, tm, w), lambda bi, ti, fi: (bi, ti, 0))
    return pl.pallas_call(
        _outffn_kernel,
        grid=(b, s // tm, D_FF // tf),
        in_specs=[
            tile(d), tile(NA_WIDTH), tile(SSD_WIDTH), tile(MLA_WIDTH),
            pl.BlockSpec((1, 1, 6, d), lambda bi, ti, fi: (layer, mod_row_fn(bi), 0, 0)),
            pl.BlockSpec((1, 1, d), lambda bi, ti, fi: (layer, 0, 0)),
            pl.BlockSpec((1, MIX_WIDTH, d), lambda bi, ti, fi: (layer, 0, 0)),
            pl.BlockSpec((1, d, tf), lambda bi, ti, fi: (layer, 0, fi)),
            pl.BlockSpec((1, tf, d), lambda bi, ti, fi: (layer, fi, 0)),
        ],
        out_specs=tile(d),
        out_shape=jax.ShapeDtypeStruct((b, s, d), F32),
        scratch_shapes=[pltpu.VMEM((tm, d), F32), pltpu.VMEM((tm, d), BF16), pltpu.VMEM((tm, d), F32)],
        compiler_params=_cparams(("parallel", "parallel", "arbitrary")),
        name="outproj_mlp",
    )(xs, na, ssd, mla, mod, norm_w.reshape(depth, 1, d), w_out, w1, w2)


def _rope_tables(n_tokens):
    pos = jnp.arange(n_tokens)
    axes = jnp.stack([pos // GRID_W, pos % GRID_W], axis=-1).astype(F32)
    n_freq = MLA_ROPE // 4
    inv_freq = ROPE_THETA ** (-jnp.arange(n_freq, dtype=F32) / n_freq)
    ang = axes[:, :, None] * inv_freq
    cos, sin = jnp.cos(ang), jnp.sin(ang)
    cos_l = jnp.stack([cos, cos], axis=2).reshape(n_tokens, MLA_ROPE)
    sin_l = jnp.stack([-sin, sin], axis=2).reshape(n_tokens, MLA_ROPE)

    def place(t, fill):
        left = jnp.full((n_tokens, MLA_NOPE), fill, F32)
        right = jnp.full((n_tokens, LANES - MLA_QK), fill, F32)
        return jnp.concatenate([left, t, right], axis=1)

    return place(cos_l, 1.0), place(sin_l, 0.0)


def _prep_params(w_in, na_qn_w, na_kn_w, ssd_conv_w, ssd_conv_b, ssd_dt_bias, ssd_a_log, ssd_d, ssd_norm_w,
                 mla_cq_norm_w, mla_ckv_norm_w, mla_w_uq, mla_w_ukv, mla_qn_w, mla_kn_w):
    depth = w_in.shape[0]
    d = w_in.shape[1]
    dtw = w_in[..., 2304:2320]
    w_in_p = jnp.concatenate(
        [w_in[..., :2304], w_in[..., 2320:2576], w_in[..., 2576:2704], w_in[..., 2704:2736]]
        + [dtw] * DT_COPIES + [jnp.zeros((depth, d, IN_WIDTH_PAD - 2720 - 16 * DT_COPIES), F32)], axis=-1).astype(BF16)
    qnw = jnp.tile(na_qn_w, (1, NA_HEADS))[:, None, :]
    knw = jnp.tile(na_kn_w, (1, NA_HEADS))[:, None, :]
    gi = np.arange(NA_WIDTH) // NA_HEAD_DIM
    grp = jnp.asarray((gi[:, None] == gi[None, :]).astype(np.float32), BF16)

    convw = jnp.pad(ssd_conv_w, ((0, 0), (0, 8 - SSD_CONV), (0, 0)))
    convb = ssd_conv_b[:, None, :]

    def dt_row(t):
        flat = jnp.tile(t.reshape(depth, 16), (1, DT_COPIES))
        return jnp.pad(flat, ((0, 0), (DT_LANE0, LANES - DT_LANE0 - 16 * DT_COPIES)))[:, None, :]

    bias_row = dt_row(ssd_dt_bias)
    alog_row = dt_row(ssd_a_log)
    dskip = jnp.repeat(ssd_d, SSD_HEAD_DIM, axis=-1)[:, None, :]
    normw = ssd_norm_w[:, None, :]
    ssd_params = (convw, convb, bias_row, alog_row, dskip, normw)

    wuq = mla_w_uq.reshape(depth, MLA_Q_RANK, MLA_HEADS, MLA_QK)
    wuq_p = jnp.pad(wuq, ((0, 0), (0, 0), (0, 0), (0, MLA_HEAD_PAD - MLA_QK))).reshape(depth, MLA_Q_RANK, -1).astype(BF16)
    wukv = mla_w_ukv.reshape(depth, MLA_KV_RANK, MLA_HEADS, MLA_NOPE + MLA_V)
    wk = jnp.pad(wukv[..., :MLA_NOPE], ((0, 0), (0, 0), (0, 0), (0, MLA_HEAD_PAD - MLA_NOPE))).reshape(depth, MLA_KV_RANK, -1)
    wuvt = jnp.swapaxes(wukv[..., MLA_NOPE:].reshape(depth, MLA_KV_RANK, -1), 1, 2).astype(BF16)
    pad_head = lambda t: jnp.tile(jnp.pad(t, ((0, 0), (0, MLA_HEAD_PAD - MLA_QK))), (1, MLA_HEADS))[:, None, :]
    li = np.arange(MLA_HEADS * MLA_HEAD_PAD)
    g96 = jnp.asarray((li[:, None] // MLA_HEAD_PAD == li[None, :] // MLA_HEAD_PAD).astype(np.float32), BF16)
    in_rope = (li % MLA_HEAD_PAD >= MLA_NOPE) & (li % MLA_HEAD_PAD < MLA_QK)
    swp = jnp.asarray(((li[:, None] == (li[None, :] ^ 8)) & in_rope[None, :]).astype(np.float32), BF16)
    mla_weights = (mla_cq_norm_w[:, None, :], mla_ckv_norm_w[:, None, :], wuq_p, wk.astype(BF16), wuvt,
                   pad_head(mla_qn_w), pad_head(mla_kn_w), g96, swp)
    return w_in_p, qnw, knw, grp, ssd_params, mla_weights


def kernel(x, c, ctx, c_ctx, w_ada, b_ada, norm1_w, norm2_w, w_in, w_out, na_qn_w, na_kn_w, na_rpb, ssd_conv_w, ssd_conv_b, ssd_dt_bias, ssd_a_log, ssd_d, ssd_norm_w, mla_cq_norm_w, mla_ckv_norm_w, mla_w_uq, mla_w_ukv, mla_qn_w, mla_kn_w, w_ff1, w_ff2):
    b, s, d = x.shape
    ctx_len = ctx.shape[1]
    depth = w_in.shape[0]
    n_rows = s // GRID_W
    assert d == D_MODEL and s % (4 * GRID_W) == 0 and n_rows > NA_WIN_ROWS and ctx_len % SSD_CHUNK == 0

    mod_rows = ((b + 1 + 7) // 8) * 8
    cc = jnp.concatenate([c, c_ctx[None, :], jnp.zeros((mod_rows - b - 1, d), F32)], axis=0)
    mod = _adaln(cc, w_ada, b_ada).reshape(depth, mod_rows, 6, d)

    w_in_p, qnw, knw, grp, ssd_params, mla_weights = _prep_params(
        w_in, na_qn_w, na_kn_w, ssd_conv_w, ssd_conv_b, ssd_dt_bias, ssd_a_log, ssd_d, ssd_norm_w,
        mla_cq_norm_w, mla_ckv_norm_w, mla_w_uq, mla_w_ukv, mla_qn_w, mla_kn_w)
    w_out_b = w_out.astype(BF16)
    w1_b = w_ff1.astype(BF16)
    w2_b = w_ff2.astype(BF16)
    bias_all = _na_bias(na_rpb, n_rows)
    rope_tabs = _rope_tables(s)

    lat_row = lambda bi: bi
    ctx_row = lambda bi: b
    tm = 512
    tm_c = min(tm, b * ctx_len)
    ctx_flat = ctx.reshape(1, b * ctx_len, d)

    for i in range(depth):
        need_ctx = i < depth - 1
        qkv_l, z_l, xbc_l, dt_l, qm_l, km_l, vt_l = _inproj(
            x, mod, i, lat_row, norm1_w, w_in_p, qnw, knw, grp, mla_weights, rope_tabs, tm)
        qkv_c, z_c, xbc_c, dt_c, qm_c, km_c, vt_c = _inproj(
            ctx_flat.reshape(b, ctx_len, d), mod, i, ctx_row, norm1_w, w_in_p, qnw, knw, grp, mla_weights, None, ctx_len)

        na_l = _na_attention(qkv_l, qkv_c, bias_all, i)
        ssd_l, ssd_c = _ssd(z_l, xbc_l, dt_l, z_c, xbc_c, dt_c, ssd_params, i, need_ctx)
        mla_l = _mla_attention_t(qm_l, km_l, vt_l, km_c, vt_c, 512)

        x = _outffn(x, na_l, ssd_l, mla_l, mod, i, lat_row, norm2_w, w_out_b, w1_b, w2_b, 2 * tm, 1024)
        if need_ctx:
            na_c = _na_ctx_attention(qkv_c)
            mla_c = _mla_attention_t(qm_c, None, None, km_c, vt_c, ctx_len)
            flat = lambda t: t.reshape(1, b * ctx_len, t.shape[-1])
            ctx_flat = _outffn(ctx_flat, flat(na_c), flat(ssd_c), flat(mla_c), mod, i, ctx_row, norm2_w,
                               w_out_b, w1_b, w2_b, tm_c, 1024)
    return x
```

```python
import functools

import jax
import jax.numpy as jnp
import numpy as np
from jax import lax
from jax.experimental import pallas as pl
from jax.experimental.pallas import tpu as pltpu

F32 = jnp.float32
BF16 = jnp.bfloat16

D_MODEL = 1024
GRID_W = 64
D_FF = 4 * D_MODEL
NORM_EPS = 1e-6
ROPE_THETA = 10000.0
LOG2E = 1.4426950408889634
NA_HEADS = 4
NA_HEAD_DIM = 64
NA_WIDTH = 256
NA_WIN_ROWS = 8
NA_WIN_COLS = 16
SSD_HEADS = 8
SSD_HEAD_DIM = 64
SSD_WIDTH = 512
SSD_GROUPS = 2
SSD_STATE = 128
SSD_CONV = 5
SSD_CHUNK = 128
SSD_CONV_CH = 1024
MLA_HEADS = 4
MLA_NOPE = 64
MLA_ROPE = 32
MLA_V = 64
MLA_QK = 96
MLA_Q_RANK = 256
MLA_KV_RANK = 128
MLA_WIDTH = 256
MIX_WIDTH = 1024
IN_WIDTH = 2736

LANES = 128
MLA_HEAD_PAD = 128
IN_WIDTH_PAD = 2816
DT_LANE0 = 32
DT_COPIES = 3
MLA_KEY_CHUNK = 256
VMEM_LIMIT = 56 * 1024 * 1024

NT_DIMS = (((1,), (1,)), ((), ()))
TN_DIMS = (((0,), (0,)), ((), ()))


def _dot(a, b):
    return jnp.dot(a, b, preferred_element_type=F32)


def _dot_nt(a, b):
    return lax.dot_general(a, b, NT_DIMS, preferred_element_type=F32)


def _dot_tn(a, b):
    return lax.dot_general(a, b, TN_DIMS, preferred_element_type=F32)


def _sigmoid(x):
    return 1.0 / (1.0 + jnp.exp(-x))


def _silu(x):
    return x * _sigmoid(x)


def _softplus(x):
    return jnp.maximum(x, 0.0) + jnp.log1p(jnp.exp(-jnp.abs(x)))


def _split3(x):
    x0 = x.astype(BF16)
    r = x - x0.astype(F32)
    x1 = r.astype(BF16)
    r = r - x1.astype(F32)
    return x0, x1, r.astype(BF16)


def _lane_parts(x, lane):
    p0 = x.astype(BF16).astype(F32)
    r = x - p0
    p1 = r.astype(BF16).astype(F32)
    p2 = r - p1
    return jnp.where(lane < DT_LANE0 + 16, p0, jnp.where(lane < DT_LANE0 + 32, p1, p2)).astype(BF16)


def _aligned(x, m):
    return x if isinstance(x, int) else pl.multiple_of(x, m)


def _cparams(sem):
    return pltpu.CompilerParams(dimension_semantics=sem, vmem_limit_bytes=VMEM_LIMIT)


def _adaln_kernel(c_ref, w_ref, b_ref, o_ref):
    act = _silu(c_ref[...]).astype(BF16)
    o_ref[0] = _dot(act, w_ref[0].astype(BF16)) + b_ref[0]


def _adaln(cc, w_ada, b_ada):
    depth, d, n = w_ada.shape
    rows = cc.shape[0]
    tn = 1536
    return pl.pallas_call(
        _adaln_kernel,
        grid=(depth, n // tn),
        in_specs=[
            pl.BlockSpec((rows, d), lambda l, j: (0, 0)),
            pl.BlockSpec((1, d, tn), lambda l, j: (l, 0, j)),
            pl.BlockSpec((1, 1, tn), lambda l, j: (l, 0, j)),
        ],
        out_specs=pl.BlockSpec((1, rows, tn), lambda l, j: (l, 0, j)),
        out_shape=jax.ShapeDtypeStruct((depth, rows, n), F32),
        compiler_params=_cparams(("parallel", "parallel")),
        name="adaln",
    )(cc, w_ada, b_ada.reshape(depth, 1, n))


def _inproj_kernel(x_ref, mod_ref, nw_ref, w_ref, qnw_ref, knw_ref, grp_ref,
                   cqn_ref, ckvn_ref, wuq_ref, wuk_ref, wuvt_ref, mqnw_ref, mknw_ref, g96_ref, swp_ref, *rest, rope):
    if rope:
        cos_ref, sin_ref = rest[:2]
        rest = rest[2:]
    qkv_ref, z_ref, xbc_ref, dt_ref, qm_ref, km_ref, vt_ref = rest

    def rms(t, w):
        ms = jnp.mean(t * t, axis=-1, keepdims=True)
        return t * lax.rsqrt(ms + NORM_EPS) * w

    mod = mod_ref[0, 0]
    h = (rms(x_ref[0], nw_ref[0]) * (1.0 + mod[1:2, :]) + mod[0:1, :]).astype(BF16)
    u = _dot(h, w_ref[0])

    grp = grp_ref[...]

    def head_norm(t, w):
        ss = _dot((t * t).astype(BF16), grp)
        return t * lax.rsqrt(ss * (1.0 / NA_HEAD_DIM) + NORM_EPS) * w

    q = head_norm(u[:, 0:256], qnw_ref[0]) * (NA_HEAD_DIM ** -0.5 * LOG2E)
    k = head_norm(u[:, 256:512], knw_ref[0])
    qkv_ref[0, :, 0:256] = q.astype(BF16)
    qkv_ref[0, :, 256:512] = k.astype(BF16)
    qkv_ref[0, :, 512:768] = u[:, 512:768].astype(BF16)
    z_ref[0] = u[:, 768:1280].astype(BF16)
    xbc_ref[0] = u[:, 1280:2304].astype(BF16)
    tail = u[:, 2688:2816]
    dt_ref[0] = tail

    g96 = g96_ref[...]
    swp = swp_ref[...]
    nh = MLA_HEADS

    def inv_rms96(ss):
        return lax.rsqrt(ss * (1.0 / MLA_QK) + NORM_EPS)

    def rotate(t):
        if not rope:
            return t
        reps = t.shape[1] // LANES
        sw = swp[:t.shape[1], :t.shape[1]]
        hi = t.astype(BF16)
        lo = (t - hi.astype(F32)).astype(BF16)
        partner = _dot(hi, sw) + _dot(lo, sw)
        return t * jnp.concatenate([cos_ref[...]] * reps, axis=1) + partner * jnp.concatenate([sin_ref[...]] * reps, axis=1)

    qf = _dot(rms(u[:, 2304:2560], cqn_ref[0]).astype(BF16), wuq_ref[0])
    ckvn = rms(u[:, 2560:2688], ckvn_ref[0]).astype(BF16)
    kn = _dot(ckvn, wuk_ref[0])
    vt_ref[0] = _dot_nt(wuvt_ref[0], ckvn).astype(BF16)
    lane = lax.broadcasted_iota(jnp.int32, (1, LANES), 1)
    kr = jnp.where((lane >= MLA_NOPE) & (lane < MLA_QK), pltpu.roll(tail, MLA_NOPE, axis=1), 0.0)
    mqnw = mqnw_ref[0]
    mknw = mknw_ref[0]

    q_ss = _dot((qf * qf).astype(BF16), g96)
    qn = rotate(qf * inv_rms96(q_ss) * mqnw)
    qm_ref[0] = (qn * (MLA_QK ** -0.5 * LOG2E)).astype(BF16)
    kr_ss = jnp.sum(kr * kr, axis=-1, keepdims=True)
    kr_rot = rotate(kr * mknw[:, :LANES])
    k_ss = _dot((kn * kn).astype(BF16), g96) + kr_ss
    km_ref[0] = ((kn * mknw + jnp.concatenate([kr_rot] * nh, axis=1)) * inv_rms96(k_ss)).astype(BF16)


def _inproj(xs, mod, layer, mod_row_fn, norm_w, w_in_p, qnw, knw, grp, mla_weights, rope_tabs, tm):
    b, s, d = xs.shape
    depth = w_in_p.shape[0]
    cqn_w, ckvn_w, wuq_p, wuk_p, wuvt_p, mqnw_p, mknw_p, g96, swp = mla_weights
    rope = rope_tabs is not None
    hw = MLA_HEADS * MLA_HEAD_PAD
    outs = (
        jax.ShapeDtypeStruct((b, s, 768), BF16),
        jax.ShapeDtypeStruct((b, s, SSD_WIDTH), BF16),
        jax.ShapeDtypeStruct((b, s, SSD_CONV_CH), BF16),
        jax.ShapeDtypeStruct((b, s, LANES), F32),
        jax.ShapeDtypeStruct((b, s, hw), BF16),
        jax.ShapeDtypeStruct((b, s, hw), BF16),
        jax.ShapeDtypeStruct((b, MLA_WIDTH, s), BF16),
    )
    tile = lambda w: pl.BlockSpec((1, tm, w), lambda bi, ti: (bi, ti, 0))
    tile_t = lambda r: pl.BlockSpec((1, r, tm), lambda bi, ti: (bi, 0, ti))

    def par(shape):
        return pl.BlockSpec((1,) + shape, lambda bi, ti: (layer,) + (0,) * len(shape))

    in_specs = [
        tile(d),
        pl.BlockSpec((1, 1, 6, d), lambda bi, ti: (layer, mod_row_fn(bi), 0, 0)),
        par((1, d)), par((d, IN_WIDTH_PAD)), par((1, 256)), par((1, 256)),
        pl.BlockSpec((256, 256), lambda bi, ti: (0, 0)),
        par((1, MLA_Q_RANK)), par((1, MLA_KV_RANK)), par((MLA_Q_RANK, hw)), par((MLA_KV_RANK, hw)),
        par((MLA_WIDTH, MLA_KV_RANK)),
        par((1, hw)), par((1, hw)),
        pl.BlockSpec((hw, hw), lambda bi, ti: (0, 0)), pl.BlockSpec((hw, hw), lambda bi, ti: (0, 0)),
    ]
    args = [xs, mod, norm_w.reshape(depth, 1, d), w_in_p, qnw, knw, grp, cqn_w, ckvn_w, wuq_p, wuk_p, wuvt_p, mqnw_p,
            mknw_p, g96, swp]
    if rope:
        in_specs += [pl.BlockSpec((tm, LANES), lambda bi, ti: (ti, 0))] * 2
        args += list(rope_tabs)
    return pl.pallas_call(
        functools.partial(_inproj_kernel, rope=rope),
        grid=(b, s // tm),
        in_specs=in_specs,
        out_specs=[tile(768), tile(SSD_WIDTH), tile(SSD_CONV_CH), tile(LANES), tile(hw), tile(hw), tile_t(MLA_WIDTH)],
        out_shape=outs,
        compiler_params=_cparams(("parallel", "parallel")),
        name="inproj",
    )(*args)


NA_VARIANT_ROWS = (0, 1, 2, 3, 4, -3, -2, -1)


def _na_bias_kernel(rpb_ref, o_ref, t_ref, *, n_rows):
    nrow_off = 2 * NA_WIN_ROWS - 1
    ncol_off = 2 * NA_WIN_COLS - 1
    base = (pl.program_id(0) * NA_HEADS + pl.program_id(1)) * (nrow_off * ncol_off)
    qi = lax.broadcasted_iota(jnp.int32, (GRID_W, LANES), 0)
    li = lax.broadcasted_iota(jnp.int32, (GRID_W, LANES), 1)
    kcol = li & (GRID_W - 1)
    colidx = jnp.clip(kcol - qi, -(NA_WIN_COLS - 1), NA_WIN_COLS - 1) + (NA_WIN_COLS - 1)
    cstart = jnp.clip(qi - NA_WIN_COLS // 2, 0, GRID_W - NA_WIN_COLS)
    valid = (kcol >= cstart) & (kcol < cstart + NA_WIN_COLS)
    for d in range(nrow_off):
        acc = jnp.zeros((GRID_W, LANES), F32)
        for j in range(ncol_off):
            acc = jnp.where(colidx == j, rpb_ref[base + d * ncol_off + j], acc)
        t_ref[d] = jnp.where(valid, acc * LOG2E, -jnp.inf)
    for v, r_rep in enumerate(NA_VARIANT_ROWS):
        r = r_rep if r_rep >= 0 else n_rows + r_rep
        rs = min(max(r - NA_WIN_ROWS // 2, 0), n_rows - NA_WIN_ROWS)
        for p in range(NA_WIN_ROWS // 2):
            d0 = rs + 2 * p - r + (NA_WIN_ROWS - 1)
            o_ref[0, v, :, p * LANES:(p + 1) * LANES] = jnp.where(li < GRID_W, t_ref[d0], t_ref[d0 + 1])


def _na_bias(rpb, n_rows):
    depth, heads = rpb.shape[0], rpb.shape[1]
    n_win = NA_WIN_ROWS * GRID_W
    return pl.pallas_call(
        functools.partial(_na_bias_kernel, n_rows=n_rows),
        grid=(depth, heads),
        in_specs=[pl.BlockSpec(memory_space=pltpu.SMEM)],
        out_specs=pl.BlockSpec((1, 8, GRID_W, n_win), lambda l, h: (l, 0, h, 0)),
        out_shape=jax.ShapeDtypeStruct((depth, 8, heads * GRID_W, n_win), F32),
        scratch_shapes=[pltpu.VMEM((2 * NA_WIN_ROWS - 1, GRID_W, LANES), F32)],
        compiler_params=_cparams(("parallel", "parallel")),
        name="na_bias",
    )(rpb.reshape(-1))


def _stacked_heads_attention(q, key_vals, biases):
    m_rows = q.shape[0]
    lane = lax.broadcasted_iota(jnp.int32, (1, NA_WIDTH), 1)
    hms = [jnp.where((lane >= h * NA_HEAD_DIM) & (lane < (h + 1) * NA_HEAD_DIM), 1.0, 0.0) for h in range(NA_HEADS)]
    qs = jnp.concatenate([q * hm.astype(BF16) for hm in hms], axis=0)
    scores = []
    for (k, _), b in zip(key_vals, biases):
        s = _dot_nt(qs, k)
        scores.append(s if b is None else s + b)
    m = scores[0].max(axis=-1, keepdims=True)
    for s in scores[1:]:
        m = jnp.maximum(m, s.max(axis=-1, keepdims=True))
    l = None
    o = None
    for s, (_, v) in zip(scores, key_vals):
        p = jnp.exp2(s - m)
        ls = p.sum(axis=-1, keepdims=True)
        os_ = _dot(p.astype(BF16), v)
        l = ls if l is None else l + ls
        o = os_ if o is None else o + os_
    o = o * (1.0 / l)
    out = o[0:m_rows] * hms[0]
    for h in range(1, NA_HEADS):
        out = out + o[h * m_rows:(h + 1) * m_rows] * hms[h]
    return out


def _na_kernel(q_ref, k_ref, v_ref, kc_ref, vc_ref, bias_ref, o_ref, *, rows_per_step, n_rows):
    blk = pl.program_id(1)
    kc = kc_ref[0]
    vc = vc_ref[0]
    n_win = NA_WIN_ROWS * GRID_W
    for i in range(rows_per_step):
        r = blk * rows_per_step + i
        rs = jnp.clip(r - NA_WIN_ROWS // 2, 0, n_rows - NA_WIN_ROWS)
        var = jnp.where(r < NA_WIN_ROWS // 2, r,
                        jnp.where(r > n_rows - NA_WIN_ROWS // 2, r - (n_rows - NA_WIN_ROWS), NA_WIN_ROWS // 2))
        start = pl.multiple_of(rs * GRID_W, GRID_W)
        kw = k_ref[0, pl.ds(start, n_win), :]
        vw = v_ref[0, pl.ds(start, n_win), :]
        q = q_ref[0, i * GRID_W:(i + 1) * GRID_W, :]
        out = _stacked_heads_attention(q, [(kw, vw), (kc, vc)], [bias_ref[0, var], None])
        o_ref[0, i * GRID_W:(i + 1) * GRID_W, :] = out.astype(BF16)


def _na_attention(qkv_l, qkv_c, bias_all, layer, rows_per_step=8):
    b, s, _ = qkv_l.shape
    ctx_len = qkv_c.shape[1]
    n_rows = s // GRID_W
    tq = rows_per_step * GRID_W
    return pl.pallas_call(
        functools.partial(_na_kernel, rows_per_step=rows_per_step, n_rows=n_rows),
        grid=(b, n_rows // rows_per_step),
        in_specs=[
            pl.BlockSpec((1, tq, 256), lambda bi, ti: (bi, ti, 0)),
            pl.BlockSpec((1, s, 256), lambda bi, ti: (bi, 0, 1)),
            pl.BlockSpec((1, s, 256), lambda bi, ti: (bi, 0, 2)),
            pl.BlockSpec((1, ctx_len, 256), lambda bi, ti: (bi, 0, 1)),
            pl.BlockSpec((1, ctx_len, 256), lambda bi, ti: (bi, 0, 2)),
            pl.BlockSpec((1,) + bias_all.shape[1:], lambda bi, ti: (layer, 0, 0, 0)),
        ],
        out_specs=pl.BlockSpec((1, tq, 256), lambda bi, ti: (bi, ti, 0)),
        out_shape=jax.ShapeDtypeStruct((b, s, NA_WIDTH), BF16),
        compiler_params=_cparams(("parallel", "parallel")),
        name="na_attn",
    )(qkv_l, qkv_l, qkv_l, qkv_c, qkv_c, bias_all)


def _na_ctx_kernel(q_ref, k_ref, v_ref, o_ref):
    o_ref[0] = _stacked_heads_attention(q_ref[0], [(k_ref[0], v_ref[0])], [None]).astype(BF16)


def _na_ctx_attention(qkv_c):
    b, c, _ = qkv_c.shape
    spec = lambda j: pl.BlockSpec((1, c, 256), lambda bi: (bi, 0, j))
    return pl.pallas_call(
        _na_ctx_kernel,
        grid=(b,),
        in_specs=[spec(0), spec(1), spec(2)],
        out_specs=spec(0),
        out_shape=jax.ShapeDtypeStruct((b, c, NA_WIDTH), BF16),
        compiler_params=_cparams(("parallel",)),
        name="na_ctx_attn",
    )(qkv_c, qkv_c, qkv_c)


def _ssd_kernel(z_l, xbc_l, dt_l, z_c, xbc_c, dt_c,
                convw_ref, convb_ref, bias_row_ref, alog_row_ref, dskip_ref, normw_ref, out_l, out_c,
                xs_s, bc_s, yf_s, yb_s, stf_s, stb_s, dt_s, acf_s, acb_s, *, n_lat, n_ctx, need_ctx):
    q = SSD_CHUNK
    lat_row0 = n_ctx * q

    convw = convw_ref[0]
    convb = convb_ref[0]
    win = q + 32
    wi = lax.broadcasted_iota(jnp.int32, (q, win), 0)
    wr = lax.broadcasted_iota(jnp.int32, (q, win), 1)
    mid = SSD_CONV // 2

    def shift_mats(off):
        return [None if j == mid else jnp.where(wr == wi + (off + j - mid), 1.0, 0.0).astype(BF16)
                for j in range(SSD_CONV)]

    def conv_chunk(xref, lo, off, smats, dst):
        w_in = xref[0, pl.ds(lo, win), :]
        acc = convb + w_in[off:off + q].astype(F32) * convw[mid:mid + 1, :]
        for j in range(SSD_CONV):
            if j != mid:
                acc = acc + _dot(smats[j], w_in) * convw[j:j + 1, :]
        act = _silu(acc)
        xs_s[pl.ds(dst, q), :] = act[:, :SSD_WIDTH]
        bc_s[pl.ds(dst, q), :] = act[:, SSD_WIDTH:].astype(BF16)

    def conv_seq(xref, n, row0):
        conv_chunk(xref, 0, 0, shift_mats(0), row0)
        conv_chunk(xref, (n - 1) * q - 32, 32, shift_mats(32), row0 + (n - 1) * q)
        if n > 2:
            smats = shift_mats(16)

            def body(c, carry):
                conv_chunk(xref, pl.multiple_of(c * q - 16, 16), 16, smats, pl.multiple_of(row0 + c * q, q))
                return carry
            lax.fori_loop(1, n - 1, body, 0, unroll=2)

    conv_seq(xbc_c, n_ctx, 0)
    conv_seq(xbc_l, n_lat, lat_row0)

    lane = lax.broadcasted_iota(jnp.int32, (1, LANES), 1)
    dmask = (lane >= DT_LANE0) & (lane < DT_LANE0 + 16 * DT_COPIES)
    bias_row = bias_row_ref[0]
    a_row = jnp.where(dmask, -jnp.exp(alog_row_ref[0]), 0.0)
    ri = lax.broadcasted_iota(jnp.int32, (q, q), 0)
    ci = lax.broadcasted_iota(jnp.int32, (q, q), 1)
    lo_mask = ci <= ri
    up_mask = ri <= ci
    tri_lo = jnp.where(lo_mask, 1.0, 0.0).astype(BF16)
    tri_up = jnp.where(up_mask, 1.0, 0.0).astype(BF16)
    sel_r = lax.broadcasted_iota(jnp.int32, (LANES, SSD_HEADS * LANES), 0)
    sel_c = lax.broadcasted_iota(jnp.int32, (LANES, SSD_HEADS * LANES), 1)
    sel64_r = lax.broadcasted_iota(jnp.int32, (LANES, SSD_WIDTH), 0)
    sel64_c = lax.broadcasted_iota(jnp.int32, (LANES, SSD_WIDTH), 1)
    lane_lt64 = lax.broadcasted_iota(jnp.int32, (1, LANES), 1) < SSD_HEAD_DIM

    def sel_matrices(direction):
        in3 = (sel_r >= DT_LANE0) & (sel_r < DT_LANE0 + 48)
        sel128 = jnp.where(in3 & (((sel_r - DT_LANE0) & 15) == direction * SSD_HEADS + (sel_c >> 7)), 1.0, 0.0)
        in2 = (sel64_r >= DT_LANE0) & (sel64_r < DT_LANE0 + 32)
        sel64 = jnp.where(in2 & (((sel64_r - DT_LANE0) & 15) == direction * SSD_HEADS + (sel64_c >> 6)), 1.0, 0.0)
        return sel128.astype(BF16), sel64.astype(BF16)

    sels = (sel_matrices(0), sel_matrices(1))
    tri3_both = jnp.concatenate([jnp.concatenate([tri_lo] * 3, axis=1), jnp.concatenate([tri_up] * 3, axis=1)], axis=0)
    masks = (lo_mask, up_mask)

    y_refs = (yf_s, yb_s)
    st_refs = (stf_s, stb_s)
    stf_s[...] = jnp.zeros_like(stf_s)
    stb_s[...] = jnp.zeros_like(stb_s)

    ac_refs = (acf_s, acb_s)

    def decay_chunk(dt_ref, c, row0):
        rows = pl.ds(_aligned(row0 + c * q, q), q)
        raw = dt_ref[0, pl.ds(_aligned(c * q, q), q), :]
        dt = jnp.where(dmask, _softplus(raw + bias_row), 0.0)
        a0, a1, a2 = _split3(dt * a_row)
        both = _dot(tri3_both, jnp.concatenate([a0, a1, a2], axis=0))
        dt_s[rows, :] = dt
        acf_s[rows, :] = both[0:q]
        acb_s[rows, :] = both[q:2 * q]

    for i in range(n_ctx):
        decay_chunk(dt_c, i, 0)

    def decay_body(c, carry):
        decay_chunk(dt_l, c, lat_row0)
        return carry

    lax.fori_loop(0, n_lat, decay_body, 0, unroll=2)

    def chunk(c, row0, direction, compute_y):
        rows = pl.ds(_aligned(row0 + c * q, q), q)
        x = xs_s[rows, :]
        bc = bc_s[rows, :]
        sel128, sel64 = sels[direction]
        dt = dt_s[rows, :]
        acum = ac_refs[direction][rows, :]
        a_exp = _dot(_lane_parts(acum, lane), sel128)
        dt64 = _dot(_lane_parts(dt, lane), sel64)
        ac64 = jnp.concatenate(
            [jnp.where(lane_lt64, a_exp[:, (2 * j) * LANES:(2 * j + 1) * LANES],
                       a_exp[:, (2 * j + 1) * LANES:(2 * j + 2) * LANES]) for j in range(SSD_HEADS // 2)], axis=1)

        acum_t = acum.T

        end = q - 1 if direction == 0 else 0
        ac_end = ac64[end:end + 1, :]
        xd = x * dt64
        xdb = xd.astype(BF16)
        xdw = (xd * jnp.exp(ac_end - ac64)).astype(BF16)
        st = st_refs[direction][...]
        grp_w = SSD_WIDTH // SSD_GROUPS

        if compute_y:
            stb = st.astype(BF16)
            e_ac = jnp.exp(ac64)
            parts = []
            offs = []
            for g in range(SSD_GROUPS):
                bg = bc[:, g * SSD_STATE:(g + 1) * SSD_STATE]
                cg = bc[:, (SSD_GROUPS + g) * SSD_STATE:(SSD_GROUPS + g + 1) * SSD_STATE]
                gmat = _dot_nt(cg, bg)
                offs.append(_dot(cg, stb[:, g * grp_w:(g + 1) * grp_w]))
                for j in range(2):
                    h0 = g * 4 + 2 * j
                    xp = xdb[:, h0 * SSD_HEAD_DIM:(h0 + 2) * SSD_HEAD_DIM]
                    res = []
                    for hh in (h0, h0 + 1):
                        row = DT_LANE0 + direction * SSD_HEADS + hh
                        dmat = a_exp[:, hh * LANES:(hh + 1) * LANES] - acum_t[row:row + 1, :]
                        lmat = jnp.exp(jnp.where(masks[direction], dmat, -jnp.inf))
                        res.append(_dot((gmat * lmat).astype(BF16), xp))
                    parts.append(jnp.where(lane_lt64, res[0], res[1]))
            y = jnp.concatenate(parts, axis=1) + jnp.concatenate(offs, axis=1) * e_ac
            y_refs[direction][rows, :] = y

        new = []
        for g in range(SSD_GROUPS):
            bg = bc[:, g * SSD_STATE:(g + 1) * SSD_STATE]
            new.append(_dot_tn(bg, xdw[:, g * grp_w:(g + 1) * grp_w]))
        st_refs[direction][...] = st * jnp.exp(ac_end) + jnp.concatenate(new, axis=1)

    for i in range(n_ctx):
        chunk(i, 0, 0, need_ctx)
        chunk(n_ctx - 1 - i, 0, 1, need_ctx)

    def lat_body(i, carry):
        chunk(i, lat_row0, 0, True)
        chunk(n_lat - 1 - i, lat_row0, 1, True)
        return carry

    lax.fori_loop(0, n_lat, lat_body, 0, unroll=2)

    dskip = dskip_ref[0]
    normw = normw_ref[0]

    def finish(z_ref, o_ref, c, row0):
        rows = pl.ds(_aligned(row0 + c * q, q), q)
        crow = pl.ds(_aligned(c * q, q), q)
        y = yf_s[rows, :] + yb_s[rows, :] + dskip * xs_s[rows, :]
        y = y * _silu(z_ref[0, crow, :].astype(F32))
        ms = jnp.mean(y * y, axis=-1, keepdims=True)
        o_ref[0, crow, :] = (y * lax.rsqrt(ms + NORM_EPS) * normw).astype(BF16)

    if need_ctx:
        for i in range(n_ctx):
            finish(z_c, out_c, i, 0)
    else:
        out_c[...] = jnp.zeros_like(out_c)

    def fin_body(c, carry):
        finish(z_l, out_l, c, lat_row0)
        return carry

    lax.fori_loop(0, n_lat, fin_body, 0)


def _ssd(z_l, xbc_l, dt_l, z_c, xbc_c, dt_c, params, layer, need_ctx):
    b, s, _ = z_l.shape
    c = z_c.shape[1]
    n_lat, n_ctx = s // SSD_CHUNK, c // SSD_CHUNK
    t = s + c
    convw, convb, bias_row, alog_row, dskip, normw = params

    def seq_specs(n):
        return [
            pl.BlockSpec((1, n, SSD_WIDTH), lambda bi: (bi, 0, 0)),
            pl.BlockSpec((1, n, SSD_CONV_CH), lambda bi: (bi, 0, 0)),
            pl.BlockSpec((1, n, LANES), lambda bi: (bi, 0, 0)),
        ]

    def par(shape):
        return pl.BlockSpec((1,) + shape, lambda bi: (layer,) + (0,) * len(shape))

    return pl.pallas_call(
        functools.partial(_ssd_kernel, n_lat=n_lat, n_ctx=n_ctx, need_ctx=need_ctx),
        grid=(b,),
        in_specs=seq_specs(s) + seq_specs(c) + [
            par((8, SSD_CONV_CH)), par((1, SSD_CONV_CH)), par((1, LANES)), par((1, LANES)),
            par((1, SSD_WIDTH)), par((1, SSD_WIDTH)),
        ],
        out_specs=[
            pl.BlockSpec((1, s, SSD_WIDTH), lambda bi: (bi, 0, 0)),
            pl.BlockSpec((1, c, SSD_WIDTH), lambda bi: (bi, 0, 0)),
        ],
        out_shape=(jax.ShapeDtypeStruct((b, s, SSD_WIDTH), BF16), jax.ShapeDtypeStruct((b, c, SSD_WIDTH), BF16)),
        scratch_shapes=[
            pltpu.VMEM((t, SSD_WIDTH), F32),
            pltpu.VMEM((t, SSD_WIDTH), BF16),
            pltpu.VMEM((t, SSD_WIDTH), F32),
            pltpu.VMEM((t, SSD_WIDTH), F32),
            pltpu.VMEM((SSD_STATE, SSD_WIDTH), F32),
            pltpu.VMEM((SSD_STATE, SSD_WIDTH), F32),
            pltpu.VMEM((t, LANES), F32),
            pltpu.VMEM((t, LANES), F32),
            pltpu.VMEM((t, LANES), F32),
        ],
        compiler_params=_cparams(("parallel",)),
        name="ssd",
    )(z_l, xbc_l, dt_l, z_c, xbc_c, dt_c, convw, convb, bias_row, alog_row, dskip, normw)


def _mla_t_kernel(*refs, with_latent):
    if with_latent:
        q_ref, kl_ref, vtl_ref, kc_ref, vtc_ref, o_ref, s_scr = refs
        n_lat = kl_ref.shape[1]
    else:
        q_ref, kc_ref, vtc_ref, o_ref, s_scr = refs
        kl_ref = vtl_ref = None
        n_lat = 0
    n_ctx = kc_ref.shape[1]
    kc_ = min(MLA_KEY_CHUNK, n_ctx)
    chunks = [(kl_ref, vtl_ref, j * kc_, j * kc_) for j in range(n_lat // kc_)]
    chunks += [(kc_ref, vtc_ref, j * kc_, n_lat + j * kc_) for j in range(n_ctx // kc_)]

    def scores(h):
        sl = slice(h * MLA_HEAD_PAD, (h + 1) * MLA_HEAD_PAD)
        qh = q_ref[0, :, sl]
        for k_ref, _, src, dst in chunks:
            s_scr[h % 2, dst:dst + kc_, :] = _dot_nt(k_ref[0, src:src + kc_, sl], qh)

    def attend(h):
        vrows = slice(h * MLA_V, (h + 1) * MLA_V)
        m = None
        for _, _, _, dst in chunks:
            mc = s_scr[h % 2, dst:dst + kc_, :].max(axis=0, keepdims=True)
            m = mc if m is None else jnp.maximum(m, mc)
        l = None
        o = None
        for _, vt_ref, src, dst in chunks:
            p = jnp.exp2(s_scr[h % 2, dst:dst + kc_, :] - m)
            lc = p.sum(axis=0, keepdims=True)
            oc = _dot(vt_ref[0, vrows, src:src + kc_], p.astype(BF16))
            l = lc if l is None else l + lc
            o = oc if o is None else o + oc
        return o * (1.0 / l)

    outs = []
    scores(0)
    for h in range(MLA_HEADS):
        if h + 1 < MLA_HEADS:
            scores(h + 1)
        outs.append(attend(h))
    o_ref[0] = jnp.concatenate(outs, axis=0).T.astype(BF16)


def _mla_attention_t(q, k_l, vt_l, k_c, vt_c, tq):
    b, sq, wq = q.shape
    c = k_c.shape[1]
    with_latent = k_l is not None
    s = k_l.shape[1] if with_latent else 0
    in_specs = [pl.BlockSpec((1, tq, wq), lambda bi, ti: (bi, ti, 0))]
    args = [q]
    if with_latent:
        in_specs += [pl.BlockSpec((1, s, wq), lambda bi, ti: (bi, 0, 0)),
                     pl.BlockSpec((1, MLA_WIDTH, s), lambda bi, ti: (bi, 0, 0))]
        args += [k_l, vt_l]
    in_specs += [pl.BlockSpec((1, c, wq), lambda bi, ti: (bi, 0, 0)),
                 pl.BlockSpec((1, MLA_WIDTH, c), lambda bi, ti: (bi, 0, 0))]
    args += [k_c, vt_c]
    return pl.pallas_call(
        functools.partial(_mla_t_kernel, with_latent=with_latent),
        grid=(b, sq // tq),
        in_specs=in_specs,
        out_specs=pl.BlockSpec((1, tq, MLA_WIDTH), lambda bi, ti: (bi, ti, 0)),
        out_shape=jax.ShapeDtypeStruct((b, sq, MLA_WIDTH), BF16),
        scratch_shapes=[pltpu.VMEM((2, s + c, tq), F32)],
        compiler_params=_cparams(("parallel", "parallel")),
        name="mla_attn" if with_latent else "mla_ctx_attn",
    )(*args)


def _outffn_kernel(x_ref, na_ref, ssd_ref, mla_ref, mod_ref, nw_ref, wo_ref, w1_ref, w2_ref, o_ref,
                   x1_s, xn_s, acc_s):
    f = pl.program_id(2)
    mod = mod_ref[0, 0]

    @pl.when(f == 0)
    def _():
        mix = (_dot(na_ref[0], wo_ref[0, 0:NA_WIDTH, :])
               + _dot(ssd_ref[0], wo_ref[0, NA_WIDTH:NA_WIDTH + SSD_WIDTH, :])
               + _dot(mla_ref[0], wo_ref[0, NA_WIDTH + SSD_WIDTH:, :]))
        x1 = x_ref[0] + mod[2:3, :] * mix
        x1_s[...] = x1
        ms = jnp.mean(x1 * x1, axis=-1, keepdims=True)
        xn = x1 * lax.rsqrt(ms + NORM_EPS) * nw_ref[0]
        xn_s[...] = (xn * (1.0 + mod[4:5, :]) + mod[3:4, :]).astype(BF16)
        acc_s[...] = jnp.zeros_like(acc_s)

    hmid = _dot(xn_s[...], w1_ref[0])
    hmid = jnp.square(jnp.maximum(hmid, 0.0)).astype(BF16)
    acc_s[...] += _dot(hmid, w2_ref[0])

    @pl.when(f == pl.num_programs(2) - 1)
    def _():
        o_ref[0] = x1_s[...] + mod[5:6, :] * acc_s[...]


def _outffn(xs, na, ssd, mla, mod, layer, mod_row_fn, norm_w, w_out, w1, w2, tm, tf):
    b, s, d = xs.shape
    depth = w_out.shape[0]
    tile = lambda w: pl.BlockSpec((1, tm, w), lambda bi, ti, fi: (bi, ti, 0))
    return pl.pallas_call(
        _outffn_kernel,
        grid=(b, s // tm, D_FF // tf),
        in_specs=[
            tile(d), tile(NA_WIDTH), tile(SSD_WIDTH), tile(MLA_WIDTH),
            pl.BlockSpec((1, 1, 6, d), lambda bi, ti, fi: (layer, mod_row_fn(bi), 0, 0)),
            pl.BlockSpec((1, 1, d), lambda bi, ti, fi: (layer, 0, 0)),
            pl.BlockSpec((1, MIX_WIDTH, d), lambda bi, ti, fi: (layer, 0, 0)),
            pl.BlockSpec((1, d, tf), lambda bi, ti, fi: (layer, 0, fi)),
            pl.BlockSpec((1, tf, d), lambda bi, ti, fi: (layer, fi, 0)),
        ],
        out_specs=tile(d),
        out_shape=jax.ShapeDtypeStruct((b, s, d), F32),
        scratch_shapes=[pltpu.VMEM((tm, d), F32), pltpu.VMEM((tm, d), BF16), pltpu.VMEM((tm, d), F32)],
        compiler_params=_cparams(("parallel", "parallel", "arbitrary")),
        name="outproj_mlp",
    )(xs, na, ssd, mla, mod, norm_w.reshape(depth, 1, d), w_out, w1, w2)


def _rope_tables(n_tokens):
    pos = jnp.arange(n_tokens)
    axes = jnp.stack([pos // GRID_W, pos % GRID_W], axis=-1).astype(F32)
    n_freq = MLA_ROPE // 4
    inv_freq = ROPE_THETA ** (-jnp.arange(n_freq, dtype=F32) / n_freq)
    ang = axes[:, :, None] * inv_freq
    cos, sin = jnp.cos(ang), jnp.sin(ang)
    cos_l = jnp.stack([cos, cos], axis=2).reshape(n_tokens, MLA_ROPE)
    sin_l = jnp.stack([-sin, sin], axis=2).reshape(n_tokens, MLA_ROPE)

    def place(t, fill):
        left = jnp.full((n_tokens, MLA_NOPE), fill, F32)
        right = jnp.full((n_tokens, LANES - MLA_QK), fill, F32)
        return jnp.concatenate([left, t, right], axis=1)

    return place(cos_l, 1.0), place(sin_l, 0.0)


def _prep_params(w_in, na_qn_w, na_kn_w, ssd_conv_w, ssd_conv_b, ssd_dt_bias, ssd_a_log, ssd_d, ssd_norm_w,
                 mla_cq_norm_w, mla_ckv_norm_w, mla_w_uq, mla_w_ukv, mla_qn_w, mla_kn_w):
    depth = w_in.shape[0]
    d = w_in.shape[1]
    dtw = w_in[..., 2304:2320]
    w_in_p = jnp.concatenate(
        [w_in[..., :2304], w_in[..., 2320:2576], w_in[..., 2576:2704], w_in[..., 2704:2736]]
        + [dtw] * DT_COPIES + [jnp.zeros((depth, d, IN_WIDTH_PAD - 2720 - 16 * DT_COPIES), F32)], axis=-1).astype(BF16)
    qnw = jnp.tile(na_qn_w, (1, NA_HEADS))[:, None, :]
    knw = jnp.tile(na_kn_w, (1, NA_HEADS))[:, None, :]
    gi = np.arange(NA_WIDTH) // NA_HEAD_DIM
    grp = jnp.asarray((gi[:, None] == gi[None, :]).astype(np.float32), BF16)

    convw = jnp.pad(ssd_conv_w, ((0, 0), (0, 8 - SSD_CONV), (0, 0)))
    convb = ssd_conv_b[:, None, :]

    def dt_row(t):
        flat = jnp.tile(t.reshape(depth, 16), (1, DT_COPIES))
        return jnp.pad(flat, ((0, 0), (DT_LANE0, LANES - DT_LANE0 - 16 * DT_COPIES)))[:, None, :]

    bias_row = dt_row(ssd_dt_bias)
    alog_row = dt_row(ssd_a_log)
    dskip = jnp.repeat(ssd_d, SSD_HEAD_DIM, axis=-1)[:, None, :]
    normw = ssd_norm_w[:, None, :]
    ssd_params = (convw, convb, bias_row, alog_row, dskip, normw)

    wuq = mla_w_uq.reshape(depth, MLA_Q_RANK, MLA_HEADS, MLA_QK)
    wuq_p = jnp.pad(wuq, ((0, 0), (0, 0), (0, 0), (0, MLA_HEAD_PAD - MLA_QK))).reshape(depth, MLA_Q_RANK, -1).astype(BF16)
    wukv = mla_w_ukv.reshape(depth, MLA_KV_RANK, MLA_HEADS, MLA_NOPE + MLA_V)
    wk = jnp.pad(wukv[..., :MLA_NOPE], ((0, 0), (0, 0), (0, 0), (0, MLA_HEAD_PAD - MLA_NOPE))).reshape(depth, MLA_KV_RANK, -1)
    wuvt = jnp.swapaxes(wukv[..., MLA_NOPE:].reshape(depth, MLA_KV_RANK, -1), 1, 2).astype(BF16)
    pad_head = lambda t: jnp.tile(jnp.pad(t, ((0, 0), (0, MLA_HEAD_PAD - MLA_QK))), (1, MLA_HEADS))[:, None, :]
    li = np.arange(MLA_HEADS * MLA_HEAD_PAD)
    g96 = jnp.asarray((li[:, None] // MLA_HEAD_PAD == li[None, :] // MLA_HEAD_PAD).astype(np.float32), BF16)
    in_rope = (li % MLA_HEAD_PAD >= MLA_NOPE) & (li % MLA_HEAD_PAD < MLA_QK)
    swp = jnp.asarray(((li[:, None] == (li[None, :] ^ 8)) & in_rope[None, :]).astype(np.float32), BF16)
    mla_weights = (mla_cq_norm_w[:, None, :], mla_ckv_norm_w[:, None, :], wuq_p, wk.astype(BF16), wuvt,
                   pad_head(mla_qn_w), pad_head(mla_kn_w), g96, swp)
    return w_in_p, qnw, knw, grp, ssd_params, mla_weights


def kernel(x, c, ctx, c_ctx, w_ada, b_ada, norm1_w, norm2_w, w_in, w_out, na_qn_w, na_kn_w, na_rpb, ssd_conv_w, ssd_conv_b, ssd_dt_bias, ssd_a_log, ssd_d, ssd_norm_w, mla_cq_norm_w, mla_ckv_norm_w, mla_w_uq, mla_w_ukv, mla_qn_w, mla_kn_w, w_ff1, w_ff2):
    b, s, d = x.shape
    ctx_len = ctx.shape[1]
    depth = w_in.shape[0]
    n_rows = s // GRID_W
    assert d == D_MODEL and s % (8 * GRID_W) == 0 and n_rows > NA_WIN_ROWS and ctx_len % SSD_CHUNK == 0

    mod_rows = ((b + 1 + 7) // 8) * 8
    cc = jnp.concatenate([c, c_ctx[None, :], jnp.zeros((mod_rows - b - 1, d), F32)], axis=0)
    mod = _adaln(cc, w_ada, b_ada).reshape(depth, mod_rows, 6, d)

    w_in_p, qnw, knw, grp, ssd_params, mla_weights = _prep_params(
        w_in, na_qn_w, na_kn_w, ssd_conv_w, ssd_conv_b, ssd_dt_bias, ssd_a_log, ssd_d, ssd_norm_w,
        mla_cq_norm_w, mla_ckv_norm_w, mla_w_uq, mla_w_ukv, mla_qn_w, mla_kn_w)
    w_out_b = w_out.astype(BF16)
    w1_b = w_ff1.astype(BF16)
    w2_b = w_ff2.astype(BF16)
    bias_all = _na_bias(na_rpb, n_rows)
    rope_tabs = _rope_tables(s)

    lat_row = lambda bi: bi
    ctx_row = lambda bi: b
    tm = 512
    tm_c = min(2 * tm, b * ctx_len)
    ctx_flat = ctx.reshape(1, b * ctx_len, d)

    for i in range(depth):
        need_ctx = i < depth - 1
        qkv_l, z_l, xbc_l, dt_l, qm_l, km_l, vt_l = _inproj(
            x, mod, i, lat_row, norm1_w, w_in_p, qnw, knw, grp, mla_weights, rope_tabs, tm)
        qkv_c, z_c, xbc_c, dt_c, qm_c, km_c, vt_c = _inproj(
            ctx_flat.reshape(b, ctx_len, d), mod, i, ctx_row, norm1_w, w_in_p, qnw, knw, grp, mla_weights, None, ctx_len)

        na_l = _na_attention(qkv_l, qkv_c, bias_all, i)
        ssd_l, ssd_c = _ssd(z_l, xbc_l, dt_l, z_c, xbc_c, dt_c, ssd_params, i, need_ctx)
        mla_l = _mla_attention_t(qm_l, km_l, vt_l, km_c, vt_c, 512)

        x = _outffn(x, na_l, ssd_l, mla_l, mod, i, lat_row, norm2_w, w_out_b, w1_b, w2_b, 2 * tm, 1024)
        if need_ctx:
            na_c = _na_ctx_attention(qkv_c)
            mla_c = _mla_attention_t(qm_c, None, None, km_c, vt_c, ctx_len)
            flat = lambda t: t.reshape(1, b * ctx_len, t.shape[-1])
            ctx_flat = _outffn(ctx_flat, flat(na_c), flat(ssd_c), flat(mla_c), mod, i, ctx_row, norm2_w,
                               w_out_b, w1_b, w2_b, tm_c, 1024)
    return x
```

```python
import functools

import jax
import jax.numpy as jnp
import numpy as np
from jax import lax
from jax.experimental import pallas as pl
from jax.experimental.pallas import tpu as pltpu

F32 = jnp.float32
BF16 = jnp.bfloat16

D_MODEL = 1024
GRID_W = 64
D_FF = 4 * D_MODEL
NORM_EPS = 1e-6
ROPE_THETA = 10000.0
LOG2E = 1.4426950408889634
NA_HEADS = 4
NA_HEAD_DIM = 64
NA_WIDTH = 256
NA_WIN_ROWS = 8
NA_WIN_COLS = 16
SSD_HEADS = 8
SSD_HEAD_DIM = 64
SSD_WIDTH = 512
SSD_GROUPS = 2
SSD_STATE = 128
SSD_CONV = 5
SSD_CHUNK = 128
SSD_CONV_CH = 1024
MLA_HEADS = 4
MLA_NOPE = 64
MLA_ROPE = 32
MLA_V = 64
MLA_QK = 96
MLA_Q_RANK = 256
MLA_KV_RANK = 128
MLA_WIDTH = 256
MIX_WIDTH = 1024
IN_WIDTH = 2736

LANES = 128
MLA_HEAD_PAD = 128
IN_WIDTH_PAD = 2816
DT_LANE0 = 32
DT_COPIES = 3
MLA_KEY_CHUNK = 256
VMEM_LIMIT = 56 * 1024 * 1024

NT_DIMS = (((1,), (1,)), ((), ()))
TN_DIMS = (((0,), (0,)), ((), ()))


def _dot(a, b):
    return jnp.dot(a, b, preferred_element_type=F32)


def _dot_nt(a, b):
    return lax.dot_general(a, b, NT_DIMS, preferred_element_type=F32)


def _dot_tn(a, b):
    return lax.dot_general(a, b, TN_DIMS, preferred_element_type=F32)


def _sigmoid(x):
    return 1.0 / (1.0 + jnp.exp(-x))


def _silu(x):
    return x * _sigmoid(x)


def _softplus(x):
    return jnp.maximum(x, 0.0) + jnp.log1p(jnp.exp(-jnp.abs(x)))


def _split3(x):
    x0 = x.astype(BF16)
    r = x - x0.astype(F32)
    x1 = r.astype(BF16)
    r = r - x1.astype(F32)
    return x0, x1, r.astype(BF16)


def _lane_parts(x, lane):
    p0 = x.astype(BF16).astype(F32)
    r = x - p0
    p1 = r.astype(BF16).astype(F32)
    p2 = r - p1
    return jnp.where(lane < DT_LANE0 + 16, p0, jnp.where(lane < DT_LANE0 + 32, p1, p2)).astype(BF16)


def _aligned(x, m):
    return x if isinstance(x, int) else pl.multiple_of(x, m)


def _cparams(sem):
    return pltpu.CompilerParams(dimension_semantics=sem, vmem_limit_bytes=VMEM_LIMIT)


def _adaln_kernel(c_ref, w_ref, b_ref, o_ref):
    act = _silu(c_ref[...]).astype(BF16)
    o_ref[0] = _dot(act, w_ref[0].astype(BF16)) + b_ref[0]


def _adaln(cc, w_ada, b_ada):
    depth, d, n = w_ada.shape
    rows = cc.shape[0]
    tn = 1536
    return pl.pallas_call(
        _adaln_kernel,
        grid=(depth, n // tn),
        in_specs=[
            pl.BlockSpec((rows, d), lambda l, j: (0, 0)),
            pl.BlockSpec((1, d, tn), lambda l, j: (l, 0, j)),
            pl.BlockSpec((1, 1, tn), lambda l, j: (l, 0, j)),
        ],
        out_specs=pl.BlockSpec((1, rows, tn), lambda l, j: (l, 0, j)),
        out_shape=jax.ShapeDtypeStruct((depth, rows, n), F32),
        compiler_params=_cparams(("parallel", "parallel")),
        name="adaln",
    )(cc, w_ada, b_ada.reshape(depth, 1, n))


def _inproj_kernel(x_ref, mod_ref, nw_ref, w_ref, qnw_ref, knw_ref, grp_ref,
                   cqn_ref, ckvn_ref, wuq_ref, wuk_ref, wuvt_ref, mqnw_ref, mknw_ref, g96_ref, swp_ref, *rest, rope):
    if rope:
        cos_ref, sin_ref = rest[:2]
        rest = rest[2:]
    qkv_ref, z_ref, xbc_ref, dt_ref, qm_ref, km_ref, vt_ref = rest

    def rms(t, w):
        ms = jnp.mean(t * t, axis=-1, keepdims=True)
        return t * lax.rsqrt(ms + NORM_EPS) * w

    mod = mod_ref[0, 0]
    h = (rms(x_ref[0], nw_ref[0]) * (1.0 + mod[1:2, :]) + mod[0:1, :]).astype(BF16)
    u = _dot(h, w_ref[0])

    grp = grp_ref[...]

    def head_norm(t, w):
        ss = _dot((t * t).astype(BF16), grp)
        return t * lax.rsqrt(ss * (1.0 / NA_HEAD_DIM) + NORM_EPS) * w

    q = head_norm(u[:, 0:256], qnw_ref[0]) * (NA_HEAD_DIM ** -0.5 * LOG2E)
    k = head_norm(u[:, 256:512], knw_ref[0])
    qkv_ref[0, :, 0:256] = q.astype(BF16)
    qkv_ref[0, :, 256:512] = k.astype(BF16)
    qkv_ref[0, :, 512:768] = u[:, 512:768].astype(BF16)
    z_ref[0] = u[:, 768:1280].astype(BF16)
    xbc_ref[0] = u[:, 1280:2304].astype(BF16)
    tail = u[:, 2688:2816]
    dt_ref[0] = tail

    g96 = g96_ref[...]
    swp = swp_ref[...]
    nh = MLA_HEADS

    def inv_rms96(ss):
        return lax.rsqrt(ss * (1.0 / MLA_QK) + NORM_EPS)

    def rotate(t):
        if not rope:
            return t
        reps = t.shape[1] // LANES
        sw = swp[:t.shape[1], :t.shape[1]]
        hi = t.astype(BF16)
        lo = (t - hi.astype(F32)).astype(BF16)
        partner = _dot(hi, sw) + _dot(lo, sw)
        return t * jnp.concatenate([cos_ref[...]] * reps, axis=1) + partner * jnp.concatenate([sin_ref[...]] * reps, axis=1)

    qf = _dot(rms(u[:, 2304:2560], cqn_ref[0]).astype(BF16), wuq_ref[0])
    ckvn = rms(u[:, 2560:2688], ckvn_ref[0]).astype(BF16)
    kn = _dot(ckvn, wuk_ref[0])
    vt_ref[0] = _dot_nt(wuvt_ref[0], ckvn).astype(BF16)
    lane = lax.broadcasted_iota(jnp.int32, (1, LANES), 1)
    kr = jnp.where((lane >= MLA_NOPE) & (lane < MLA_QK), pltpu.roll(tail, MLA_NOPE, axis=1), 0.0)
    mqnw = mqnw_ref[0]
    mknw = mknw_ref[0]

    q_ss = _dot((qf * qf).astype(BF16), g96)
    qn = rotate(qf * inv_rms96(q_ss) * mqnw)
    qm_ref[0] = (qn * (MLA_QK ** -0.5 * LOG2E)).astype(BF16)
    kr_ss = jnp.sum(kr * kr, axis=-1, keepdims=True)
    kr_rot = rotate(kr * mknw[:, :LANES])
    k_ss = _dot((kn * kn).astype(BF16), g96) + kr_ss
    km_ref[0] = ((kn * mknw + jnp.concatenate([kr_rot] * nh, axis=1)) * inv_rms96(k_ss)).astype(BF16)


def _inproj(xs, mod, layer, mod_row_fn, norm_w, w_in_p, qnw, knw, grp, mla_weights, rope_tabs, tm):
    b, s, d = xs.shape
    depth = w_in_p.shape[0]
    cqn_w, ckvn_w, wuq_p, wuk_p, wuvt_p, mqnw_p, mknw_p, g96, swp = mla_weights
    rope = rope_tabs is not None
    hw = MLA_HEADS * MLA_HEAD_PAD
    outs = (
        jax.ShapeDtypeStruct((b, s, 768), BF16),
        jax.ShapeDtypeStruct((b, s, SSD_WIDTH), BF16),
        jax.ShapeDtypeStruct((b, s, SSD_CONV_CH), BF16),
        jax.ShapeDtypeStruct((b, s, LANES), F32),
        jax.ShapeDtypeStruct((b, s, hw), BF16),
        jax.ShapeDtypeStruct((b, s, hw), BF16),
        jax.ShapeDtypeStruct((b, MLA_WIDTH, s), BF16),
    )
    tile = lambda w: pl.BlockSpec((1, tm, w), lambda bi, ti: (bi, ti, 0))
    tile_t = lambda r: pl.BlockSpec((1, r, tm), lambda bi, ti: (bi, 0, ti))

    def par(shape):
        return pl.BlockSpec((1,) + shape, lambda bi, ti: (layer,) + (0,) * len(shape))

    in_specs = [
        tile(d),
        pl.BlockSpec((1, 1, 6, d), lambda bi, ti: (layer, mod_row_fn(bi), 0, 0)),
        par((1, d)), par((d, IN_WIDTH_PAD)), par((1, 256)), par((1, 256)),
        pl.BlockSpec((256, 256), lambda bi, ti: (0, 0)),
        par((1, MLA_Q_RANK)), par((1, MLA_KV_RANK)), par((MLA_Q_RANK, hw)), par((MLA_KV_RANK, hw)),
        par((MLA_WIDTH, MLA_KV_RANK)),
        par((1, hw)), par((1, hw)),
        pl.BlockSpec((hw, hw), lambda bi, ti: (0, 0)), pl.BlockSpec((hw, hw), lambda bi, ti: (0, 0)),
    ]
    args = [xs, mod, norm_w.reshape(depth, 1, d), w_in_p, qnw, knw, grp, cqn_w, ckvn_w, wuq_p, wuk_p, wuvt_p, mqnw_p,
            mknw_p, g96, swp]
    if rope:
        in_specs += [pl.BlockSpec((tm, LANES), lambda bi, ti: (ti, 0))] * 2
        args += list(rope_tabs)
    return pl.pallas_call(
        functools.partial(_inproj_kernel, rope=rope),
        grid=(b, s // tm),
        in_specs=in_specs,
        out_specs=[tile(768), tile(SSD_WIDTH), tile(SSD_CONV_CH), tile(LANES), tile(hw), tile(hw), tile_t(MLA_WIDTH)],
        out_shape=outs,
        compiler_params=_cparams(("parallel", "parallel")),
        name="inproj",
    )(*args)


NA_VARIANT_ROWS = (0, 1, 2, 3, 4, -3, -2, -1)


def _na_bias_kernel(rpb_ref, o_ref, t_ref, *, n_rows):
    nrow_off = 2 * NA_WIN_ROWS - 1
    ncol_off = 2 * NA_WIN_COLS - 1
    base = (pl.program_id(0) * NA_HEADS + pl.program_id(1)) * (nrow_off * ncol_off)
    qi = lax.broadcasted_iota(jnp.int32, (GRID_W, LANES), 0)
    li = lax.broadcasted_iota(jnp.int32, (GRID_W, LANES), 1)
    kcol = li & (GRID_W - 1)
    colidx = jnp.clip(kcol - qi, -(NA_WIN_COLS - 1), NA_WIN_COLS - 1) + (NA_WIN_COLS - 1)
    cstart = jnp.clip(qi - NA_WIN_COLS // 2, 0, GRID_W - NA_WIN_COLS)
    valid = (kcol >= cstart) & (kcol < cstart + NA_WIN_COLS)
    for d in range(nrow_off):
        acc = jnp.zeros((GRID_W, LANES), F32)
        for j in range(ncol_off):
            acc = jnp.where(colidx == j, rpb_ref[base + d * ncol_off + j], acc)
        t_ref[d] = jnp.where(valid, acc * LOG2E, -jnp.inf)
    for v, r_rep in enumerate(NA_VARIANT_ROWS):
        r = r_rep if r_rep >= 0 else n_rows + r_rep
        rs = min(max(r - NA_WIN_ROWS // 2, 0), n_rows - NA_WIN_ROWS)
        for p in range(NA_WIN_ROWS // 2):
            d0 = rs + 2 * p - r + (NA_WIN_ROWS - 1)
            o_ref[0, v, :, p * LANES:(p + 1) * LANES] = jnp.where(li < GRID_W, t_ref[d0], t_ref[d0 + 1])


def _na_bias(rpb, n_rows):
    depth, heads = rpb.shape[0], rpb.shape[1]
    n_win = NA_WIN_ROWS * GRID_W
    return pl.pallas_call(
        functools.partial(_na_bias_kernel, n_rows=n_rows),
        grid=(depth, heads),
        in_specs=[pl.BlockSpec(memory_space=pltpu.SMEM)],
        out_specs=pl.BlockSpec((1, 8, GRID_W, n_win), lambda l, h: (l, 0, h, 0)),
        out_shape=jax.ShapeDtypeStruct((depth, 8, heads * GRID_W, n_win), F32),
        scratch_shapes=[pltpu.VMEM((2 * NA_WIN_ROWS - 1, GRID_W, LANES), F32)],
        compiler_params=_cparams(("parallel", "parallel")),
        name="na_bias",
    )(rpb.reshape(-1))


def _head_masks():
    lane = lax.broadcasted_iota(jnp.int32, (1, NA_WIDTH), 1)
    return [jnp.where((lane >= h * NA_HEAD_DIM) & (lane < (h + 1) * NA_HEAD_DIM), 1.0, 0.0) for h in range(NA_HEADS)]


def _stacked_scores(q, keys, biases, hms):
    qs = jnp.concatenate([q * hm.astype(BF16) for hm in hms], axis=0)
    scores = []
    for k, b in zip(keys, biases):
        s = _dot_nt(qs, k)
        scores.append(s if b is None else s + b)
    return scores


def _stacked_attend(scores, values, hms):
    m_rows = scores[0].shape[0] // NA_HEADS
    m = scores[0].max(axis=-1, keepdims=True)
    for s in scores[1:]:
        m = jnp.maximum(m, s.max(axis=-1, keepdims=True))
    l = None
    o = None
    for s, v in zip(scores, values):
        p = jnp.exp2(s - m)
        ls = p.sum(axis=-1, keepdims=True)
        os_ = _dot(p.astype(BF16), v)
        l = ls if l is None else l + ls
        o = os_ if o is None else o + os_
    o = o * (1.0 / l)
    out = o[0:m_rows] * hms[0]
    for h in range(1, NA_HEADS):
        out = out + o[h * m_rows:(h + 1) * m_rows] * hms[h]
    return out


def _na_kernel(q_ref, k_ref, v_ref, kc_ref, vc_ref, bias_ref, o_ref, *, rows_per_step, n_rows):
    blk = pl.program_id(1)
    kc = kc_ref[0]
    vc = vc_ref[0]
    n_win = NA_WIN_ROWS * GRID_W
    hms = _head_masks()

    def window_start(i):
        r = blk * rows_per_step + i
        rs = jnp.clip(r - NA_WIN_ROWS // 2, 0, n_rows - NA_WIN_ROWS)
        var = jnp.where(r < NA_WIN_ROWS // 2, r,
                        jnp.where(r > n_rows - NA_WIN_ROWS // 2, r - (n_rows - NA_WIN_ROWS), NA_WIN_ROWS // 2))
        return pl.multiple_of(rs * GRID_W, GRID_W), var

    def row_scores(i):
        start, var = window_start(i)
        q = q_ref[0, i * GRID_W:(i + 1) * GRID_W, :]
        return _stacked_scores(q, [k_ref[0, pl.ds(start, n_win), :], kc], [bias_ref[0, var], None], hms)

    nxt = row_scores(0)
    for i in range(rows_per_step):
        cur = nxt
        if i + 1 < rows_per_step:
            nxt = row_scores(i + 1)
        start, _ = window_start(i)
        out = _stacked_attend(cur, [v_ref[0, pl.ds(start, n_win), :], vc], hms)
        o_ref[0, i * GRID_W:(i + 1) * GRID_W, :] = out.astype(BF16)


def _na_attention(qkv_l, qkv_c, bias_all, layer, rows_per_step=8):
    b, s, _ = qkv_l.shape
    ctx_len = qkv_c.shape[1]
    n_rows = s // GRID_W
    tq = rows_per_step * GRID_W
    return pl.pallas_call(
        functools.partial(_na_kernel, rows_per_step=rows_per_step, n_rows=n_rows),
        grid=(b, n_rows // rows_per_step),
        in_specs=[
            pl.BlockSpec((1, tq, 256), lambda bi, ti: (bi, ti, 0)),
            pl.BlockSpec((1, s, 256), lambda bi, ti: (bi, 0, 1)),
            pl.BlockSpec((1, s, 256), lambda bi, ti: (bi, 0, 2)),
            pl.BlockSpec((1, ctx_len, 256), lambda bi, ti: (bi, 0, 1)),
            pl.BlockSpec((1, ctx_len, 256), lambda bi, ti: (bi, 0, 2)),
            pl.BlockSpec((1,) + bias_all.shape[1:], lambda bi, ti: (layer, 0, 0, 0)),
        ],
        out_specs=pl.BlockSpec((1, tq, 256), lambda bi, ti: (bi, ti, 0)),
        out_shape=jax.ShapeDtypeStruct((b, s, NA_WIDTH), BF16),
        compiler_params=_cparams(("parallel", "parallel")),
        name="na_attn",
    )(qkv_l, qkv_l, qkv_l, qkv_c, qkv_c, bias_all)


def _na_ctx_kernel(q_ref, k_ref, v_ref, o_ref):
    hms = _head_masks()
    scores = _stacked_scores(q_ref[0], [k_ref[0]], [None], hms)
    o_ref[0] = _stacked_attend(scores, [v_ref[0]], hms).astype(BF16)


def _na_ctx_attention(qkv_c):
    b, c, _ = qkv_c.shape
    spec = lambda j: pl.BlockSpec((1, c, 256), lambda bi: (bi, 0, j))
    return pl.pallas_call(
        _na_ctx_kernel,
        grid=(b,),
        in_specs=[spec(0), spec(1), spec(2)],
        out_specs=spec(0),
        out_shape=jax.ShapeDtypeStruct((b, c, NA_WIDTH), BF16),
        compiler_params=_cparams(("parallel",)),
        name="na_ctx_attn",
    )(qkv_c, qkv_c, qkv_c)


def _ssd_kernel(z_l, xbc_l, dt_l, z_c, xbc_c, dt_c,
                convw_ref, convb_ref, bias_row_ref, alog_row_ref, dskip_ref, normw_ref, out_l, out_c,
                xs_s, bc_s, yf_s, yb_s, stf_s, stb_s, dt_s, acf_s, acb_s, *, n_lat, n_ctx, need_ctx):
    q = SSD_CHUNK
    lat_row0 = n_ctx * q

    convw = convw_ref[0]
    convb = convb_ref[0]
    win = q + 32
    wi = lax.broadcasted_iota(jnp.int32, (q, win), 0)
    wr = lax.broadcasted_iota(jnp.int32, (q, win), 1)
    mid = SSD_CONV // 2

    def shift_mats(off):
        return [None if j == mid else jnp.where(wr == wi + (off + j - mid), 1.0, 0.0).astype(BF16)
                for j in range(SSD_CONV)]

    def conv_chunk(xref, lo, off, smats, dst):
        w_in = xref[0, pl.ds(lo, win), :]
        acc = convb + w_in[off:off + q].astype(F32) * convw[mid:mid + 1, :]
        for j in range(SSD_CONV):
            if j != mid:
                acc = acc + _dot(smats[j], w_in) * convw[j:j + 1, :]
        act = _silu(acc)
        xs_s[pl.ds(dst, q), :] = act[:, :SSD_WIDTH]
        bc_s[pl.ds(dst, q), :] = act[:, SSD_WIDTH:].astype(BF16)

    def conv_seq(xref, n, row0):
        conv_chunk(xref, 0, 0, shift_mats(0), row0)
        conv_chunk(xref, (n - 1) * q - 32, 32, shift_mats(32), row0 + (n - 1) * q)
        if n > 2:
            smats = shift_mats(16)

            def body(c, carry):
                conv_chunk(xref, pl.multiple_of(c * q - 16, 16), 16, smats, pl.multiple_of(row0 + c * q, q))
                return carry
            lax.fori_loop(1, n - 1, body, 0, unroll=2)

    conv_seq(xbc_c, n_ctx, 0)
    conv_seq(xbc_l, n_lat, lat_row0)

    lane = lax.broadcasted_iota(jnp.int32, (1, LANES), 1)
    dmask = (lane >= DT_LANE0) & (lane < DT_LANE0 + 16 * DT_COPIES)
    bias_row = bias_row_ref[0]
    a_row = jnp.where(dmask, -jnp.exp(alog_row_ref[0]), 0.0)
    ri = lax.broadcasted_iota(jnp.int32, (q, q), 0)
    ci = lax.broadcasted_iota(jnp.int32, (q, q), 1)
    lo_mask = ci <= ri
    up_mask = ri <= ci
    tri_lo = jnp.where(lo_mask, 1.0, 0.0).astype(BF16)
    tri_up = jnp.where(up_mask, 1.0, 0.0).astype(BF16)
    sel_r = lax.broadcasted_iota(jnp.int32, (LANES, SSD_HEADS * LANES), 0)
    sel_c = lax.broadcasted_iota(jnp.int32, (LANES, SSD_HEADS * LANES), 1)
    sel64_r = lax.broadcasted_iota(jnp.int32, (LANES, SSD_WIDTH), 0)
    sel64_c = lax.broadcasted_iota(jnp.int32, (LANES, SSD_WIDTH), 1)
    lane_lt64 = lax.broadcasted_iota(jnp.int32, (1, LANES), 1) < SSD_HEAD_DIM

    def sel_matrices(direction):
        in3 = (sel_r >= DT_LANE0) & (sel_r < DT_LANE0 + 48)
        sel128 = jnp.where(in3 & (((sel_r - DT_LANE0) & 15) == direction * SSD_HEADS + (sel_c >> 7)), 1.0, 0.0)
        in2 = (sel64_r >= DT_LANE0) & (sel64_r < DT_LANE0 + 32)
        sel64 = jnp.where(in2 & (((sel64_r - DT_LANE0) & 15) == direction * SSD_HEADS + (sel64_c >> 6)), 1.0, 0.0)
        return sel128.astype(BF16), sel64.astype(BF16)

    sels = (sel_matrices(0), sel_matrices(1))
    tri3_both = jnp.concatenate([jnp.concatenate([tri_lo] * 3, axis=1), jnp.concatenate([tri_up] * 3, axis=1)], axis=0)
    masks = (lo_mask, up_mask)

    y_refs = (yf_s, yb_s)
    st_refs = (stf_s, stb_s)
    stf_s[...] = jnp.zeros_like(stf_s)
    stb_s[...] = jnp.zeros_like(stb_s)

    ac_refs = (acf_s, acb_s)

    def decay_chunk(dt_ref, c, row0):
        rows = pl.ds(_aligned(row0 + c * q, q), q)
        raw = dt_ref[0, pl.ds(_aligned(c * q, q), q), :]
        dt = jnp.where(dmask, _softplus(raw + bias_row), 0.0)
        a0, a1, a2 = _split3(dt * a_row)
        both = _dot(tri3_both, jnp.concatenate([a0, a1, a2], axis=0))
        dt_s[rows, :] = dt
        acf_s[rows, :] = both[0:q]
        acb_s[rows, :] = both[q:2 * q]

    for i in range(n_ctx):
        decay_chunk(dt_c, i, 0)

    def decay_body(c, carry):
        decay_chunk(dt_l, c, lat_row0)
        return carry

    lax.fori_loop(0, n_lat, decay_body, 0, unroll=2)

    def chunk(c, row0, direction, compute_y):
        rows = pl.ds(_aligned(row0 + c * q, q), q)
        x = xs_s[rows, :]
        bc = bc_s[rows, :]
        sel128, sel64 = sels[direction]
        dt = dt_s[rows, :]
        acum = ac_refs[direction][rows, :]
        a_exp = _dot(_lane_parts(acum, lane), sel128)
        dt64 = _dot(_lane_parts(dt, lane), sel64)
        st = st_refs[direction][...]
        grp_w = SSD_WIDTH // SSD_GROUPS
        b_g = [bc[:, g * SSD_STATE:(g + 1) * SSD_STATE] for g in range(SSD_GROUPS)]
        c_g = [bc[:, (SSD_GROUPS + g) * SSD_STATE:(SSD_GROUPS + g + 1) * SSD_STATE] for g in range(SSD_GROUPS)]
        if compute_y:
            stb = st.astype(BF16)
            gmats = [_dot_nt(c_g[g], b_g[g]) for g in range(SSD_GROUPS)]
            offs = [_dot(c_g[g], stb[:, g * grp_w:(g + 1) * grp_w]) for g in range(SSD_GROUPS)]
        yield

        ac64 = jnp.concatenate(
            [jnp.where(lane_lt64, a_exp[:, (2 * j) * LANES:(2 * j + 1) * LANES],
                       a_exp[:, (2 * j + 1) * LANES:(2 * j + 2) * LANES]) for j in range(SSD_HEADS // 2)], axis=1)
        acum_t = acum.T
        end = q - 1 if direction == 0 else 0
        ac_end = ac64[end:end + 1, :]
        xd = x * dt64
        xdb = xd.astype(BF16)
        xdw = (xd * jnp.exp(ac_end - ac64)).astype(BF16)
        yield

        if compute_y:
            parts = []
            for pair in range(SSD_HEADS // 2):
                h0 = 2 * pair
                gmat = gmats[h0 // (SSD_HEADS // SSD_GROUPS)]
                xp = xdb[:, h0 * SSD_HEAD_DIM:(h0 + 2) * SSD_HEAD_DIM]
                res = []
                for hh in (h0, h0 + 1):
                    row = DT_LANE0 + direction * SSD_HEADS + hh
                    dmat = a_exp[:, hh * LANES:(hh + 1) * LANES] - acum_t[row:row + 1, :]
                    lmat = jnp.exp(jnp.where(masks[direction], dmat, -jnp.inf))
                    res.append(_dot((gmat * lmat).astype(BF16), xp))
                parts.append(jnp.where(lane_lt64, res[0], res[1]))
                yield
            y = jnp.concatenate(parts, axis=1) + jnp.concatenate(offs, axis=1) * jnp.exp(ac64)
            y_refs[direction][rows, :] = y

        new = [_dot_tn(b_g[g], xdw[:, g * grp_w:(g + 1) * grp_w]) for g in range(SSD_GROUPS)]
        st_refs[direction][...] = st * jnp.exp(ac_end) + jnp.concatenate(new, axis=1)
        yield

    def run_interleaved(*gens):
        live = list(gens)
        while live:
            for g in list(live):
                try:
                    next(g)
                except StopIteration:
                    live.remove(g)

    for i in range(n_ctx):
        run_interleaved(chunk(i, 0, 0, need_ctx), chunk(n_ctx - 1 - i, 0, 1, need_ctx))

    def lat_body(i, carry):
        run_interleaved(chunk(i, lat_row0, 0, True), chunk(n_lat - 1 - i, lat_row0, 1, True))
        return carry

    lax.fori_loop(0, n_lat, lat_body, 0, unroll=2)

    dskip = dskip_ref[0]
    normw = normw_ref[0]

    def finish(z_ref, o_ref, c, row0):
        rows = pl.ds(_aligned(row0 + c * q, q), q)
        crow = pl.ds(_aligned(c * q, q), q)
        y = yf_s[rows, :] + yb_s[rows, :] + dskip * xs_s[rows, :]
        y = y * _silu(z_ref[0, crow, :].astype(F32))
        ms = jnp.mean(y * y, axis=-1, keepdims=True)
        o_ref[0, crow, :] = (y * lax.rsqrt(ms + NORM_EPS) * normw).astype(BF16)

    if need_ctx:
        for i in range(n_ctx):
            finish(z_c, out_c, i, 0)
    else:
        out_c[...] = jnp.zeros_like(out_c)

    def fin_body(c, carry):
        finish(z_l, out_l, c, lat_row0)
        return carry

    lax.fori_loop(0, n_lat, fin_body, 0)


def _ssd(z_l, xbc_l, dt_l, z_c, xbc_c, dt_c, params, layer, need_ctx):
    b, s, _ = z_l.shape
    c = z_c.shape[1]
    n_lat, n_ctx = s // SSD_CHUNK, c // SSD_CHUNK
    t = s + c
    convw, convb, bias_row, alog_row, dskip, normw = params

    def seq_specs(n):
        return [
            pl.BlockSpec((1, n, SSD_WIDTH), lambda bi: (bi, 0, 0)),
            pl.BlockSpec((1, n, SSD_CONV_CH), lambda bi: (bi, 0, 0)),
            pl.BlockSpec((1, n, LANES), lambda bi: (bi, 0, 0)),
        ]

    def par(shape):
        return pl.BlockSpec((1,) + shape, lambda bi: (layer,) + (0,) * len(shape))

    return pl.pallas_call(
        functools.partial(_ssd_kernel, n_lat=n_lat, n_ctx=n_ctx, need_ctx=need_ctx),
        grid=(b,),
        in_specs=seq_specs(s) + seq_specs(c) + [
            par((8, SSD_CONV_CH)), par((1, SSD_CONV_CH)), par((1, LANES)), par((1, LANES)),
            par((1, SSD_WIDTH)), par((1, SSD_WIDTH)),
        ],
        out_specs=[
            pl.BlockSpec((1, s, SSD_WIDTH), lambda bi: (bi, 0, 0)),
            pl.BlockSpec((1, c, SSD_WIDTH), lambda bi: (bi, 0, 0)),
        ],
        out_shape=(jax.ShapeDtypeStruct((b, s, SSD_WIDTH), BF16), jax.ShapeDtypeStruct((b, c, SSD_WIDTH), BF16)),
        scratch_shapes=[
            pltpu.VMEM((t, SSD_WIDTH), F32),
            pltpu.VMEM((t, SSD_WIDTH), BF16),
            pltpu.VMEM((t, SSD_WIDTH), F32),
            pltpu.VMEM((t, SSD_WIDTH), F32),
            pltpu.VMEM((SSD_STATE, SSD_WIDTH), F32),
            pltpu.VMEM((SSD_STATE, SSD_WIDTH), F32),
            pltpu.VMEM((t, LANES), F32),
            pltpu.VMEM((t, LANES), F32),
            pltpu.VMEM((t, LANES), F32),
        ],
        compiler_params=_cparams(("parallel",)),
        name="ssd",
    )(z_l, xbc_l, dt_l, z_c, xbc_c, dt_c, convw, convb, bias_row, alog_row, dskip, normw)


def _mla_t_kernel(*refs, with_latent):
    if with_latent:
        q_ref, kl_ref, vtl_ref, kc_ref, vtc_ref, o_ref, s_scr = refs
        n_lat = kl_ref.shape[1]
    else:
        q_ref, kc_ref, vtc_ref, o_ref, s_scr = refs
        kl_ref = vtl_ref = None
        n_lat = 0
    n_ctx = kc_ref.shape[1]
    kc_ = min(MLA_KEY_CHUNK, n_ctx)
    chunks = [(kl_ref, vtl_ref, j * kc_, j * kc_) for j in range(n_lat // kc_)]
    chunks += [(kc_ref, vtc_ref, j * kc_, n_lat + j * kc_) for j in range(n_ctx // kc_)]

    def scores(h):
        sl = slice(h * MLA_HEAD_PAD, (h + 1) * MLA_HEAD_PAD)
        qh = q_ref[0, :, sl]
        for k_ref, _, src, dst in chunks:
            s_scr[h % 2, dst:dst + kc_, :] = _dot_nt(k_ref[0, src:src + kc_, sl], qh)

    def attend(h):
        vrows = slice(h * MLA_V, (h + 1) * MLA_V)
        m = None
        for _, _, _, dst in chunks:
            mc = s_scr[h % 2, dst:dst + kc_, :].max(axis=0, keepdims=True)
            m = mc if m is None else jnp.maximum(m, mc)
        l = None
        o = None
        for _, vt_ref, src, dst in chunks:
            p = jnp.exp2(s_scr[h % 2, dst:dst + kc_, :] - m)
            lc = p.sum(axis=0, keepdims=True)
            oc = _dot(vt_ref[0, vrows, src:src + kc_], p.astype(BF16))
            l = lc if l is None else l + lc
            o = oc if o is None else o + oc
        return o * (1.0 / l)

    outs = []
    scores(0)
    for h in range(MLA_HEADS):
        if h + 1 < MLA_HEADS:
            scores(h + 1)
        outs.append(attend(h))
    o_ref[0] = jnp.concatenate(outs, axis=0).T.astype(BF16)


def _mla_attention_t(q, k_l, vt_l, k_c, vt_c, tq):
    b, sq, wq = q.shape
    c = k_c.shape[1]
    with_latent = k_l is not None
    s = k_l.shape[1] if with_latent else 0
    in_specs = [pl.BlockSpec((1, tq, wq), lambda bi, ti: (bi, ti, 0))]
    args = [q]
    if with_latent:
        in_specs += [pl.BlockSpec((1, s, wq), lambda bi, ti: (bi, 0, 0)),
                     pl.BlockSpec((1, MLA_WIDTH, s), lambda bi, ti: (bi, 0, 0))]
        args += [k_l, vt_l]
    in_specs += [pl.BlockSpec((1, c, wq), lambda bi, ti: (bi, 0, 0)),
                 pl.BlockSpec((1, MLA_WIDTH, c), lambda bi, ti: (bi, 0, 0))]
    args += [k_c, vt_c]
    return pl.pallas_call(
        functools.partial(_mla_t_kernel, with_latent=with_latent),
        grid=(b, sq // tq),
        in_specs=in_specs,
        out_specs=pl.BlockSpec((1, tq, MLA_WIDTH), lambda bi, ti: (bi, ti, 0)),
        out_shape=jax.ShapeDtypeStruct((b, sq, MLA_WIDTH), BF16),
        scratch_shapes=[pltpu.VMEM((2, s + c, tq), F32)],
        compiler_params=_cparams(("parallel", "parallel")),
        name="mla_attn" if with_latent else "mla_ctx_attn",
    )(*args)


def _outffn_resident_kernel(x_ref, na_ref, ssd_ref, mla_ref, mod_ref, nw_ref, wo_ref, w1_ref, w2_ref, o_ref, *, tf):
    mod = mod_ref[0, 0]
    mix = (_dot(na_ref[0], wo_ref[0, 0:NA_WIDTH, :])
           + _dot(ssd_ref[0], wo_ref[0, NA_WIDTH:NA_WIDTH + SSD_WIDTH, :])
           + _dot(mla_ref[0], wo_ref[0, NA_WIDTH + SSD_WIDTH:, :]))
    x1 = x_ref[0] + mod[2:3, :] * mix
    ms = jnp.mean(x1 * x1, axis=-1, keepdims=True)
    xn = x1 * lax.rsqrt(ms + NORM_EPS) * nw_ref[0]
    xn = (xn * (1.0 + mod[4:5, :]) + mod[3:4, :]).astype(BF16)
    acc = None
    for f in range(D_FF // tf):
        hmid = _dot(xn, w1_ref[0, :, f * tf:(f + 1) * tf])
        hmid = jnp.square(jnp.maximum(hmid, 0.0)).astype(BF16)
        part = _dot(hmid, w2_ref[0, f * tf:(f + 1) * tf, :])
        acc = part if acc is None else acc + part
    o_ref[0] = x1 + mod[5:6, :] * acc


def _outffn_resident(xs, na, ssd, mla, mod, layer, mod_row_fn, norm_w, w_out, w1, w2, tm, tf):
    b, s, d = xs.shape
    depth = w_out.shape[0]
    tile = lambda w: pl.BlockSpec((1, tm, w), lambda bi, ti: (bi, ti, 0))
    once = lambda shape: pl.BlockSpec((1,) + shape, lambda bi, ti: (layer, 0, 0), pipeline_mode=pl.Buffered(1))
    return pl.pallas_call(
        functools.partial(_outffn_resident_kernel, tf=tf),
        grid=(b, s // tm),
        in_specs=[
            tile(d), tile(NA_WIDTH), tile(SSD_WIDTH), tile(MLA_WIDTH),
            pl.BlockSpec((1, 1, 6, d), lambda bi, ti: (layer, mod_row_fn(bi), 0, 0)),
            pl.BlockSpec((1, 1, d), lambda bi, ti: (layer, 0, 0)),
            once((MIX_WIDTH, d)), once((d, D_FF)), once((D_FF, d)),
        ],
        out_specs=tile(d),
        out_shape=jax.ShapeDtypeStruct((b, s, d), F32),
        compiler_params=_cparams(("parallel", "parallel")),
        name="outproj_mlp_res",
    )(xs, na, ssd, mla, mod, norm_w.reshape(depth, 1, d), w_out, w1, w2)


def _rope_tables(n_tokens):
    pos = jnp.arange(n_tokens)
    axes = jnp.stack([pos // GRID_W, pos % GRID_W], axis=-1).astype(F32)
    n_freq = MLA_ROPE // 4
    inv_freq = ROPE_THETA ** (-jnp.arange(n_freq, dtype=F32) / n_freq)
    ang = axes[:, :, None] * inv_freq
    cos, sin = jnp.cos(ang), jnp.sin(ang)
    cos_l = jnp.stack([cos, cos], axis=2).reshape(n_tokens, MLA_ROPE)
    sin_l = jnp.stack([-sin, sin], axis=2).reshape(n_tokens, MLA_ROPE)

    def place(t, fill):
        left = jnp.full((n_tokens, MLA_NOPE), fill, F32)
        right = jnp.full((n_tokens, LANES - MLA_QK), fill, F32)
        return jnp.concatenate([left, t, right], axis=1)

    return place(cos_l, 1.0), place(sin_l, 0.0)


def _prep_params(w_in, na_qn_w, na_kn_w, ssd_conv_w, ssd_conv_b, ssd_dt_bias, ssd_a_log, ssd_d, ssd_norm_w,
                 mla_cq_norm_w, mla_ckv_norm_w, mla_w_uq, mla_w_ukv, mla_qn_w, mla_kn_w):
    depth = w_in.shape[0]
    d = w_in.shape[1]
    dtw = w_in[..., 2304:2320]
    w_in_p = jnp.concatenate(
        [w_in[..., :2304], w_in[..., 2320:2576], w_in[..., 2576:2704], w_in[..., 2704:2736]]
        + [dtw] * DT_COPIES + [jnp.zeros((depth, d, IN_WIDTH_PAD - 2720 - 16 * DT_COPIES), F32)], axis=-1).astype(BF16)
    qnw = jnp.tile(na_qn_w, (1, NA_HEADS))[:, None, :]
    knw = jnp.tile(na_kn_w, (1, NA_HEADS))[:, None, :]
    gi = np.arange(NA_WIDTH) // NA_HEAD_DIM
    grp = jnp.asarray((gi[:, None] == gi[None, :]).astype(np.float32), BF16)

    convw = jnp.pad(ssd_conv_w, ((0, 0), (0, 8 - SSD_CONV), (0, 0)))
    convb = ssd_conv_b[:, None, :]

    def dt_row(t):
        flat = jnp.tile(t.reshape(depth, 16), (1, DT_COPIES))
        return jnp.pad(flat, ((0, 0), (DT_LANE0, LANES - DT_LANE0 - 16 * DT_COPIES)))[:, None, :]

    bias_row = dt_row(ssd_dt_bias)
    alog_row = dt_row(ssd_a_log)
    dskip = jnp.repeat(ssd_d, SSD_HEAD_DIM, axis=-1)[:, None, :]
    normw = ssd_norm_w[:, None, :]
    ssd_params = (convw, convb, bias_row, alog_row, dskip, normw)

    wuq = mla_w_uq.reshape(depth, MLA_Q_RANK, MLA_HEADS, MLA_QK)
    wuq_p = jnp.pad(wuq, ((0, 0), (0, 0), (0, 0), (0, MLA_HEAD_PAD - MLA_QK))).reshape(depth, MLA_Q_RANK, -1).astype(BF16)
    wukv = mla_w_ukv.reshape(depth, MLA_KV_RANK, MLA_HEADS, MLA_NOPE + MLA_V)
    wk = jnp.pad(wukv[..., :MLA_NOPE], ((0, 0), (0, 0), (0, 0), (0, MLA_HEAD_PAD - MLA_NOPE))).reshape(depth, MLA_KV_RANK, -1)
    wuvt = jnp.swapaxes(wukv[..., MLA_NOPE:].reshape(depth, MLA_KV_RANK, -1), 1, 2).astype(BF16)
    pad_head = lambda t: jnp.tile(jnp.pad(t, ((0, 0), (0, MLA_HEAD_PAD - MLA_QK))), (1, MLA_HEADS))[:, None, :]
    li = np.arange(MLA_HEADS * MLA_HEAD_PAD)
    g96 = jnp.asarray((li[:, None] // MLA_HEAD_PAD == li[None, :] // MLA_HEAD_PAD).astype(np.float32), BF16)
    in_rope = (li % MLA_HEAD_PAD >= MLA_NOPE) & (li % MLA_HEAD_PAD < MLA_QK)
    swp = jnp.asarray(((li[:, None] == (li[None, :] ^ 8)) & in_rope[None, :]).astype(np.float32), BF16)
    mla_weights = (mla_cq_norm_w[:, None, :], mla_ckv_norm_w[:, None, :], wuq_p, wk.astype(BF16), wuvt,
                   pad_head(mla_qn_w), pad_head(mla_kn_w), g96, swp)
    return w_in_p, qnw, knw, grp, ssd_params, mla_weights


def kernel(x, c, ctx, c_ctx, w_ada, b_ada, norm1_w, norm2_w, w_in, w_out, na_qn_w, na_kn_w, na_rpb, ssd_conv_w, ssd_conv_b, ssd_dt_bias, ssd_a_log, ssd_d, ssd_norm_w, mla_cq_norm_w, mla_ckv_norm_w, mla_w_uq, mla_w_ukv, mla_qn_w, mla_kn_w, w_ff1, w_ff2):
    b, s, d = x.shape
    ctx_len = ctx.shape[1]
    depth = w_in.shape[0]
    n_rows = s // GRID_W
    assert d == D_MODEL and s % (8 * GRID_W) == 0 and n_rows > NA_WIN_ROWS and ctx_len % SSD_CHUNK == 0

    mod_rows = ((b + 1 + 7) // 8) * 8
    cc = jnp.concatenate([c, c_ctx[None, :], jnp.zeros((mod_rows - b - 1, d), F32)], axis=0)
    mod = _adaln(cc, w_ada, b_ada).reshape(depth, mod_rows, 6, d)

    w_in_p, qnw, knw, grp, ssd_params, mla_weights = _prep_params(
        w_in, na_qn_w, na_kn_w, ssd_conv_w, ssd_conv_b, ssd_dt_bias, ssd_a_log, ssd_d, ssd_norm_w,
        mla_cq_norm_w, mla_ckv_norm_w, mla_w_uq, mla_w_ukv, mla_qn_w, mla_kn_w)
    w_out_b = w_out.astype(BF16)
    w1_b = w_ff1.astype(BF16)
    w2_b = w_ff2.astype(BF16)
    bias_all = _na_bias(na_rpb, n_rows)
    rope_tabs = _rope_tables(s)

    lat_row = lambda bi: bi
    ctx_row = lambda bi: b
    tm = 512
    tm_c = min(tm, b * ctx_len)
    ctx_flat = ctx.reshape(1, b * ctx_len, d)

    for i in range(depth):
        need_ctx = i < depth - 1
        qkv_l, z_l, xbc_l, dt_l, qm_l, km_l, vt_l = _inproj(
            x, mod, i, lat_row, norm1_w, w_in_p, qnw, knw, grp, mla_weights, rope_tabs, tm)
        qkv_c, z_c, xbc_c, dt_c, qm_c, km_c, vt_c = _inproj(
            ctx_flat.reshape(b, ctx_len, d), mod, i, ctx_row, norm1_w, w_in_p, qnw, knw, grp, mla_weights, None, ctx_len)

        na_l = _na_attention(qkv_l, qkv_c, bias_all, i)
        ssd_l, ssd_c = _ssd(z_l, xbc_l, dt_l, z_c, xbc_c, dt_c, ssd_params, i, need_ctx)
        mla_l = _mla_attention_t(qm_l, km_l, vt_l, km_c, vt_c, 512)

        x = _outffn_resident(x, na_l, ssd_l, mla_l, mod, i, lat_row, norm2_w, w_out_b, w1_b, w2_b, tm, 1024)
        if need_ctx:
            na_c = _na_ctx_attention(qkv_c)
            mla_c = _mla_attention_t(qm_c, None, None, km_c, vt_c, ctx_len)
            flat = lambda t: t.reshape(1, b * ctx_len, t.shape[-1])
            ctx_flat = _outffn_resident(ctx_flat, flat(na_c), flat(ssd_c), flat(mla_c), mod, i, ctx_row, norm2_w,
                                        w_out_b, w1_b, w2_b, tm_c, 1024)
    return x
```

```python
import functools

import jax
import jax.numpy as jnp
import numpy as np
from jax import lax
from jax.experimental import pallas as pl
from jax.experimental.pallas import tpu as pltpu

F32 = jnp.float32
BF16 = jnp.bfloat16

D_MODEL = 1024
GRID_W = 64
D_FF = 4 * D_MODEL
NORM_EPS = 1e-6
ROPE_THETA = 10000.0
LOG2E = 1.4426950408889634
NA_HEADS = 4
NA_HEAD_DIM = 64
NA_WIDTH = 256
NA_WIN_ROWS = 8
NA_WIN_COLS = 16
SSD_HEADS = 8
SSD_HEAD_DIM = 64
SSD_WIDTH = 512
SSD_GROUPS = 2
SSD_STATE = 128
SSD_CONV = 5
SSD_CHUNK = 128
SSD_CONV_CH = 1024
MLA_HEADS = 4
MLA_NOPE = 64
MLA_ROPE = 32
MLA_V = 64
MLA_QK = 96
MLA_Q_RANK = 256
MLA_KV_RANK = 128
MLA_WIDTH = 256
MIX_WIDTH = 1024
IN_WIDTH = 2736

LANES = 128
MLA_HEAD_PAD = 128
IN_WIDTH_PAD = 2816
DT_LANE0 = 32
DT_COPIES = 3
MLA_KEY_CHUNK = 256
VMEM_LIMIT = 56 * 1024 * 1024

NT_DIMS = (((1,), (1,)), ((), ()))
TN_DIMS = (((0,), (0,)), ((), ()))


def _dot(a, b):
    return jnp.dot(a, b, preferred_element_type=F32)


def _dot_nt(a, b):
    return lax.dot_general(a, b, NT_DIMS, preferred_element_type=F32)


def _dot_tn(a, b):
    return lax.dot_general(a, b, TN_DIMS, preferred_element_type=F32)


def _sigmoid(x):
    return 1.0 / (1.0 + jnp.exp(-x))


def _silu(x):
    return x * _sigmoid(x)


def _softplus(x):
    return jnp.maximum(x, 0.0) + jnp.log1p(jnp.exp(-jnp.abs(x)))


def _split3(x):
    x0 = x.astype(BF16)
    r = x - x0.astype(F32)
    x1 = r.astype(BF16)
    r = r - x1.astype(F32)
    return x0, x1, r.astype(BF16)


def _lane_parts(x, lane):
    p0 = x.astype(BF16).astype(F32)
    r = x - p0
    p1 = r.astype(BF16).astype(F32)
    p2 = r - p1
    return jnp.where(lane < DT_LANE0 + 16, p0, jnp.where(lane < DT_LANE0 + 32, p1, p2)).astype(BF16)


def _aligned(x, m):
    return x if isinstance(x, int) else pl.multiple_of(x, m)


def _cparams(sem):
    return pltpu.CompilerParams(dimension_semantics=sem, vmem_limit_bytes=VMEM_LIMIT)


def _adaln_kernel(c_ref, w_ref, b_ref, o_ref):
    act = _silu(c_ref[...]).astype(BF16)
    o_ref[0] = _dot(act, w_ref[0].astype(BF16)) + b_ref[0]


def _adaln(cc, w_ada, b_ada):
    depth, d, n = w_ada.shape
    rows = cc.shape[0]
    tn = 1536
    return pl.pallas_call(
        _adaln_kernel,
        grid=(depth, n // tn),
        in_specs=[
            pl.BlockSpec((rows, d), lambda l, j: (0, 0)),
            pl.BlockSpec((1, d, tn), lambda l, j: (l, 0, j)),
            pl.BlockSpec((1, 1, tn), lambda l, j: (l, 0, j)),
        ],
        out_specs=pl.BlockSpec((1, rows, tn), lambda l, j: (l, 0, j)),
        out_shape=jax.ShapeDtypeStruct((depth, rows, n), F32),
        compiler_params=_cparams(("parallel", "parallel")),
        name="adaln",
    )(cc, w_ada, b_ada.reshape(depth, 1, n))


def _inproj_kernel(x_ref, mod_ref, nw_ref, w_ref, qnw_ref, knw_ref, grp_ref,
                   cqn_ref, ckvn_ref, wuq_ref, wuk_ref, wuvt_ref, mqnw_ref, mknw_ref, g96_ref, swp_ref, *rest, rope):
    if rope:
        cos_ref, sin_ref = rest[:2]
        rest = rest[2:]
    qkv_ref, z_ref, xbc_ref, dt_ref, qm_ref, km_ref, vt_ref = rest

    def rms(t, w):
        ms = jnp.mean(t * t, axis=-1, keepdims=True)
        return t * lax.rsqrt(ms + NORM_EPS) * w

    mod = mod_ref[0, 0]
    h = (rms(x_ref[0], nw_ref[0]) * (1.0 + mod[1:2, :]) + mod[0:1, :]).astype(BF16)
    u = _dot(h, w_ref[0])

    grp = grp_ref[...]

    def head_norm(t, w):
        ss = _dot((t * t).astype(BF16), grp)
        return t * lax.rsqrt(ss * (1.0 / NA_HEAD_DIM) + NORM_EPS) * w

    q = head_norm(u[:, 0:256], qnw_ref[0]) * (NA_HEAD_DIM ** -0.5 * LOG2E)
    k = head_norm(u[:, 256:512], knw_ref[0])
    qkv_ref[0, :, 0:256] = q.astype(BF16)
    qkv_ref[0, :, 256:512] = k.astype(BF16)
    qkv_ref[0, :, 512:768] = u[:, 512:768].astype(BF16)
    z_ref[0] = u[:, 768:1280].astype(BF16)
    xbc_ref[0] = u[:, 1280:2304].astype(BF16)
    tail = u[:, 2688:2816]
    dt_ref[0] = tail

    g96 = g96_ref[...]
    swp = swp_ref[...]
    nh = MLA_HEADS

    def inv_rms96(ss):
        return lax.rsqrt(ss * (1.0 / MLA_QK) + NORM_EPS)

    def rotate(t):
        if not rope:
            return t
        reps = t.shape[1] // LANES
        sw = swp[:t.shape[1], :t.shape[1]]
        partner = _dot(t.astype(BF16), sw)
        return t * jnp.concatenate([cos_ref[...]] * reps, axis=1) + partner * jnp.concatenate([sin_ref[...]] * reps, axis=1)

    qf = _dot(rms(u[:, 2304:2560], cqn_ref[0]).astype(BF16), wuq_ref[0])
    ckvn = rms(u[:, 2560:2688], ckvn_ref[0]).astype(BF16)
    kn = _dot(ckvn, wuk_ref[0])
    vt_ref[0] = _dot_nt(wuvt_ref[0], ckvn).astype(BF16)
    lane = lax.broadcasted_iota(jnp.int32, (1, LANES), 1)
    kr = jnp.where((lane >= MLA_NOPE) & (lane < MLA_QK), pltpu.roll(tail, MLA_NOPE, axis=1), 0.0)
    mqnw = mqnw_ref[0]
    mknw = mknw_ref[0]

    q_ss = _dot((qf * qf).astype(BF16), g96)
    qn = rotate(qf * inv_rms96(q_ss) * mqnw)
    qm_ref[0] = (qn * (MLA_QK ** -0.5 * LOG2E)).astype(BF16)
    kr_ss = jnp.sum(kr * kr, axis=-1, keepdims=True)
    kr_rot = rotate(kr * mknw[:, :LANES])
    k_ss = _dot((kn * kn).astype(BF16), g96) + kr_ss
    km_ref[0] = ((kn * mknw + jnp.concatenate([kr_rot] * nh, axis=1)) * inv_rms96(k_ss)).astype(BF16)


def _inproj(xs, mod, layer, mod_row_fn, norm_w, w_in_p, qnw, knw, grp, mla_weights, rope_tabs, tm):
    b, s, d = xs.shape
    depth = w_in_p.shape[0]
    cqn_w, ckvn_w, wuq_p, wuk_p, wuvt_p, mqnw_p, mknw_p, g96, swp = mla_weights
    rope = rope_tabs is not None
    hw = MLA_HEADS * MLA_HEAD_PAD
    outs = (
        jax.ShapeDtypeStruct((b, s, 768), BF16),
        jax.ShapeDtypeStruct((b, s, SSD_WIDTH), BF16),
        jax.ShapeDtypeStruct((b, s, SSD_CONV_CH), BF16),
        jax.ShapeDtypeStruct((b, s, LANES), F32),
        jax.ShapeDtypeStruct((b, s, hw), BF16),
        jax.ShapeDtypeStruct((b, s, hw), BF16),
        jax.ShapeDtypeStruct((b, MLA_WIDTH, s), BF16),
    )
    tile = lambda w: pl.BlockSpec((1, tm, w), lambda bi, ti: (bi, ti, 0))
    tile_t = lambda r: pl.BlockSpec((1, r, tm), lambda bi, ti: (bi, 0, ti))

    def par(shape):
        return pl.BlockSpec((1,) + shape, lambda bi, ti: (layer,) + (0,) * len(shape))

    in_specs = [
        tile(d),
        pl.BlockSpec((1, 1, 6, d), lambda bi, ti: (layer, mod_row_fn(bi), 0, 0)),
        par((1, d)), par((d, IN_WIDTH_PAD)), par((1, 256)), par((1, 256)),
        pl.BlockSpec((256, 256), lambda bi, ti: (0, 0)),
        par((1, MLA_Q_RANK)), par((1, MLA_KV_RANK)), par((MLA_Q_RANK, hw)), par((MLA_KV_RANK, hw)),
        par((MLA_WIDTH, MLA_KV_RANK)),
        par((1, hw)), par((1, hw)),
        pl.BlockSpec((hw, hw), lambda bi, ti: (0, 0)), pl.BlockSpec((hw, hw), lambda bi, ti: (0, 0)),
    ]
    args = [xs, mod, norm_w.reshape(depth, 1, d), w_in_p, qnw, knw, grp, cqn_w, ckvn_w, wuq_p, wuk_p, wuvt_p, mqnw_p,
            mknw_p, g96, swp]
    if rope:
        in_specs += [pl.BlockSpec((tm, LANES), lambda bi, ti: (ti, 0))] * 2
        args += list(rope_tabs)
    return pl.pallas_call(
        functools.partial(_inproj_kernel, rope=rope),
        grid=(b, s // tm),
        in_specs=in_specs,
        out_specs=[tile(768), tile(SSD_WIDTH), tile(SSD_CONV_CH), tile(LANES), tile(hw), tile(hw), tile_t(MLA_WIDTH)],
        out_shape=outs,
        compiler_params=_cparams(("parallel", "parallel")),
        name="inproj",
    )(*args)


NA_VARIANT_ROWS = (0, 1, 2, 3, 4, -3, -2, -1)


def _na_bias_kernel(rpb_ref, o_ref, t_ref, *, n_rows):
    nrow_off = 2 * NA_WIN_ROWS - 1
    ncol_off = 2 * NA_WIN_COLS - 1
    base = (pl.program_id(0) * NA_HEADS + pl.program_id(1)) * (nrow_off * ncol_off)
    qi = lax.broadcasted_iota(jnp.int32, (GRID_W, LANES), 0)
    li = lax.broadcasted_iota(jnp.int32, (GRID_W, LANES), 1)
    kcol = li & (GRID_W - 1)
    colidx = jnp.clip(kcol - qi, -(NA_WIN_COLS - 1), NA_WIN_COLS - 1) + (NA_WIN_COLS - 1)
    cstart = jnp.clip(qi - NA_WIN_COLS // 2, 0, GRID_W - NA_WIN_COLS)
    valid = (kcol >= cstart) & (kcol < cstart + NA_WIN_COLS)
    for d in range(nrow_off):
        acc = jnp.zeros((GRID_W, LANES), F32)
        for j in range(ncol_off):
            acc = jnp.where(colidx == j, rpb_ref[base + d * ncol_off + j], acc)
        t_ref[d] = jnp.where(valid, acc * LOG2E, -jnp.inf)
    for v, r_rep in enumerate(NA_VARIANT_ROWS):
        r = r_rep if r_rep >= 0 else n_rows + r_rep
        rs = min(max(r - NA_WIN_ROWS // 2, 0), n_rows - NA_WIN_ROWS)
        for p in range(NA_WIN_ROWS // 2):
            d0 = rs + 2 * p - r + (NA_WIN_ROWS - 1)
            o_ref[0, v, :, p * LANES:(p + 1) * LANES] = jnp.where(li < GRID_W, t_ref[d0], t_ref[d0 + 1])


def _na_bias(rpb, n_rows):
    depth, heads = rpb.shape[0], rpb.shape[1]
    n_win = NA_WIN_ROWS * GRID_W
    return pl.pallas_call(
        functools.partial(_na_bias_kernel, n_rows=n_rows),
        grid=(depth, heads),
        in_specs=[pl.BlockSpec(memory_space=pltpu.SMEM)],
        out_specs=pl.BlockSpec((1, 8, GRID_W, n_win), lambda l, h: (l, 0, h, 0)),
        out_shape=jax.ShapeDtypeStruct((depth, 8, heads * GRID_W, n_win), F32),
        scratch_shapes=[pltpu.VMEM((2 * NA_WIN_ROWS - 1, GRID_W, LANES), F32)],
        compiler_params=_cparams(("parallel", "parallel")),
        name="na_bias",
    )(rpb.reshape(-1))


def _head_masks():
    lane = lax.broadcasted_iota(jnp.int32, (1, NA_WIDTH), 1)
    return [jnp.where((lane >= h * NA_HEAD_DIM) & (lane < (h + 1) * NA_HEAD_DIM), 1.0, 0.0) for h in range(NA_HEADS)]


def _stacked_scores(q, keys, biases, hms):
    qs = jnp.concatenate([q * hm.astype(BF16) for hm in hms], axis=0)
    scores = []
    for k, b in zip(keys, biases):
        s = _dot_nt(qs, k)
        scores.append(s if b is None else s + b)
    return scores


def _stacked_attend(scores, values, hms):
    m_rows = scores[0].shape[0] // NA_HEADS
    m = scores[0].max(axis=-1, keepdims=True)
    for s in scores[1:]:
        m = jnp.maximum(m, s.max(axis=-1, keepdims=True))
    l = None
    o = None
    for s, v in zip(scores, values):
        p = jnp.exp2(s - m)
        ls = p.sum(axis=-1, keepdims=True)
        os_ = _dot(p.astype(BF16), v)
        l = ls if l is None else l + ls
        o = os_ if o is None else o + os_
    o = o * (1.0 / l)
    out = o[0:m_rows] * hms[0]
    for h in range(1, NA_HEADS):
        out = out + o[h * m_rows:(h + 1) * m_rows] * hms[h]
    return out


def _na_kernel(q_ref, k_ref, v_ref, kc_ref, vc_ref, bias_ref, o_ref, *, rows_per_step, n_rows):
    blk = pl.program_id(1)
    kc = kc_ref[0]
    vc = vc_ref[0]
    n_win = NA_WIN_ROWS * GRID_W
    hms = _head_masks()

    def window_start(i):
        r = blk * rows_per_step + i
        rs = jnp.clip(r - NA_WIN_ROWS // 2, 0, n_rows - NA_WIN_ROWS)
        var = jnp.where(r < NA_WIN_ROWS // 2, r,
                        jnp.where(r > n_rows - NA_WIN_ROWS // 2, r - (n_rows - NA_WIN_ROWS), NA_WIN_ROWS // 2))
        return pl.multiple_of(rs * GRID_W, GRID_W), var

    def row_scores(i):
        start, var = window_start(i)
        q = q_ref[0, i * GRID_W:(i + 1) * GRID_W, :]
        return _stacked_scores(q, [k_ref[0, pl.ds(start, n_win), :], kc], [bias_ref[0, var], None], hms)

    nxt = row_scores(0)
    for i in range(rows_per_step):
        cur = nxt
        if i + 1 < rows_per_step:
            nxt = row_scores(i + 1)
        start, _ = window_start(i)
        out = _stacked_attend(cur, [v_ref[0, pl.ds(start, n_win), :], vc], hms)
        o_ref[0, i * GRID_W:(i + 1) * GRID_W, :] = out.astype(BF16)


def _na_attention(qkv_l, qkv_c, bias_all, layer, rows_per_step=16):
    b, s, _ = qkv_l.shape
    ctx_len = qkv_c.shape[1]
    n_rows = s // GRID_W
    tq = rows_per_step * GRID_W
    return pl.pallas_call(
        functools.partial(_na_kernel, rows_per_step=rows_per_step, n_rows=n_rows),
        grid=(b, n_rows // rows_per_step),
        in_specs=[
            pl.BlockSpec((1, tq, 256), lambda bi, ti: (bi, ti, 0)),
            pl.BlockSpec((1, s, 256), lambda bi, ti: (bi, 0, 1)),
            pl.BlockSpec((1, s, 256), lambda bi, ti: (bi, 0, 2)),
            pl.BlockSpec((1, ctx_len, 256), lambda bi, ti: (bi, 0, 1)),
            pl.BlockSpec((1, ctx_len, 256), lambda bi, ti: (bi, 0, 2)),
            pl.BlockSpec((1,) + bias_all.shape[1:], lambda bi, ti: (layer, 0, 0, 0)),
        ],
        out_specs=pl.BlockSpec((1, tq, 256), lambda bi, ti: (bi, ti, 0)),
        out_shape=jax.ShapeDtypeStruct((b, s, NA_WIDTH), BF16),
        compiler_params=_cparams(("parallel", "parallel")),
        name="na_attn",
    )(qkv_l, qkv_l, qkv_l, qkv_c, qkv_c, bias_all)


def _na_ctx_kernel(q_ref, k_ref, v_ref, o_ref):
    hms = _head_masks()
    scores = _stacked_scores(q_ref[0], [k_ref[0]], [None], hms)
    o_ref[0] = _stacked_attend(scores, [v_ref[0]], hms).astype(BF16)


def _na_ctx_attention(qkv_c):
    b, c, _ = qkv_c.shape
    spec = lambda j: pl.BlockSpec((1, c, 256), lambda bi: (bi, 0, j))
    return pl.pallas_call(
        _na_ctx_kernel,
        grid=(b,),
        in_specs=[spec(0), spec(1), spec(2)],
        out_specs=spec(0),
        out_shape=jax.ShapeDtypeStruct((b, c, NA_WIDTH), BF16),
        compiler_params=_cparams(("parallel",)),
        name="na_ctx_attn",
    )(qkv_c, qkv_c, qkv_c)


def _ssd_kernel(z_l, xbc_l, dt_l, z_c, xbc_c, dt_c,
                convw_ref, convb_ref, bias_row_ref, alog_row_ref, dskip_ref, normw_ref, out_l, out_c,
                xs_s, bc_s, yf_s, yb_s, stf_s, stb_s, dt_s, acf_s, acb_s, *, n_lat, n_ctx, need_ctx):
    q = SSD_CHUNK
    lat_row0 = n_ctx * q

    convw = convw_ref[0]
    convb = convb_ref[0]
    win = q + 32
    wi = lax.broadcasted_iota(jnp.int32, (q, win), 0)
    wr = lax.broadcasted_iota(jnp.int32, (q, win), 1)
    mid = SSD_CONV // 2

    def shift_mats(off):
        return [None if j == mid else jnp.where(wr == wi + (off + j - mid), 1.0, 0.0).astype(BF16)
                for j in range(SSD_CONV)]

    def conv_chunk(xref, lo, off, smats, dst):
        w_in = xref[0, pl.ds(lo, win), :]
        acc = convb + w_in[off:off + q].astype(F32) * convw[mid:mid + 1, :]
        for j in range(SSD_CONV):
            if j != mid:
                acc = acc + _dot(smats[j], w_in) * convw[j:j + 1, :]
        act = _silu(acc)
        xs_s[pl.ds(dst, q), :] = act[:, :SSD_WIDTH]
        bc_s[pl.ds(dst, q), :] = act[:, SSD_WIDTH:].astype(BF16)

    def conv_seq(xref, n, row0):
        conv_chunk(xref, 0, 0, shift_mats(0), row0)
        conv_chunk(xref, (n - 1) * q - 32, 32, shift_mats(32), row0 + (n - 1) * q)
        if n > 2:
            smats = shift_mats(16)

            def body(c, carry):
                conv_chunk(xref, pl.multiple_of(c * q - 16, 16), 16, smats, pl.multiple_of(row0 + c * q, q))
                return carry
            lax.fori_loop(1, n - 1, body, 0, unroll=2)

    conv_seq(xbc_c, n_ctx, 0)
    conv_seq(xbc_l, n_lat, lat_row0)

    lane = lax.broadcasted_iota(jnp.int32, (1, LANES), 1)
    dmask = (lane >= DT_LANE0) & (lane < DT_LANE0 + 16 * DT_COPIES)
    bias_row = bias_row_ref[0]
    a_row = jnp.where(dmask, -jnp.exp(alog_row_ref[0]), 0.0)
    ri = lax.broadcasted_iota(jnp.int32, (q, q), 0)
    ci = lax.broadcasted_iota(jnp.int32, (q, q), 1)
    lo_mask = ci <= ri
    up_mask = ri <= ci
    tri_lo = jnp.where(lo_mask, 1.0, 0.0).astype(BF16)
    tri_up = jnp.where(up_mask, 1.0, 0.0).astype(BF16)
    sel_r = lax.broadcasted_iota(jnp.int32, (LANES, SSD_HEADS * LANES), 0)
    sel_c = lax.broadcasted_iota(jnp.int32, (LANES, SSD_HEADS * LANES), 1)
    sel64_r = lax.broadcasted_iota(jnp.int32, (LANES, SSD_WIDTH), 0)
    sel64_c = lax.broadcasted_iota(jnp.int32, (LANES, SSD_WIDTH), 1)
    lane_lt64 = lax.broadcasted_iota(jnp.int32, (1, LANES), 1) < SSD_HEAD_DIM

    def sel_matrices(direction):
        in3 = (sel_r >= DT_LANE0) & (sel_r < DT_LANE0 + 48)
        sel128 = jnp.where(in3 & (((sel_r - DT_LANE0) & 15) == direction * SSD_HEADS + (sel_c >> 7)), 1.0, 0.0)
        in2 = (sel64_r >= DT_LANE0) & (sel64_r < DT_LANE0 + 32)
        sel64 = jnp.where(in2 & (((sel64_r - DT_LANE0) & 15) == direction * SSD_HEADS + (sel64_c >> 6)), 1.0, 0.0)
        return sel128.astype(BF16), sel64.astype(BF16)

    sels = (sel_matrices(0), sel_matrices(1))
    tri3_both = jnp.concatenate([jnp.concatenate([tri_lo] * 3, axis=1), jnp.concatenate([tri_up] * 3, axis=1)], axis=0)
    masks = (lo_mask, up_mask)

    y_refs = (yf_s, yb_s)
    st_refs = (stf_s, stb_s)
    stf_s[...] = jnp.zeros_like(stf_s)
    stb_s[...] = jnp.zeros_like(stb_s)

    ac_refs = (acf_s, acb_s)

    def decay_chunk(dt_ref, c, row0):
        rows = pl.ds(_aligned(row0 + c * q, q), q)
        raw = dt_ref[0, pl.ds(_aligned(c * q, q), q), :]
        dt = jnp.where(dmask, _softplus(raw + bias_row), 0.0)
        a0, a1, a2 = _split3(dt * a_row)
        both = _dot(tri3_both, jnp.concatenate([a0, a1, a2], axis=0))
        dt_s[rows, :] = dt
        acf_s[rows, :] = both[0:q]
        acb_s[rows, :] = both[q:2 * q]

    for i in range(n_ctx):
        decay_chunk(dt_c, i, 0)

    def decay_body(c, carry):
        decay_chunk(dt_l, c, lat_row0)
        return carry

    lax.fori_loop(0, n_lat, decay_body, 0, unroll=2)

    def chunk(c, row0, direction, compute_y):
        rows = pl.ds(_aligned(row0 + c * q, q), q)
        x = xs_s[rows, :]
        bc = bc_s[rows, :]
        sel128, sel64 = sels[direction]
        dt = dt_s[rows, :]
        acum = ac_refs[direction][rows, :]
        a_exp = _dot(_lane_parts(acum, lane), sel128)
        dt64 = _dot(_lane_parts(dt, lane), sel64)
        st = st_refs[direction][...]
        grp_w = SSD_WIDTH // SSD_GROUPS
        b_g = [bc[:, g * SSD_STATE:(g + 1) * SSD_STATE] for g in range(SSD_GROUPS)]
        c_g = [bc[:, (SSD_GROUPS + g) * SSD_STATE:(SSD_GROUPS + g + 1) * SSD_STATE] for g in range(SSD_GROUPS)]
        if compute_y:
            stb = st.astype(BF16)
            gmats = [_dot_nt(c_g[g], b_g[g]) for g in range(SSD_GROUPS)]
            offs = [_dot(c_g[g], stb[:, g * grp_w:(g + 1) * grp_w]) for g in range(SSD_GROUPS)]
        yield

        ac64 = jnp.concatenate(
            [jnp.where(lane_lt64, a_exp[:, (2 * j) * LANES:(2 * j + 1) * LANES],
                       a_exp[:, (2 * j + 1) * LANES:(2 * j + 2) * LANES]) for j in range(SSD_HEADS // 2)], axis=1)
        acum_t = acum.T
        end = q - 1 if direction == 0 else 0
        ac_end = ac64[end:end + 1, :]
        xd = x * dt64
        xdb = xd.astype(BF16)
        xdw = (xd * jnp.exp(ac_end - ac64)).astype(BF16)
        yield

        if compute_y:
            parts = []
            for pair in range(SSD_HEADS // 2):
                h0 = 2 * pair
                gmat = gmats[h0 // (SSD_HEADS // SSD_GROUPS)]
                xp = xdb[:, h0 * SSD_HEAD_DIM:(h0 + 2) * SSD_HEAD_DIM]
                res = []
                for hh in (h0, h0 + 1):
                    row = DT_LANE0 + direction * SSD_HEADS + hh
                    dmat = a_exp[:, hh * LANES:(hh + 1) * LANES] - acum_t[row:row + 1, :]
                    lmat = jnp.exp(jnp.where(masks[direction], dmat, -jnp.inf))
                    res.append(_dot((gmat * lmat).astype(BF16), xp))
                parts.append(jnp.where(lane_lt64, res[0], res[1]))
                yield
            y = jnp.concatenate(parts, axis=1) + jnp.concatenate(offs, axis=1) * jnp.exp(ac64)
            y_refs[direction][rows, :] = y

        new = [_dot_tn(b_g[g], xdw[:, g * grp_w:(g + 1) * grp_w]) for g in range(SSD_GROUPS)]
        st_refs[direction][...] = st * jnp.exp(ac_end) + jnp.concatenate(new, axis=1)
        yield

    def run_interleaved(*gens):
        live = list(gens)
        while live:
            for g in list(live):
                try:
                    next(g)
                except StopIteration:
                    live.remove(g)

    for i in range(n_ctx):
        run_interleaved(chunk(i, 0, 0, need_ctx), chunk(n_ctx - 1 - i, 0, 1, need_ctx))

    def lat_body(i, carry):
        run_interleaved(chunk(i, lat_row0, 0, True), chunk(n_lat - 1 - i, lat_row0, 1, True))
        return carry

    lax.fori_loop(0, n_lat, lat_body, 0, unroll=2)

    dskip = dskip_ref[0]
    normw = normw_ref[0]

    def finish(z_ref, o_ref, c, row0):
        rows = pl.ds(_aligned(row0 + c * q, q), q)
        crow = pl.ds(_aligned(c * q, q), q)
        y = yf_s[rows, :] + yb_s[rows, :] + dskip * xs_s[rows, :]
        y = y * _silu(z_ref[0, crow, :].astype(F32))
        ms = jnp.mean(y * y, axis=-1, keepdims=True)
        o_ref[0, crow, :] = (y * lax.rsqrt(ms + NORM_EPS) * normw).astype(BF16)

    if need_ctx:
        for i in range(n_ctx):
            finish(z_c, out_c, i, 0)
    else:
        out_c[...] = jnp.zeros_like(out_c)

    def fin_body(c, carry):
        finish(z_l, out_l, c, lat_row0)
        return carry

    lax.fori_loop(0, n_lat, fin_body, 0, unroll=2)


def _ssd(z_l, xbc_l, dt_l, z_c, xbc_c, dt_c, params, layer, need_ctx):
    b, s, _ = z_l.shape
    c = z_c.shape[1]
    n_lat, n_ctx = s // SSD_CHUNK, c // SSD_CHUNK
    t = s + c
    convw, convb, bias_row, alog_row, dskip, normw = params

    def seq_specs(n):
        return [
            pl.BlockSpec((1, n, SSD_WIDTH), lambda bi: (bi, 0, 0)),
            pl.BlockSpec((1, n, SSD_CONV_CH), lambda bi: (bi, 0, 0)),
            pl.BlockSpec((1, n, LANES), lambda bi: (bi, 0, 0)),
        ]

    def par(shape):
        return pl.BlockSpec((1,) + shape, lambda bi: (layer,) + (0,) * len(shape))

    return pl.pallas_call(
        functools.partial(_ssd_kernel, n_lat=n_lat, n_ctx=n_ctx, need_ctx=need_ctx),
        grid=(b,),
        in_specs=seq_specs(s) + seq_specs(c) + [
            par((8, SSD_CONV_CH)), par((1, SSD_CONV_CH)), par((1, LANES)), par((1, LANES)),
            par((1, SSD_WIDTH)), par((1, SSD_WIDTH)),
        ],
        out_specs=[
            pl.BlockSpec((1, s, SSD_WIDTH), lambda bi: (bi, 0, 0)),
            pl.BlockSpec((1, c, SSD_WIDTH), lambda bi: (bi, 0, 0)),
        ],
        out_shape=(jax.ShapeDtypeStruct((b, s, SSD_WIDTH), BF16), jax.ShapeDtypeStruct((b, c, SSD_WIDTH), BF16)),
        scratch_shapes=[
            pltpu.VMEM((t, SSD_WIDTH), F32),
            pltpu.VMEM((t, SSD_WIDTH), BF16),
            pltpu.VMEM((t, SSD_WIDTH), F32),
            pltpu.VMEM((t, SSD_WIDTH), F32),
            pltpu.VMEM((SSD_STATE, SSD_WIDTH), F32),
            pltpu.VMEM((SSD_STATE, SSD_WIDTH), F32),
            pltpu.VMEM((t, LANES), F32),
            pltpu.VMEM((t, LANES), F32),
            pltpu.VMEM((t, LANES), F32),
        ],
        compiler_params=_cparams(("parallel",)),
        name="ssd",
    )(z_l, xbc_l, dt_l, z_c, xbc_c, dt_c, convw, convb, bias_row, alog_row, dskip, normw)


def _mla_t_kernel(*refs, with_latent):
    if with_latent:
        q_ref, kl_ref, vtl_ref, kc_ref, vtc_ref, o_ref, s_scr = refs
        n_lat = kl_ref.shape[1]
    else:
        q_ref, kc_ref, vtc_ref, o_ref, s_scr = refs
        kl_ref = vtl_ref = None
        n_lat = 0
    n_ctx = kc_ref.shape[1]
    kc_ = min(MLA_KEY_CHUNK, n_ctx)
    chunks = [(kl_ref, vtl_ref, j * kc_, j * kc_) for j in range(n_lat // kc_)]
    chunks += [(kc_ref, vtc_ref, j * kc_, n_lat + j * kc_) for j in range(n_ctx // kc_)]
    key_blocks = ([(kl_ref, 0)] if with_latent else []) + [(kc_ref, n_lat)]

    def scores(h):
        sl = slice(h * MLA_HEAD_PAD, (h + 1) * MLA_HEAD_PAD)
        qh = q_ref[0, :, sl]
        m = None
        for k_ref, dst in key_blocks:
            n = k_ref.shape[1]
            s = _dot_nt(k_ref[0, :, sl], qh)
            s_scr[h % 2, dst:dst + n, :] = s
            mc = s.max(axis=0, keepdims=True)
            m = mc if m is None else jnp.maximum(m, mc)
        return m

    def attend(h, m):
        vrows = slice(h * MLA_V, (h + 1) * MLA_V)
        l = None
        o = None
        for _, vt_ref, src, dst in chunks:
            p = jnp.exp2(s_scr[h % 2, dst:dst + kc_, :] - m)
            lc = p.sum(axis=0, keepdims=True)
            oc = _dot(vt_ref[0, vrows, src:src + kc_], p.astype(BF16))
            l = lc if l is None else l + lc
            o = oc if o is None else o + oc
        return o * (1.0 / l)

    outs = []
    m_next = scores(0)
    for h in range(MLA_HEADS):
        m_cur = m_next
        if h + 1 < MLA_HEADS:
            m_next = scores(h + 1)
        outs.append(attend(h, m_cur))
    o_ref[0] = jnp.concatenate(outs, axis=0).T.astype(BF16)


def _mla_attention_t(q, k_l, vt_l, k_c, vt_c, tq):
    b, sq, wq = q.shape
    c = k_c.shape[1]
    with_latent = k_l is not None
    s = k_l.shape[1] if with_latent else 0
    in_specs = [pl.BlockSpec((1, tq, wq), lambda bi, ti: (bi, ti, 0))]
    args = [q]
    if with_latent:
        in_specs += [pl.BlockSpec((1, s, wq), lambda bi, ti: (bi, 0, 0)),
                     pl.BlockSpec((1, MLA_WIDTH, s), lambda bi, ti: (bi, 0, 0))]
        args += [k_l, vt_l]
    in_specs += [pl.BlockSpec((1, c, wq), lambda bi, ti: (bi, 0, 0)),
                 pl.BlockSpec((1, MLA_WIDTH, c), lambda bi, ti: (bi, 0, 0))]
    args += [k_c, vt_c]
    return pl.pallas_call(
        functools.partial(_mla_t_kernel, with_latent=with_latent),
        grid=(b, sq // tq),
        in_specs=in_specs,
        out_specs=pl.BlockSpec((1, tq, MLA_WIDTH), lambda bi, ti: (bi, ti, 0)),
        out_shape=jax.ShapeDtypeStruct((b, sq, MLA_WIDTH), BF16),
        scratch_shapes=[pltpu.VMEM((2, s + c, tq), F32)],
        compiler_params=_cparams(("parallel", "parallel")),
        name="mla_attn" if with_latent else "mla_ctx_attn",
    )(*args)


def _outffn_resident_kernel(x_ref, na_ref, ssd_ref, mla_ref, mod_ref, nw_ref, wo_ref, w1_ref, w2_ref, o_ref, *, tf):
    mod = mod_ref[0, 0]
    mix = (_dot(na_ref[0], wo_ref[0, 0:NA_WIDTH, :])
           + _dot(ssd_ref[0], wo_ref[0, NA_WIDTH:NA_WIDTH + SSD_WIDTH, :])
           + _dot(mla_ref[0], wo_ref[0, NA_WIDTH + SSD_WIDTH:, :]))
    x1 = x_ref[0] + mod[2:3, :] * mix
    ms = jnp.mean(x1 * x1, axis=-1, keepdims=True)
    xn = x1 * lax.rsqrt(ms + NORM_EPS) * nw_ref[0]
    xn = (xn * (1.0 + mod[4:5, :]) + mod[3:4, :]).astype(BF16)
    acc = None
    for f in range(D_FF // tf):
        hmid = _dot(xn, w1_ref[0, :, f * tf:(f + 1) * tf])
        hmid = jnp.square(jnp.maximum(hmid, 0.0)).astype(BF16)
        part = _dot(hmid, w2_ref[0, f * tf:(f + 1) * tf, :])
        acc = part if acc is None else acc + part
    o_ref[0] = x1 + mod[5:6, :] * acc


def _outffn_resident(xs, na, ssd, mla, mod, layer, mod_row_fn, norm_w, w_out, w1, w2, tm, tf):
    b, s, d = xs.shape
    depth = w_out.shape[0]
    tile = lambda w: pl.BlockSpec((1, tm, w), lambda bi, ti: (bi, ti, 0))
    once = lambda shape: pl.BlockSpec((1,) + shape, lambda bi, ti: (layer, 0, 0), pipeline_mode=pl.Buffered(1))
    return pl.pallas_call(
        functools.partial(_outffn_resident_kernel, tf=tf),
        grid=(b, s // tm),
        in_specs=[
            tile(d), tile(NA_WIDTH), tile(SSD_WIDTH), tile(MLA_WIDTH),
            pl.BlockSpec((1, 1, 6, d), lambda bi, ti: (layer, mod_row_fn(bi), 0, 0)),
            pl.BlockSpec((1, 1, d), lambda bi, ti: (layer, 0, 0)),
            once((MIX_WIDTH, d)), once((d, D_FF)), once((D_FF, d)),
        ],
        out_specs=tile(d),
        out_shape=jax.ShapeDtypeStruct((b, s, d), F32),
        compiler_params=_cparams(("parallel", "parallel")),
        name="outproj_mlp_res",
    )(xs, na, ssd, mla, mod, norm_w.reshape(depth, 1, d), w_out, w1, w2)


def _rope_tables(n_tokens):
    pos = jnp.arange(n_tokens)
    axes = jnp.stack([pos // GRID_W, pos % GRID_W], axis=-1).astype(F32)
    n_freq = MLA_ROPE // 4
    inv_freq = ROPE_THETA ** (-jnp.arange(n_freq, dtype=F32) / n_freq)
    ang = axes[:, :, None] * inv_freq
    cos, sin = jnp.cos(ang), jnp.sin(ang)
    cos_l = jnp.stack([cos, cos], axis=2).reshape(n_tokens, MLA_ROPE)
    sin_l = jnp.stack([-sin, sin], axis=2).reshape(n_tokens, MLA_ROPE)

    def place(t, fill):
        left = jnp.full((n_tokens, MLA_NOPE), fill, F32)
        right = jnp.full((n_tokens, LANES - MLA_QK), fill, F32)
        return jnp.concatenate([left, t, right], axis=1)

    return place(cos_l, 1.0), place(sin_l, 0.0)


def _prep_params(w_in, na_qn_w, na_kn_w, ssd_conv_w, ssd_conv_b, ssd_dt_bias, ssd_a_log, ssd_d, ssd_norm_w,
                 mla_cq_norm_w, mla_ckv_norm_w, mla_w_uq, mla_w_ukv, mla_qn_w, mla_kn_w):
    depth = w_in.shape[0]
    d = w_in.shape[1]
    dtw = w_in[..., 2304:2320]
    w_in_p = jnp.concatenate(
        [w_in[..., :2304], w_in[..., 2320:2576], w_in[..., 2576:2704], w_in[..., 2704:2736]]
        + [dtw] * DT_COPIES + [jnp.zeros((depth, d, IN_WIDTH_PAD - 2720 - 16 * DT_COPIES), F32)], axis=-1).astype(BF16)
    qnw = jnp.tile(na_qn_w, (1, NA_HEADS))[:, None, :]
    knw = jnp.tile(na_kn_w, (1, NA_HEADS))[:, None, :]
    gi = np.arange(NA_WIDTH) // NA_HEAD_DIM
    grp = jnp.asarray((gi[:, None] == gi[None, :]).astype(np.float32), BF16)

    convw = jnp.pad(ssd_conv_w, ((0, 0), (0, 8 - SSD_CONV), (0, 0)))
    convb = ssd_conv_b[:, None, :]

    def dt_row(t):
        flat = jnp.tile(t.reshape(depth, 16), (1, DT_COPIES))
        return jnp.pad(flat, ((0, 0), (DT_LANE0, LANES - DT_LANE0 - 16 * DT_COPIES)))[:, None, :]

    bias_row = dt_row(ssd_dt_bias)
    alog_row = dt_row(ssd_a_log)
    dskip = jnp.repeat(ssd_d, SSD_HEAD_DIM, axis=-1)[:, None, :]
    normw = ssd_norm_w[:, None, :]
    ssd_params = (convw, convb, bias_row, alog_row, dskip, normw)

    wuq = mla_w_uq.reshape(depth, MLA_Q_RANK, MLA_HEADS, MLA_QK)
    wuq_p = jnp.pad(wuq, ((0, 0), (0, 0), (0, 0), (0, MLA_HEAD_PAD - MLA_QK))).reshape(depth, MLA_Q_RANK, -1).astype(BF16)
    wukv = mla_w_ukv.reshape(depth, MLA_KV_RANK, MLA_HEADS, MLA_NOPE + MLA_V)
    wk = jnp.pad(wukv[..., :MLA_NOPE], ((0, 0), (0, 0), (0, 0), (0, MLA_HEAD_PAD - MLA_NOPE))).reshape(depth, MLA_KV_RANK, -1)
    wuvt = jnp.swapaxes(wukv[..., MLA_NOPE:].reshape(depth, MLA_KV_RANK, -1), 1, 2).astype(BF16)
    pad_head = lambda t: jnp.tile(jnp.pad(t, ((0, 0), (0, MLA_HEAD_PAD - MLA_QK))), (1, MLA_HEADS))[:, None, :]
    li = np.arange(MLA_HEADS * MLA_HEAD_PAD)
    g96 = jnp.asarray((li[:, None] // MLA_HEAD_PAD == li[None, :] // MLA_HEAD_PAD).astype(np.float32), BF16)
    in_rope = (li % MLA_HEAD_PAD >= MLA_NOPE) & (li % MLA_HEAD_PAD < MLA_QK)
    swp = jnp.asarray(((li[:, None] == (li[None, :] ^ 8)) & in_rope[None, :]).astype(np.float32), BF16)
    mla_weights = (mla_cq_norm_w[:, None, :], mla_ckv_norm_w[:, None, :], wuq_p, wk.astype(BF16), wuvt,
                   pad_head(mla_qn_w), pad_head(mla_kn_w), g96, swp)
    return w_in_p, qnw, knw, grp, ssd_params, mla_weights


def kernel(x, c, ctx, c_ctx, w_ada, b_ada, norm1_w, norm2_w, w_in, w_out, na_qn_w, na_kn_w, na_rpb, ssd_conv_w, ssd_conv_b, ssd_dt_bias, ssd_a_log, ssd_d, ssd_norm_w, mla_cq_norm_w, mla_ckv_norm_w, mla_w_uq, mla_w_ukv, mla_qn_w, mla_kn_w, w_ff1, w_ff2):
    b, s, d = x.shape
    ctx_len = ctx.shape[1]
    depth = w_in.shape[0]
    n_rows = s // GRID_W
    assert d == D_MODEL and s % (16 * GRID_W) == 0 and n_rows > NA_WIN_ROWS and ctx_len % SSD_CHUNK == 0

    mod_rows = ((b + 1 + 7) // 8) * 8
    cc = jnp.concatenate([c, c_ctx[None, :], jnp.zeros((mod_rows - b - 1, d), F32)], axis=0)
    mod = _adaln(cc, w_ada, b_ada).reshape(depth, mod_rows, 6, d)

    w_in_p, qnw, knw, grp, ssd_params, mla_weights = _prep_params(
        w_in, na_qn_w, na_kn_w, ssd_conv_w, ssd_conv_b, ssd_dt_bias, ssd_a_log, ssd_d, ssd_norm_w,
        mla_cq_norm_w, mla_ckv_norm_w, mla_w_uq, mla_w_ukv, mla_qn_w, mla_kn_w)
    w_out_b = w_out.astype(BF16)
    w1_b = w_ff1.astype(BF16)
    w2_b = w_ff2.astype(BF16)
    bias_all = _na_bias(na_rpb, n_rows)
    rope_tabs = _rope_tables(s)

    lat_row = lambda bi: bi
    ctx_row = lambda bi: b
    tm = 512
    tm_c = min(tm, b * ctx_len)
    ctx_flat = ctx.reshape(1, b * ctx_len, d)

    for i in range(depth):
        need_ctx = i < depth - 1
        qkv_l, z_l, xbc_l, dt_l, qm_l, km_l, vt_l = _inproj(
            x, mod, i, lat_row, norm1_w, w_in_p, qnw, knw, grp, mla_weights, rope_tabs, tm)
        qkv_c, z_c, xbc_c, dt_c, qm_c, km_c, vt_c = _inproj(
            ctx_flat.reshape(b, ctx_len, d), mod, i, ctx_row, norm1_w, w_in_p, qnw, knw, grp, mla_weights, None, ctx_len)

        na_l = _na_attention(qkv_l, qkv_c, bias_all, i)
        ssd_l, ssd_c = _ssd(z_l, xbc_l, dt_l, z_c, xbc_c, dt_c, ssd_params, i, need_ctx)
        mla_l = _mla_attention_t(qm_l, km_l, vt_l, km_c, vt_c, 512)

        x = _outffn_resident(x, na_l, ssd_l, mla_l, mod, i, lat_row, norm2_w, w_out_b, w1_b, w2_b, tm, 1024)
        if need_ctx:
            na_c = _na_ctx_attention(qkv_c)
            mla_c = _mla_attention_t(qm_c, None, None, km_c, vt_c, ctx_len)
            flat = lambda t: t.reshape(1, b * ctx_len, t.shape[-1])
            ctx_flat = _outffn_resident(ctx_flat, flat(na_c), flat(ssd_c), flat(mla_c), mod, i, ctx_row, norm2_w,
                                        w_out_b, w1_b, w2_b, tm_c, 1024)
    return x
```

```python
import functools

import jax
import jax.numpy as jnp
import numpy as np
from jax import lax
from jax.experimental import pallas as pl
from jax.experimental.pallas import tpu as pltpu

F32 = jnp.float32
BF16 = jnp.bfloat16

D_MODEL = 1024
GRID_W = 64
D_FF = 4 * D_MODEL
NORM_EPS = 1e-6
ROPE_THETA = 10000.0
LOG2E = 1.4426950408889634
NA_HEADS = 4
NA_HEAD_DIM = 64
NA_WIDTH = 256
NA_WIN_ROWS = 8
NA_WIN_COLS = 16
SSD_HEADS = 8
SSD_HEAD_DIM = 64
SSD_WIDTH = 512
SSD_GROUPS = 2
SSD_STATE = 128
SSD_CONV = 5
SSD_CHUNK = 128
SSD_CONV_CH = 1024
MLA_HEADS = 4
MLA_NOPE = 64
MLA_ROPE = 32
MLA_V = 64
MLA_QK = 96
MLA_Q_RANK = 256
MLA_KV_RANK = 128
MLA_WIDTH = 256
MIX_WIDTH = 1024
IN_WIDTH = 2736

LANES = 128
MLA_HEAD_PAD = 128
IN_WIDTH_PAD = 2816
DT_LANE0 = 32
DT_COPIES = 3
MLA_KEY_CHUNK = 256
VMEM_LIMIT = 56 * 1024 * 1024

NT_DIMS = (((1,), (1,)), ((), ()))
TN_DIMS = (((0,), (0,)), ((), ()))


def _dot(a, b):
    return jnp.dot(a, b, preferred_element_type=F32)


def _dot_nt(a, b):
    return lax.dot_general(a, b, NT_DIMS, preferred_element_type=F32)


def _dot_tn(a, b):
    return lax.dot_general(a, b, TN_DIMS, preferred_element_type=F32)


def _sigmoid(x):
    return 1.0 / (1.0 + jnp.exp(-x))


def _silu(x):
    return x * _sigmoid(x)


def _softplus(x):
    return jnp.maximum(x, 0.0) + jnp.log1p(jnp.exp(-jnp.abs(x)))


def _split3(x):
    x0 = x.astype(BF16)
    r = x - x0.astype(F32)
    x1 = r.astype(BF16)
    r = r - x1.astype(F32)
    return x0, x1, r.astype(BF16)


def _lane_parts(x, lane):
    p0 = x.astype(BF16).astype(F32)
    r = x - p0
    p1 = r.astype(BF16).astype(F32)
    p2 = r - p1
    return jnp.where(lane < DT_LANE0 + 16, p0, jnp.where(lane < DT_LANE0 + 32, p1, p2)).astype(BF16)


def _aligned(x, m):
    return x if isinstance(x, int) else pl.multiple_of(x, m)


def _cparams(sem):
    return pltpu.CompilerParams(dimension_semantics=sem, vmem_limit_bytes=VMEM_LIMIT)


def _adaln_kernel(c_ref, w_ref, b_ref, o_ref):
    act = _silu(c_ref[...]).astype(BF16)
    o_ref[0] = _dot(act, w_ref[0].astype(BF16)) + b_ref[0]


def _adaln(cc, w_ada, b_ada):
    depth, d, n = w_ada.shape
    rows = cc.shape[0]
    tn = 1536
    return pl.pallas_call(
        _adaln_kernel,
        grid=(depth, n // tn),
        in_specs=[
            pl.BlockSpec((rows, d), lambda l, j: (0, 0)),
            pl.BlockSpec((1, d, tn), lambda l, j: (l, 0, j)),
            pl.BlockSpec((1, 1, tn), lambda l, j: (l, 0, j)),
        ],
        out_specs=pl.BlockSpec((1, rows, tn), lambda l, j: (l, 0, j)),
        out_shape=jax.ShapeDtypeStruct((depth, rows, n), F32),
        compiler_params=_cparams(("parallel", "parallel")),
        name="adaln",
    )(cc, w_ada, b_ada.reshape(depth, 1, n))


def _inproj_kernel(x_ref, mod_ref, nw_ref, w_ref, qnw_ref, knw_ref, grp_ref,
                   cqn_ref, ckvn_ref, wuq_ref, wuk_ref, wuvt_ref, mqnw_ref, mknw_ref, g96_ref, swp_ref, *rest, rope):
    if rope:
        cos_ref, sin_ref = rest[:2]
        rest = rest[2:]
    qkv_ref, z_ref, xbc_ref, dt_ref, qm_ref, km_ref, vt_ref = rest

    def rms(t, w):
        ms = jnp.mean(t * t, axis=-1, keepdims=True)
        return t * lax.rsqrt(ms + NORM_EPS) * w

    mod = mod_ref[0, 0]
    h = (rms(x_ref[0], nw_ref[0]) * (1.0 + mod[1:2, :]) + mod[0:1, :]).astype(BF16)
    u = _dot(h, w_ref[0])

    grp = grp_ref[...]

    def head_norm(t, w):
        ss = _dot((t * t).astype(BF16), grp)
        return t * lax.rsqrt(ss * (1.0 / NA_HEAD_DIM) + NORM_EPS) * w

    q = head_norm(u[:, 0:256], qnw_ref[0]) * (NA_HEAD_DIM ** -0.5 * LOG2E)
    k = head_norm(u[:, 256:512], knw_ref[0])
    qkv_ref[0, :, 0:256] = q.astype(BF16)
    qkv_ref[0, :, 256:512] = k.astype(BF16)
    qkv_ref[0, :, 512:768] = u[:, 512:768].astype(BF16)
    z_ref[0] = u[:, 768:1280].astype(BF16)
    xbc_ref[0] = u[:, 1280:2304].astype(BF16)
    tail = u[:, 2688:2816]
    dt_ref[0] = tail

    g96 = g96_ref[...]
    swp = swp_ref[...]
    nh = MLA_HEADS

    def inv_rms96(ss):
        return lax.rsqrt(ss * (1.0 / MLA_QK) + NORM_EPS)

    def rotate(t):
        if not rope:
            return t
        reps = t.shape[1] // LANES
        sw = swp[:t.shape[1], :t.shape[1]]
        partner = _dot(t.astype(BF16), sw)
        return t * jnp.concatenate([cos_ref[...]] * reps, axis=1) + partner * jnp.concatenate([sin_ref[...]] * reps, axis=1)

    qf = _dot(rms(u[:, 2304:2560], cqn_ref[0]).astype(BF16), wuq_ref[0])
    ckvn = rms(u[:, 2560:2688], ckvn_ref[0]).astype(BF16)
    kn = _dot(ckvn, wuk_ref[0])
    vt_ref[0] = _dot_nt(wuvt_ref[0], ckvn).astype(BF16)
    lane = lax.broadcasted_iota(jnp.int32, (1, LANES), 1)
    kr = jnp.where((lane >= MLA_NOPE) & (lane < MLA_QK), pltpu.roll(tail, MLA_NOPE, axis=1), 0.0)
    mqnw = mqnw_ref[0]
    mknw = mknw_ref[0]

    q_ss = _dot((qf * qf).astype(BF16), g96)
    qn = rotate(qf * inv_rms96(q_ss) * mqnw)
    qm_ref[0] = (qn * (MLA_QK ** -0.5 * LOG2E)).astype(BF16)
    kr_ss = jnp.sum(kr * kr, axis=-1, keepdims=True)
    kr_rot = rotate(kr * mknw[:, :LANES])
    k_ss = _dot((kn * kn).astype(BF16), g96) + kr_ss
    km_ref[0] = ((kn * mknw + jnp.concatenate([kr_rot] * nh, axis=1)) * inv_rms96(k_ss)).astype(BF16)


def _inproj(xs, mod, layer, mod_row_fn, norm_w, w_in_p, qnw, knw, grp, mla_weights, rope_tabs, tm):
    b, s, d = xs.shape
    depth = w_in_p.shape[0]
    cqn_w, ckvn_w, wuq_p, wuk_p, wuvt_p, mqnw_p, mknw_p, g96, swp = mla_weights
    rope = rope_tabs is not None
    hw = MLA_HEADS * MLA_HEAD_PAD
    outs = (
        jax.ShapeDtypeStruct((b, s, 768), BF16),
        jax.ShapeDtypeStruct((b, s, SSD_WIDTH), BF16),
        jax.ShapeDtypeStruct((b, s, SSD_CONV_CH), BF16),
        jax.ShapeDtypeStruct((b, s, LANES), F32),
        jax.ShapeDtypeStruct((b, s, hw), BF16),
        jax.ShapeDtypeStruct((b, s, hw), BF16),
        jax.ShapeDtypeStruct((b, MLA_WIDTH, s), BF16),
    )
    tile = lambda w: pl.BlockSpec((1, tm, w), lambda bi, ti: (bi, ti, 0))
    tile_t = lambda r: pl.BlockSpec((1, r, tm), lambda bi, ti: (bi, 0, ti))

    def par(shape):
        return pl.BlockSpec((1,) + shape, lambda bi, ti: (layer,) + (0,) * len(shape))

    in_specs = [
        tile(d),
        pl.BlockSpec((1, 1, 6, d), lambda bi, ti: (layer, mod_row_fn(bi), 0, 0)),
        par((1, d)), par((d, IN_WIDTH_PAD)), par((1, 256)), par((1, 256)),
        pl.BlockSpec((256, 256), lambda bi, ti: (0, 0)),
        par((1, MLA_Q_RANK)), par((1, MLA_KV_RANK)), par((MLA_Q_RANK, hw)), par((MLA_KV_RANK, hw)),
        par((MLA_WIDTH, MLA_KV_RANK)),
        par((1, hw)), par((1, hw)),
        pl.BlockSpec((hw, hw), lambda bi, ti: (0, 0)), pl.BlockSpec((hw, hw), lambda bi, ti: (0, 0)),
    ]
    args = [xs, mod, norm_w.reshape(depth, 1, d), w_in_p, qnw, knw, grp, cqn_w, ckvn_w, wuq_p, wuk_p, wuvt_p, mqnw_p,
            mknw_p, g96, swp]
    if rope:
        in_specs += [pl.BlockSpec((tm, LANES), lambda bi, ti: (ti, 0))] * 2
        args += list(rope_tabs)
    return pl.pallas_call(
        functools.partial(_inproj_kernel, rope=rope),
        grid=(b, s // tm),
        in_specs=in_specs,
        out_specs=[tile(768), tile(SSD_WIDTH), tile(SSD_CONV_CH), tile(LANES), tile(hw), tile(hw), tile_t(MLA_WIDTH)],
        out_shape=outs,
        compiler_params=_cparams(("parallel", "parallel")),
        name="inproj",
    )(*args)


NA_VARIANT_ROWS = (0, 1, 2, 3, 4, -3, -2, -1)


def _na_bias_kernel(rpb_ref, o_ref, t_ref, *, n_rows):
    nrow_off = 2 * NA_WIN_ROWS - 1
    ncol_off = 2 * NA_WIN_COLS - 1
    base = (pl.program_id(0) * NA_HEADS + pl.program_id(1)) * (nrow_off * ncol_off)
    qi = lax.broadcasted_iota(jnp.int32, (GRID_W, LANES), 0)
    li = lax.broadcasted_iota(jnp.int32, (GRID_W, LANES), 1)
    kcol = li & (GRID_W - 1)
    colidx = jnp.clip(kcol - qi, -(NA_WIN_COLS - 1), NA_WIN_COLS - 1) + (NA_WIN_COLS - 1)
    cstart = jnp.clip(qi - NA_WIN_COLS // 2, 0, GRID_W - NA_WIN_COLS)
    valid = (kcol >= cstart) & (kcol < cstart + NA_WIN_COLS)
    for d in range(nrow_off):
        acc = jnp.zeros((GRID_W, LANES), F32)
        for j in range(ncol_off):
            acc = jnp.where(colidx == j, rpb_ref[base + d * ncol_off + j], acc)
        t_ref[d] = jnp.where(valid, acc * LOG2E, -jnp.inf)
    for v, r_rep in enumerate(NA_VARIANT_ROWS):
        r = r_rep if r_rep >= 0 else n_rows + r_rep
        rs = min(max(r - NA_WIN_ROWS // 2, 0), n_rows - NA_WIN_ROWS)
        for p in range(NA_WIN_ROWS // 2):
            d0 = rs + 2 * p - r + (NA_WIN_ROWS - 1)
            o_ref[0, v, :, p * LANES:(p + 1) * LANES] = jnp.where(li < GRID_W, t_ref[d0], t_ref[d0 + 1])


def _na_bias(rpb, n_rows):
    depth, heads = rpb.shape[0], rpb.shape[1]
    n_win = NA_WIN_ROWS * GRID_W
    return pl.pallas_call(
        functools.partial(_na_bias_kernel, n_rows=n_rows),
        grid=(depth, heads),
        in_specs=[pl.BlockSpec(memory_space=pltpu.SMEM)],
        out_specs=pl.BlockSpec((1, 8, GRID_W, n_win), lambda l, h: (l, 0, h, 0)),
        out_shape=jax.ShapeDtypeStruct((depth, 8, heads * GRID_W, n_win), F32),
        scratch_shapes=[pltpu.VMEM((2 * NA_WIN_ROWS - 1, GRID_W, LANES), F32)],
        compiler_params=_cparams(("parallel", "parallel")),
        name="na_bias",
    )(rpb.reshape(-1))


def _head_masks():
    lane = lax.broadcasted_iota(jnp.int32, (1, NA_WIDTH), 1)
    return [jnp.where((lane >= h * NA_HEAD_DIM) & (lane < (h + 1) * NA_HEAD_DIM), 1.0, 0.0) for h in range(NA_HEADS)]


def _stacked_scores(q, keys, biases, hms):
    qs = jnp.concatenate([q * hm.astype(BF16) for hm in hms], axis=0)
    scores = []
    for k, b in zip(keys, biases):
        s = _dot_nt(qs, k)
        scores.append(s if b is None else s + b)
    return scores


def _stacked_attend(scores, values, hms):
    m_rows = scores[0].shape[0] // NA_HEADS
    m = scores[0].max(axis=-1, keepdims=True)
    for s in scores[1:]:
        m = jnp.maximum(m, s.max(axis=-1, keepdims=True))
    l = None
    o = None
    for s, v in zip(scores, values):
        p = jnp.exp2(s - m)
        ls = p.sum(axis=-1, keepdims=True)
        os_ = _dot(p.astype(BF16), v)
        l = ls if l is None else l + ls
        o = os_ if o is None else o + os_
    o = o * (1.0 / l)
    out = o[0:m_rows] * hms[0]
    for h in range(1, NA_HEADS):
        out = out + o[h * m_rows:(h + 1) * m_rows] * hms[h]
    return out


def _na_kernel(q_ref, k_ref, v_ref, kc_ref, vc_ref, bias_ref, o_ref, *, rows_per_step, n_rows):
    blk = pl.program_id(1)
    kc = kc_ref[0]
    vc = vc_ref[0]
    n_win = NA_WIN_ROWS * GRID_W
    hms = _head_masks()

    def window_start(i):
        r = blk * rows_per_step + i
        rs = jnp.clip(r - NA_WIN_ROWS // 2, 0, n_rows - NA_WIN_ROWS)
        var = jnp.where(r < NA_WIN_ROWS // 2, r,
                        jnp.where(r > n_rows - NA_WIN_ROWS // 2, r - (n_rows - NA_WIN_ROWS), NA_WIN_ROWS // 2))
        return pl.multiple_of(rs * GRID_W, GRID_W), var

    def row_scores(i):
        start, var = window_start(i)
        q = q_ref[0, i * GRID_W:(i + 1) * GRID_W, :]
        return _stacked_scores(q, [k_ref[0, pl.ds(start, n_win), :], kc], [bias_ref[0, var], None], hms)

    nxt = row_scores(0)
    for i in range(rows_per_step):
        cur = nxt
        if i + 1 < rows_per_step:
            nxt = row_scores(i + 1)
        start, _ = window_start(i)
        out = _stacked_attend(cur, [v_ref[0, pl.ds(start, n_win), :], vc], hms)
        o_ref[0, i * GRID_W:(i + 1) * GRID_W, :] = out.astype(BF16)


def _na_attention(qkv_l, qkv_c, bias_all, layer, rows_per_step=16):
    b, s, _ = qkv_l.shape
    ctx_len = qkv_c.shape[1]
    n_rows = s // GRID_W
    tq = rows_per_step * GRID_W
    return pl.pallas_call(
        functools.partial(_na_kernel, rows_per_step=rows_per_step, n_rows=n_rows),
        grid=(b, n_rows // rows_per_step),
        in_specs=[
            pl.BlockSpec((1, tq, 256), lambda bi, ti: (bi, ti, 0)),
            pl.BlockSpec((1, s, 256), lambda bi, ti: (bi, 0, 1)),
            pl.BlockSpec((1, s, 256), lambda bi, ti: (bi, 0, 2)),
            pl.BlockSpec((1, ctx_len, 256), lambda bi, ti: (bi, 0, 1)),
            pl.BlockSpec((1, ctx_len, 256), lambda bi, ti: (bi, 0, 2)),
            pl.BlockSpec((1,) + bias_all.shape[1:], lambda bi, ti: (layer, 0, 0, 0)),
        ],
        out_specs=pl.BlockSpec((1, tq, 256), lambda bi, ti: (bi, ti, 0)),
        out_shape=jax.ShapeDtypeStruct((b, s, NA_WIDTH), BF16),
        compiler_params=_cparams(("parallel", "parallel")),
        name="na_attn",
    )(qkv_l, qkv_l, qkv_l, qkv_c, qkv_c, bias_all)


def _na_ctx_kernel(q_ref, k_ref, v_ref, o_ref):
    hms = _head_masks()
    scores = _stacked_scores(q_ref[0], [k_ref[0]], [None], hms)
    o_ref[0] = _stacked_attend(scores, [v_ref[0]], hms).astype(BF16)


def _na_ctx_attention(qkv_c):
    b, c, _ = qkv_c.shape
    spec = lambda j: pl.BlockSpec((1, c, 256), lambda bi: (bi, 0, j))
    return pl.pallas_call(
        _na_ctx_kernel,
        grid=(b,),
        in_specs=[spec(0), spec(1), spec(2)],
        out_specs=spec(0),
        out_shape=jax.ShapeDtypeStruct((b, c, NA_WIDTH), BF16),
        compiler_params=_cparams(("parallel",)),
        name="na_ctx_attn",
    )(qkv_c, qkv_c, qkv_c)


def _ssd_kernel(z_l, xbc_l, dt_l, z_c, xbc_c, dt_c,
                convw_ref, convb_ref, bias_row_ref, alog_row_ref, dskip_ref, normw_ref, out_l, out_c,
                xs_s, bc_s, yf_s, yb_s, stf_s, stb_s, dt_s, acf_s, acb_s, g_s, *, n_lat, n_ctx, need_ctx):
    q = SSD_CHUNK
    lat_row0 = n_ctx * q

    convw = convw_ref[0]
    convb = convb_ref[0]
    win = q + 32
    wi = lax.broadcasted_iota(jnp.int32, (q, win), 0)
    wr = lax.broadcasted_iota(jnp.int32, (q, win), 1)
    mid = SSD_CONV // 2

    side_taps = [j for j in range(SSD_CONV) if j != mid]

    def shift_mats(off):
        return jnp.concatenate([jnp.where(wr == wi + (off + j - mid), 1.0, 0.0).astype(BF16) for j in side_taps], axis=0)

    def conv_chunk(xref, lo, off, smats, dst):
        w_in = xref[0, pl.ds(lo, win), :]
        acc = convb + w_in[off:off + q].astype(F32) * convw[mid:mid + 1, :]
        shifted = _dot(smats, w_in)
        for n, j in enumerate(side_taps):
            acc = acc + shifted[n * q:(n + 1) * q] * convw[j:j + 1, :]
        act = _silu(acc)
        xs_s[pl.ds(dst, q), :] = act[:, :SSD_WIDTH]
        bc_s[pl.ds(dst, q), :] = act[:, SSD_WIDTH:].astype(BF16)

    def conv_seq(xref, n, row0):
        conv_chunk(xref, 0, 0, shift_mats(0), row0)
        conv_chunk(xref, (n - 1) * q - 32, 32, shift_mats(32), row0 + (n - 1) * q)
        if n > 2:
            smats = shift_mats(16)

            def body(c, carry):
                conv_chunk(xref, pl.multiple_of(c * q - 16, 16), 16, smats, pl.multiple_of(row0 + c * q, q))
                return carry
            lax.fori_loop(1, n - 1, body, 0, unroll=2)

    conv_seq(xbc_c, n_ctx, 0)
    conv_seq(xbc_l, n_lat, lat_row0)

    lane = lax.broadcasted_iota(jnp.int32, (1, LANES), 1)
    dmask = (lane >= DT_LANE0) & (lane < DT_LANE0 + 16 * DT_COPIES)
    bias_row = bias_row_ref[0]
    a_row = jnp.where(dmask, -jnp.exp(alog_row_ref[0]) * LOG2E, 0.0)
    ri = lax.broadcasted_iota(jnp.int32, (q, q), 0)
    ci = lax.broadcasted_iota(jnp.int32, (q, q), 1)
    lo_mask = ci <= ri
    up_mask = ri <= ci
    tri_lo = jnp.where(lo_mask, 1.0, 0.0).astype(BF16)
    tri_up = jnp.where(up_mask, 1.0, 0.0).astype(BF16)
    sel_r = lax.broadcasted_iota(jnp.int32, (LANES, SSD_HEADS * LANES), 0)
    sel_c = lax.broadcasted_iota(jnp.int32, (LANES, SSD_HEADS * LANES), 1)
    sel64_r = lax.broadcasted_iota(jnp.int32, (LANES, SSD_WIDTH), 0)
    sel64_c = lax.broadcasted_iota(jnp.int32, (LANES, SSD_WIDTH), 1)
    lane_lt64 = lax.broadcasted_iota(jnp.int32, (1, LANES), 1) < SSD_HEAD_DIM
    pair_lo = jnp.where(lane_lt64, 1.0, 0.0).astype(BF16)
    pair_hi = jnp.where(lane_lt64, 0.0, 1.0).astype(BF16)

    def sel_matrices(direction):
        in3 = (sel_r >= DT_LANE0) & (sel_r < DT_LANE0 + 48)
        sel128 = jnp.where(in3 & (((sel_r - DT_LANE0) & 15) == direction * SSD_HEADS + (sel_c >> 7)), 1.0, 0.0)
        in2 = (sel64_r >= DT_LANE0) & (sel64_r < DT_LANE0 + 32)
        sel64 = jnp.where(in2 & (((sel64_r - DT_LANE0) & 15) == direction * SSD_HEADS + (sel64_c >> 6)), 1.0, 0.0)
        return sel128.astype(BF16), sel64.astype(BF16)

    sels = (sel_matrices(0), sel_matrices(1))
    tri3_both = jnp.concatenate([jnp.concatenate([tri_lo] * 3, axis=1), jnp.concatenate([tri_up] * 3, axis=1)], axis=0)
    masks = (lo_mask, up_mask)

    y_refs = (yf_s, yb_s)
    st_refs = (stf_s, stb_s)
    stf_s[...] = jnp.zeros_like(stf_s)
    stb_s[...] = jnp.zeros_like(stb_s)

    ac_refs = (acf_s, acb_s)

    def decay_chunk(dt_ref, c, row0):
        rows = pl.ds(_aligned(row0 + c * q, q), q)
        raw = dt_ref[0, pl.ds(_aligned(c * q, q), q), :]
        dt = jnp.where(dmask, _softplus(raw + bias_row), 0.0)
        a0, a1, a2 = _split3(dt * a_row)
        both = _dot(tri3_both, jnp.concatenate([a0, a1, a2], axis=0))
        dt_s[rows, :] = dt
        acf_s[rows, :] = both[0:q]
        acb_s[rows, :] = both[q:2 * q]
        bc = bc_s[rows, :]
        for g in range(SSD_GROUPS):
            g_s[rows, g * q:(g + 1) * q] = _dot_nt(bc[:, (SSD_GROUPS + g) * SSD_STATE:(SSD_GROUPS + g + 1) * SSD_STATE],
                                                   bc[:, g * SSD_STATE:(g + 1) * SSD_STATE])

    for i in range(n_ctx):
        decay_chunk(dt_c, i, 0)

    def decay_body(c, carry):
        decay_chunk(dt_l, c, lat_row0)
        return carry

    lax.fori_loop(0, n_lat, decay_body, 0, unroll=2)

    def chunk(c, row0, direction, compute_y):
        rows = pl.ds(_aligned(row0 + c * q, q), q)
        x = xs_s[rows, :]
        bc = bc_s[rows, :]
        sel128, sel64 = sels[direction]
        dt = dt_s[rows, :]
        acum = ac_refs[direction][rows, :]
        a_exp = _dot(_lane_parts(acum, lane), sel128)
        dt64 = _dot(_lane_parts(dt, lane), sel64)
        st = st_refs[direction][...]
        grp_w = SSD_WIDTH // SSD_GROUPS
        b_g = [bc[:, g * SSD_STATE:(g + 1) * SSD_STATE] for g in range(SSD_GROUPS)]
        c_g = [bc[:, (SSD_GROUPS + g) * SSD_STATE:(SSD_GROUPS + g + 1) * SSD_STATE] for g in range(SSD_GROUPS)]
        if compute_y:
            stb = st.astype(BF16)
            gmats = [g_s[rows, g * q:(g + 1) * q] for g in range(SSD_GROUPS)]
            offs = [_dot(c_g[g], stb[:, g * grp_w:(g + 1) * grp_w]) for g in range(SSD_GROUPS)]
        yield

        ac64 = jnp.concatenate(
            [jnp.where(lane_lt64, a_exp[:, (2 * j) * LANES:(2 * j + 1) * LANES],
                       a_exp[:, (2 * j + 1) * LANES:(2 * j + 2) * LANES]) for j in range(SSD_HEADS // 2)], axis=1)
        acum_t = acum.T
        end = q - 1 if direction == 0 else 0
        ac_end = ac64[end:end + 1, :]
        xd = x * dt64
        xdb = xd.astype(BF16)
        xdw = (xd * jnp.exp2(ac_end - ac64)).astype(BF16)
        yield

        if compute_y:
            parts = []
            for pair in range(SSD_HEADS // 2):
                h0 = 2 * pair
                gmat = gmats[h0 // (SSD_HEADS // SSD_GROUPS)]
                xp = xdb[:, h0 * SSD_HEAD_DIM:(h0 + 2) * SSD_HEAD_DIM]
                mats = []
                for hh in (h0, h0 + 1):
                    row = DT_LANE0 + direction * SSD_HEADS + hh
                    dmat = a_exp[:, hh * LANES:(hh + 1) * LANES] - acum_t[row:row + 1, :]
                    lmat = jnp.exp2(jnp.where(masks[direction], dmat, -jnp.inf))
                    mats.append((gmat * lmat).astype(BF16))
                parts.append(_dot(jnp.concatenate(mats, axis=1),
                                  jnp.concatenate([xp * pair_lo, xp * pair_hi], axis=0)))
                yield
            y = jnp.concatenate(parts, axis=1) + jnp.concatenate(offs, axis=1) * jnp.exp2(ac64)
            y_refs[direction][rows, :] = y

        new = [_dot_tn(b_g[g], xdw[:, g * grp_w:(g + 1) * grp_w]) for g in range(SSD_GROUPS)]
        st_refs[direction][...] = st * jnp.exp2(ac_end) + jnp.concatenate(new, axis=1)
        yield

    def run_interleaved(*gens):
        live = list(gens)
        while live:
            for g in list(live):
                try:
                    next(g)
                except StopIteration:
                    live.remove(g)

    for i in range(n_ctx):
        run_interleaved(chunk(i, 0, 0, need_ctx), chunk(n_ctx - 1 - i, 0, 1, need_ctx))

    def lat_body(i, carry):
        run_interleaved(chunk(i, lat_row0, 0, True), chunk(n_lat - 1 - i, lat_row0, 1, True))
        return carry

    lax.fori_loop(0, n_lat, lat_body, 0, unroll=2)

    dskip = dskip_ref[0]
    normw = normw_ref[0]

    def finish(z_ref, o_ref, c, row0):
        rows = pl.ds(_aligned(row0 + c * q, q), q)
        crow = pl.ds(_aligned(c * q, q), q)
        y = yf_s[rows, :] + yb_s[rows, :] + dskip * xs_s[rows, :]
        y = y * _silu(z_ref[0, crow, :].astype(F32))
        ms = jnp.mean(y * y, axis=-1, keepdims=True)
        o_ref[0, crow, :] = (y * lax.rsqrt(ms + NORM_EPS) * normw).astype(BF16)

    if need_ctx:
        for i in range(n_ctx):
            finish(z_c, out_c, i, 0)
    else:
        out_c[...] = jnp.zeros_like(out_c)

    def fin_body(c, carry):
        finish(z_l, out_l, c, lat_row0)
        return carry

    lax.fori_loop(0, n_lat, fin_body, 0, unroll=2)


def _ssd(z_l, xbc_l, dt_l, z_c, xbc_c, dt_c, params, layer, need_ctx):
    b, s, _ = z_l.shape
    c = z_c.shape[1]
    n_lat, n_ctx = s // SSD_CHUNK, c // SSD_CHUNK
    t = s + c
    convw, convb, bias_row, alog_row, dskip, normw = params

    def seq_specs(n):
        return [
            pl.BlockSpec((1, n, SSD_WIDTH), lambda bi: (bi, 0, 0)),
            pl.BlockSpec((1, n, SSD_CONV_CH), lambda bi: (bi, 0, 0)),
            pl.BlockSpec((1, n, LANES), lambda bi: (bi, 0, 0)),
        ]

    def par(shape):
        return pl.BlockSpec((1,) + shape, lambda bi: (layer,) + (0,) * len(shape))

    return pl.pallas_call(
        functools.partial(_ssd_kernel, n_lat=n_lat, n_ctx=n_ctx, need_ctx=need_ctx),
        grid=(b,),
        in_specs=seq_specs(s) + seq_specs(c) + [
            par((8, SSD_CONV_CH)), par((1, SSD_CONV_CH)), par((1, LANES)), par((1, LANES)),
            par((1, SSD_WIDTH)), par((1, SSD_WIDTH)),
        ],
        out_specs=[
            pl.BlockSpec((1, s, SSD_WIDTH), lambda bi: (bi, 0, 0)),
            pl.BlockSpec((1, c, SSD_WIDTH), lambda bi: (bi, 0, 0)),
        ],
        out_shape=(jax.ShapeDtypeStruct((b, s, SSD_WIDTH), BF16), jax.ShapeDtypeStruct((b, c, SSD_WIDTH), BF16)),
        scratch_shapes=[
            pltpu.VMEM((t, SSD_WIDTH), F32),
            pltpu.VMEM((t, SSD_WIDTH), BF16),
            pltpu.VMEM((t, SSD_WIDTH), F32),
            pltpu.VMEM((t, SSD_WIDTH), F32),
            pltpu.VMEM((SSD_STATE, SSD_WIDTH), F32),
            pltpu.VMEM((SSD_STATE, SSD_WIDTH), F32),
            pltpu.VMEM((t, LANES), F32),
            pltpu.VMEM((t, LANES), F32),
            pltpu.VMEM((t, LANES), F32),
            pltpu.VMEM((t, SSD_GROUPS * SSD_CHUNK), F32),
        ],
        compiler_params=_cparams(("parallel",)),
        name="ssd",
    )(z_l, xbc_l, dt_l, z_c, xbc_c, dt_c, convw, convb, bias_row, alog_row, dskip, normw)


def _mla_t_kernel(*refs, with_latent):
    if with_latent:
        q_ref, kl_ref, vtl_ref, kc_ref, vtc_ref, o_ref, s_scr = refs
        n_lat = kl_ref.shape[1]
    else:
        q_ref, kc_ref, vtc_ref, o_ref, s_scr = refs
        kl_ref = vtl_ref = None
        n_lat = 0
    n_ctx = kc_ref.shape[1]
    kc_ = min(MLA_KEY_CHUNK, n_ctx)
    chunks = [(kl_ref, vtl_ref, j * kc_, j * kc_) for j in range(n_lat // kc_)]
    chunks += [(kc_ref, vtc_ref, j * kc_, n_lat + j * kc_) for j in range(n_ctx // kc_)]
    key_blocks = ([(kl_ref, 0)] if with_latent else []) + [(kc_ref, n_lat)]

    def scores(h):
        sl = slice(h * MLA_HEAD_PAD, (h + 1) * MLA_HEAD_PAD)
        qh = q_ref[0, :, sl]
        m = None
        for k_ref, dst in key_blocks:
            n = k_ref.shape[1]
            s = _dot_nt(k_ref[0, :, sl], qh)
            s_scr[h % 2, dst:dst + n, :] = s
            mc = s.max(axis=0, keepdims=True)
            m = mc if m is None else jnp.maximum(m, mc)
        return m

    def attend(h, m):
        vrows = slice(h * MLA_V, (h + 1) * MLA_V)
        l = None
        o = None
        for _, vt_ref, src, dst in chunks:
            p = jnp.exp2(s_scr[h % 2, dst:dst + kc_, :] - m)
            lc = p.sum(axis=0, keepdims=True)
            oc = _dot(vt_ref[0, vrows, src:src + kc_], p.astype(BF16))
            l = lc if l is None else l + lc
            o = oc if o is None else o + oc
        return o * (1.0 / l)

    outs = []
    m_next = scores(0)
    for h in range(MLA_HEADS):
        m_cur = m_next
        if h + 1 < MLA_HEADS:
            m_next = scores(h + 1)
        outs.append(attend(h, m_cur))
    o_ref[0] = jnp.concatenate(outs, axis=0).T.astype(BF16)


def _mla_attention_t(q, k_l, vt_l, k_c, vt_c, tq):
    b, sq, wq = q.shape
    c = k_c.shape[1]
    with_latent = k_l is not None
    s = k_l.shape[1] if with_latent else 0
    in_specs = [pl.BlockSpec((1, tq, wq), lambda bi, ti: (bi, ti, 0))]
    args = [q]
    if with_latent:
        in_specs += [pl.BlockSpec((1, s, wq), lambda bi, ti: (bi, 0, 0)),
                     pl.BlockSpec((1, MLA_WIDTH, s), lambda bi, ti: (bi, 0, 0))]
        args += [k_l, vt_l]
    in_specs += [pl.BlockSpec((1, c, wq), lambda bi, ti: (bi, 0, 0)),
                 pl.BlockSpec((1, MLA_WIDTH, c), lambda bi, ti: (bi, 0, 0))]
    args += [k_c, vt_c]
    return pl.pallas_call(
        functools.partial(_mla_t_kernel, with_latent=with_latent),
        grid=(b, sq // tq),
        in_specs=in_specs,
        out_specs=pl.BlockSpec((1, tq, MLA_WIDTH), lambda bi, ti: (bi, ti, 0)),
        out_shape=jax.ShapeDtypeStruct((b, sq, MLA_WIDTH), BF16),
        scratch_shapes=[pltpu.VMEM((2, s + c, tq), F32)],
        compiler_params=_cparams(("parallel", "parallel")),
        name="mla_attn" if with_latent else "mla_ctx_attn",
    )(*args)


def _outffn_resident_kernel(x_ref, na_ref, ssd_ref, mla_ref, mod_ref, nw_ref, wo_ref, w1_ref, w2_ref, o_ref, *, tf):
    mod = mod_ref[0, 0]
    mix = (_dot(na_ref[0], wo_ref[0, 0:NA_WIDTH, :])
           + _dot(ssd_ref[0], wo_ref[0, NA_WIDTH:NA_WIDTH + SSD_WIDTH, :])
           + _dot(mla_ref[0], wo_ref[0, NA_WIDTH + SSD_WIDTH:, :]))
    x1 = x_ref[0] + mod[2:3, :] * mix
    ms = jnp.mean(x1 * x1, axis=-1, keepdims=True)
    xn = x1 * lax.rsqrt(ms + NORM_EPS) * nw_ref[0]
    xn = (xn * (1.0 + mod[4:5, :]) + mod[3:4, :]).astype(BF16)
    acc = None
    for f in range(D_FF // tf):
        hmid = _dot(xn, w1_ref[0, :, f * tf:(f + 1) * tf])
        hmid = jnp.square(jnp.maximum(hmid, 0.0)).astype(BF16)
        part = _dot(hmid, w2_ref[0, f * tf:(f + 1) * tf, :])
        acc = part if acc is None else acc + part
    o_ref[0] = x1 + mod[5:6, :] * acc


def _outffn_resident(xs, na, ssd, mla, mod, layer, mod_row_fn, norm_w, w_out, w1, w2, tm, tf):
    b, s, d = xs.shape
    depth = w_out.shape[0]
    tile = lambda w: pl.BlockSpec((1, tm, w), lambda bi, ti: (bi, ti, 0))
    once = lambda shape: pl.BlockSpec((1,) + shape, lambda bi, ti: (layer, 0, 0), pipeline_mode=pl.Buffered(1))
    return pl.pallas_call(
        functools.partial(_outffn_resident_kernel, tf=tf),
        grid=(b, s // tm),
        in_specs=[
            tile(d), tile(NA_WIDTH), tile(SSD_WIDTH), tile(MLA_WIDTH),
            pl.BlockSpec((1, 1, 6, d), lambda bi, ti: (layer, mod_row_fn(bi), 0, 0)),
            pl.BlockSpec((1, 1, d), lambda bi, ti: (layer, 0, 0)),
            once((MIX_WIDTH, d)), once((d, D_FF)), once((D_FF, d)),
        ],
        out_specs=tile(d),
        out_shape=jax.ShapeDtypeStruct((b, s, d), F32),
        compiler_params=_cparams(("parallel", "parallel")),
        name="outproj_mlp_res",
    )(xs, na, ssd, mla, mod, norm_w.reshape(depth, 1, d), w_out, w1, w2)


def _rope_tables(n_tokens):
    pos = jnp.arange(n_tokens)
    axes = jnp.stack([pos // GRID_W, pos % GRID_W], axis=-1).astype(F32)
    n_freq = MLA_ROPE // 4
    inv_freq = ROPE_THETA ** (-jnp.arange(n_freq, dtype=F32) / n_freq)
    ang = axes[:, :, None] * inv_freq
    cos, sin = jnp.cos(ang), jnp.sin(ang)
    cos_l = jnp.stack([cos, cos], axis=2).reshape(n_tokens, MLA_ROPE)
    sin_l = jnp.stack([-sin, sin], axis=2).reshape(n_tokens, MLA_ROPE)

    def place(t, fill):
        left = jnp.full((n_tokens, MLA_NOPE), fill, F32)
        right = jnp.full((n_tokens, LANES - MLA_QK), fill, F32)
        return jnp.concatenate([left, t, right], axis=1)

    return place(cos_l, 1.0), place(sin_l, 0.0)


def _prep_params(w_in, na_qn_w, na_kn_w, ssd_conv_w, ssd_conv_b, ssd_dt_bias, ssd_a_log, ssd_d, ssd_norm_w,
                 mla_cq_norm_w, mla_ckv_norm_w, mla_w_uq, mla_w_ukv, mla_qn_w, mla_kn_w):
    depth = w_in.shape[0]
    d = w_in.shape[1]
    dtw = w_in[..., 2304:2320]
    w_in_p = jnp.concatenate(
        [w_in[..., :2304], w_in[..., 2320:2576], w_in[..., 2576:2704], w_in[..., 2704:2736]]
        + [dtw] * DT_COPIES + [jnp.zeros((depth, d, IN_WIDTH_PAD - 2720 - 16 * DT_COPIES), F32)], axis=-1).astype(BF16)
    qnw = jnp.tile(na_qn_w, (1, NA_HEADS))[:, None, :]
    knw = jnp.tile(na_kn_w, (1, NA_HEADS))[:, None, :]
    gi = np.arange(NA_WIDTH) // NA_HEAD_DIM
    grp = jnp.asarray((gi[:, None] == gi[None, :]).astype(np.float32), BF16)

    convw = jnp.pad(ssd_conv_w, ((0, 0), (0, 8 - SSD_CONV), (0, 0)))
    convb = ssd_conv_b[:, None, :]

    def dt_row(t):
        flat = jnp.tile(t.reshape(depth, 16), (1, DT_COPIES))
        return jnp.pad(flat, ((0, 0), (DT_LANE0, LANES - DT_LANE0 - 16 * DT_COPIES)))[:, None, :]

    bias_row = dt_row(ssd_dt_bias)
    alog_row = dt_row(ssd_a_log)
    dskip = jnp.repeat(ssd_d, SSD_HEAD_DIM, axis=-1)[:, None, :]
    normw = ssd_norm_w[:, None, :]
    ssd_params = (convw, convb, bias_row, alog_row, dskip, normw)

    wuq = mla_w_uq.reshape(depth, MLA_Q_RANK, MLA_HEADS, MLA_QK)
    wuq_p = jnp.pad(wuq, ((0, 0), (0, 0), (0, 0), (0, MLA_HEAD_PAD - MLA_QK))).reshape(depth, MLA_Q_RANK, -1).astype(BF16)
    wukv = mla_w_ukv.reshape(depth, MLA_KV_RANK, MLA_HEADS, MLA_NOPE + MLA_V)
    wk = jnp.pad(wukv[..., :MLA_NOPE], ((0, 0), (0, 0), (0, 0), (0, MLA_HEAD_PAD - MLA_NOPE))).reshape(depth, MLA_KV_RANK, -1)
    wuvt = jnp.swapaxes(wukv[..., MLA_NOPE:].reshape(depth, MLA_KV_RANK, -1), 1, 2).astype(BF16)
    pad_head = lambda t: jnp.tile(jnp.pad(t, ((0, 0), (0, MLA_HEAD_PAD - MLA_QK))), (1, MLA_HEADS))[:, None, :]
    li = np.arange(MLA_HEADS * MLA_HEAD_PAD)
    g96 = jnp.asarray((li[:, None] // MLA_HEAD_PAD == li[None, :] // MLA_HEAD_PAD).astype(np.float32), BF16)
    in_rope = (li % MLA_HEAD_PAD >= MLA_NOPE) & (li % MLA_HEAD_PAD < MLA_QK)
    swp = jnp.asarray(((li[:, None] == (li[None, :] ^ 8)) & in_rope[None, :]).astype(np.float32), BF16)
    mla_weights = (mla_cq_norm_w[:, None, :], mla_ckv_norm_w[:, None, :], wuq_p, wk.astype(BF16), wuvt,
                   pad_head(mla_qn_w), pad_head(mla_kn_w), g96, swp)
    return w_in_p, qnw, knw, grp, ssd_params, mla_weights


def kernel(x, c, ctx, c_ctx, w_ada, b_ada, norm1_w, norm2_w, w_in, w_out, na_qn_w, na_kn_w, na_rpb, ssd_conv_w, ssd_conv_b, ssd_dt_bias, ssd_a_log, ssd_d, ssd_norm_w, mla_cq_norm_w, mla_ckv_norm_w, mla_w_uq, mla_w_ukv, mla_qn_w, mla_kn_w, w_ff1, w_ff2):
    b, s, d = x.shape
    ctx_len = ctx.shape[1]
    depth = w_in.shape[0]
    n_rows = s // GRID_W
    assert d == D_MODEL and s % (16 * GRID_W) == 0 and n_rows > NA_WIN_ROWS and ctx_len % SSD_CHUNK == 0

    mod_rows = ((b + 1 + 7) // 8) * 8
    cc = jnp.concatenate([c, c_ctx[None, :], jnp.zeros((mod_rows - b - 1, d), F32)], axis=0)
    mod = _adaln(cc, w_ada, b_ada).reshape(depth, mod_rows, 6, d)

    w_in_p, qnw, knw, grp, ssd_params, mla_weights = _prep_params(
        w_in, na_qn_w, na_kn_w, ssd_conv_w, ssd_conv_b, ssd_dt_bias, ssd_a_log, ssd_d, ssd_norm_w,
        mla_cq_norm_w, mla_ckv_norm_w, mla_w_uq, mla_w_ukv, mla_qn_w, mla_kn_w)
    w_out_b = w_out.astype(BF16)
    w1_b = w_ff1.astype(BF16)
    w2_b = w_ff2.astype(BF16)
    bias_all = _na_bias(na_rpb, n_rows)
    rope_tabs = _rope_tables(s)

    lat_row = lambda bi: bi
    ctx_row = lambda bi: b
    tm = 512
    tm_c = min(tm, b * ctx_len)
    ctx_flat = ctx.reshape(1, b * ctx_len, d)

    for i in range(depth):
        need_ctx = i < depth - 1
        qkv_l, z_l, xbc_l, dt_l, qm_l, km_l, vt_l = _inproj(
            x, mod, i, lat_row, norm1_w, w_in_p, qnw, knw, grp, mla_weights, rope_tabs, tm)
        qkv_c, z_c, xbc_c, dt_c, qm_c, km_c, vt_c = _inproj(
            ctx_flat.reshape(b, ctx_len, d), mod, i, ctx_row, norm1_w, w_in_p, qnw, knw, grp, mla_weights, None, ctx_len)

        na_l = _na_attention(qkv_l, qkv_c, bias_all, i)
        ssd_l, ssd_c = _ssd(z_l, xbc_l, dt_l, z_c, xbc_c, dt_c, ssd_params, i, need_ctx)
        mla_l = _mla_attention_t(qm_l, km_l, vt_l, km_c, vt_c, 512)

        x = _outffn_resident(x, na_l, ssd_l, mla_l, mod, i, lat_row, norm2_w, w_out_b, w1_b, w2_b, tm, 1024)
        if need_ctx:
            na_c = _na_ctx_attention(qkv_c)
            mla_c = _mla_attention_t(qm_c, None, None, km_c, vt_c, ctx_len)
            flat = lambda t: t.reshape(1, b * ctx_len, t.shape[-1])
            ctx_flat = _outffn_resident(ctx_flat, flat(na_c), flat(ssd_c), flat(mla_c), mod, i, ctx_row, norm2_w,
                                        w_out_b, w1_b, w2_b, tm_c, 1024)
    return x
```

```python
import functools

import jax
import jax.numpy as jnp
import numpy as np
from jax import lax
from jax.experimental import pallas as pl
from jax.experimental.pallas import tpu as pltpu

F32 = jnp.float32
BF16 = jnp.bfloat16

D_MODEL = 1024
GRID_W = 64
D_FF = 4 * D_MODEL
NORM_EPS = 1e-6
ROPE_THETA = 10000.0
LOG2E = 1.4426950408889634
NA_HEADS = 4
NA_HEAD_DIM = 64
NA_WIDTH = 256
NA_WIN_ROWS = 8
NA_WIN_COLS = 16
SSD_HEADS = 8
SSD_HEAD_DIM = 64
SSD_WIDTH = 512
SSD_GROUPS = 2
SSD_STATE = 128
SSD_CONV = 5
SSD_CHUNK = 128
SSD_CONV_CH = 1024
MLA_HEADS = 4
MLA_NOPE = 64
MLA_ROPE = 32
MLA_V = 64
MLA_QK = 96
MLA_Q_RANK = 256
MLA_KV_RANK = 128
MLA_WIDTH = 256
MIX_WIDTH = 1024
IN_WIDTH = 2736

LANES = 128
MLA_HEAD_PAD = 128
IN_WIDTH_PAD = 2816
DT_LANE0 = 32
DT_COPIES = 3
VMEM_LIMIT = 56 * 1024 * 1024

ROW_TILE = 512
FF_CHUNK = 1024
ADALN_COLS = 1536
NA_ROWS_PER_STEP = 32
MLA_QUERY_TILE = 512
MLA_KEY_CHUNK = 256

NT_DIMS = (((1,), (1,)), ((), ()))
TN_DIMS = (((0,), (0,)), ((), ()))


def _dot(a, b):
    return jnp.dot(a, b, preferred_element_type=F32)


def _dot_nt(a, b):
    return lax.dot_general(a, b, NT_DIMS, preferred_element_type=F32)


def _dot_tn(a, b):
    return lax.dot_general(a, b, TN_DIMS, preferred_element_type=F32)


def _sigmoid(x):
    return 1.0 / (1.0 + jnp.exp(-x))


def _silu(x):
    return x * _sigmoid(x)


def _softplus(x):
    return jnp.maximum(x, 0.0) + jnp.log1p(jnp.exp(-jnp.abs(x)))


def _split3(x):
    x0 = x.astype(BF16)
    r = x - x0.astype(F32)
    x1 = r.astype(BF16)
    r = r - x1.astype(F32)
    return x0, x1, r.astype(BF16)


def _lane_parts(x, lane):
    p0 = x.astype(BF16).astype(F32)
    r = x - p0
    p1 = r.astype(BF16).astype(F32)
    p2 = r - p1
    return jnp.where(lane < DT_LANE0 + 16, p0, jnp.where(lane < DT_LANE0 + 32, p1, p2)).astype(BF16)


def _aligned(x, m):
    return x if isinstance(x, int) else pl.multiple_of(x, m)


def _cparams(sem):
    return pltpu.CompilerParams(dimension_semantics=sem, vmem_limit_bytes=VMEM_LIMIT)


def _adaln_kernel(c_ref, w_ref, b_ref, o_ref):
    act = _silu(c_ref[...]).astype(BF16)
    o_ref[0] = _dot(act, w_ref[0].astype(BF16)) + b_ref[0]


def _adaln(cc, w_ada, b_ada):
    depth, d, n = w_ada.shape
    rows = cc.shape[0]
    tn = ADALN_COLS
    return pl.pallas_call(
        _adaln_kernel,
        grid=(depth, n // tn),
        in_specs=[
            pl.BlockSpec((rows, d), lambda l, j: (0, 0)),
            pl.BlockSpec((1, d, tn), lambda l, j: (l, 0, j)),
            pl.BlockSpec((1, 1, tn), lambda l, j: (l, 0, j)),
        ],
        out_specs=pl.BlockSpec((1, rows, tn), lambda l, j: (l, 0, j)),
        out_shape=jax.ShapeDtypeStruct((depth, rows, n), F32),
        compiler_params=_cparams(("parallel", "parallel")),
        name="adaln",
    )(cc, w_ada, b_ada.reshape(depth, 1, n))


def _inproj_kernel(x_ref, mod_ref, nw_ref, w_ref, qnw_ref, knw_ref, grp_ref,
                   cqn_ref, ckvn_ref, wuq_ref, wuk_ref, wuvt_ref, mqnw_ref, mknw_ref, g96_ref, swp_ref, *rest, rope):
    if rope:
        cos_ref, sin_ref = rest[:2]
        rest = rest[2:]
    qkv_ref, z_ref, xbc_ref, dt_ref, qm_ref, km_ref, vt_ref = rest

    def rms(t, w):
        ms = jnp.mean(t * t, axis=-1, keepdims=True)
        return t * lax.rsqrt(ms + NORM_EPS) * w

    mod = mod_ref[0, 0]
    h = (rms(x_ref[0], nw_ref[0]) * (1.0 + mod[1:2, :]) + mod[0:1, :]).astype(BF16)
    u = _dot(h, w_ref[0])

    grp = grp_ref[...]

    def head_norm(t, w):
        ss = _dot((t * t).astype(BF16), grp)
        return t * lax.rsqrt(ss * (1.0 / NA_HEAD_DIM) + NORM_EPS) * w

    q = head_norm(u[:, 0:256], qnw_ref[0]) * (NA_HEAD_DIM ** -0.5 * LOG2E)
    k = head_norm(u[:, 256:512], knw_ref[0])
    qkv_ref[0, :, 0:256] = q.astype(BF16)
    qkv_ref[0, :, 256:512] = k.astype(BF16)
    qkv_ref[0, :, 512:768] = u[:, 512:768].astype(BF16)
    z_ref[0] = u[:, 768:1280].astype(BF16)
    xbc_ref[0] = u[:, 1280:2304].astype(BF16)
    tail = u[:, 2688:2816]
    dt_ref[0] = tail

    g96 = g96_ref[...]
    swp = swp_ref[...]
    nh = MLA_HEADS

    def inv_rms96(ss):
        return lax.rsqrt(ss * (1.0 / MLA_QK) + NORM_EPS)

    def rotate(t):
        if not rope:
            return t
        reps = t.shape[1] // LANES
        sw = swp[:t.shape[1], :t.shape[1]]
        partner = _dot(t.astype(BF16), sw)
        return t * jnp.concatenate([cos_ref[...]] * reps, axis=1) + partner * jnp.concatenate([sin_ref[...]] * reps, axis=1)

    qf = _dot(rms(u[:, 2304:2560], cqn_ref[0]).astype(BF16), wuq_ref[0])
    ckvn = rms(u[:, 2560:2688], ckvn_ref[0]).astype(BF16)
    kn = _dot(ckvn, wuk_ref[0])
    vt_ref[0] = _dot_nt(wuvt_ref[0], ckvn).astype(BF16)
    lane = lax.broadcasted_iota(jnp.int32, (1, LANES), 1)
    kr = jnp.where((lane >= MLA_NOPE) & (lane < MLA_QK), pltpu.roll(tail, MLA_NOPE, axis=1), 0.0)
    mqnw = mqnw_ref[0]
    mknw = mknw_ref[0]

    q_ss = _dot((qf * qf).astype(BF16), g96)
    qn = rotate(qf * inv_rms96(q_ss) * mqnw)
    qm_ref[0] = (qn * (MLA_QK ** -0.5 * LOG2E)).astype(BF16)
    kr_ss = jnp.sum(kr * kr, axis=-1, keepdims=True)
    kr_rot = rotate(kr * mknw[:, :LANES])
    k_ss = _dot((kn * kn).astype(BF16), g96) + kr_ss
    km_ref[0] = ((kn * mknw + jnp.concatenate([kr_rot] * nh, axis=1)) * inv_rms96(k_ss)).astype(BF16)


def _inproj(xs, mod, layer, mod_row_fn, norm_w, w_in_p, qnw, knw, grp, mla_weights, rope_tabs, tm):
    b, s, d = xs.shape
    depth = w_in_p.shape[0]
    cqn_w, ckvn_w, wuq_p, wuk_p, wuvt_p, mqnw_p, mknw_p, g96, swp = mla_weights
    rope = rope_tabs is not None
    hw = MLA_HEADS * MLA_HEAD_PAD
    outs = (
        jax.ShapeDtypeStruct((b, s, 768), BF16),
        jax.ShapeDtypeStruct((b, s, SSD_WIDTH), BF16),
        jax.ShapeDtypeStruct((b, s, SSD_CONV_CH), BF16),
        jax.ShapeDtypeStruct((b, s, LANES), F32),
        jax.ShapeDtypeStruct((b, s, hw), BF16),
        jax.ShapeDtypeStruct((b, s, hw), BF16),
        jax.ShapeDtypeStruct((b, MLA_WIDTH, s), BF16),
    )
    tile = lambda w: pl.BlockSpec((1, tm, w), lambda bi, ti: (bi, ti, 0))
    tile_t = lambda r: pl.BlockSpec((1, r, tm), lambda bi, ti: (bi, 0, ti))

    def par(shape):
        return pl.BlockSpec((1,) + shape, lambda bi, ti: (layer,) + (0,) * len(shape))

    in_specs = [
        tile(d),
        pl.BlockSpec((1, 1, 6, d), lambda bi, ti: (layer, mod_row_fn(bi), 0, 0)),
        par((1, d)), par((d, IN_WIDTH_PAD)), par((1, 256)), par((1, 256)),
        pl.BlockSpec((256, 256), lambda bi, ti: (0, 0)),
        par((1, MLA_Q_RANK)), par((1, MLA_KV_RANK)), par((MLA_Q_RANK, hw)), par((MLA_KV_RANK, hw)),
        par((MLA_WIDTH, MLA_KV_RANK)),
        par((1, hw)), par((1, hw)),
        pl.BlockSpec((hw, hw), lambda bi, ti: (0, 0)), pl.BlockSpec((hw, hw), lambda bi, ti: (0, 0)),
    ]
    args = [xs, mod, norm_w.reshape(depth, 1, d), w_in_p, qnw, knw, grp, cqn_w, ckvn_w, wuq_p, wuk_p, wuvt_p, mqnw_p,
            mknw_p, g96, swp]
    if rope:
        in_specs += [pl.BlockSpec((tm, LANES), lambda bi, ti: (ti, 0))] * 2
        args += list(rope_tabs)
    return pl.pallas_call(
        functools.partial(_inproj_kernel, rope=rope),
        grid=(b, s // tm),
        in_specs=in_specs,
        out_specs=[tile(768), tile(SSD_WIDTH), tile(SSD_CONV_CH), tile(LANES), tile(hw), tile(hw), tile_t(MLA_WIDTH)],
        out_shape=outs,
        compiler_params=_cparams(("parallel", "parallel")),
        name="inproj",
    )(*args)


NA_VARIANT_ROWS = (0, 1, 2, 3, 4, -3, -2, -1)


def _na_bias_kernel(rpb_ref, o_ref, t_ref, *, n_rows):
    nrow_off = 2 * NA_WIN_ROWS - 1
    ncol_off = 2 * NA_WIN_COLS - 1
    base = (pl.program_id(0) * NA_HEADS + pl.program_id(1)) * (nrow_off * ncol_off)
    qi = lax.broadcasted_iota(jnp.int32, (GRID_W, LANES), 0)
    li = lax.broadcasted_iota(jnp.int32, (GRID_W, LANES), 1)
    kcol = li & (GRID_W - 1)
    colidx = jnp.clip(kcol - qi, -(NA_WIN_COLS - 1), NA_WIN_COLS - 1) + (NA_WIN_COLS - 1)
    cstart = jnp.clip(qi - NA_WIN_COLS // 2, 0, GRID_W - NA_WIN_COLS)
    valid = (kcol >= cstart) & (kcol < cstart + NA_WIN_COLS)
    for d in range(nrow_off):
        acc = jnp.zeros((GRID_W, LANES), F32)
        for j in range(ncol_off):
            acc = jnp.where(colidx == j, rpb_ref[base + d * ncol_off + j], acc)
        t_ref[d] = jnp.where(valid, acc * LOG2E, -jnp.inf)
    for v, r_rep in enumerate(NA_VARIANT_ROWS):
        r = r_rep if r_rep >= 0 else n_rows + r_rep
        rs = min(max(r - NA_WIN_ROWS // 2, 0), n_rows - NA_WIN_ROWS)
        for p in range(NA_WIN_ROWS // 2):
            d0 = rs + 2 * p - r + (NA_WIN_ROWS - 1)
            o_ref[0, v, :, p * LANES:(p + 1) * LANES] = jnp.where(li < GRID_W, t_ref[d0], t_ref[d0 + 1])


def _na_bias(rpb, n_rows):
    depth, heads = rpb.shape[0], rpb.shape[1]
    n_win = NA_WIN_ROWS * GRID_W
    return pl.pallas_call(
        functools.partial(_na_bias_kernel, n_rows=n_rows),
        grid=(depth, heads),
        in_specs=[pl.BlockSpec(memory_space=pltpu.SMEM)],
        out_specs=pl.BlockSpec((1, 8, GRID_W, n_win), lambda l, h: (l, 0, h, 0)),
        out_shape=jax.ShapeDtypeStruct((depth, 8, heads * GRID_W, n_win), F32),
        scratch_shapes=[pltpu.VMEM((2 * NA_WIN_ROWS - 1, GRID_W, LANES), F32)],
        compiler_params=_cparams(("parallel", "parallel")),
        name="na_bias",
    )(rpb.reshape(-1))


def _head_masks():
    lane = lax.broadcasted_iota(jnp.int32, (1, NA_WIDTH), 1)
    return [jnp.where((lane >= h * NA_HEAD_DIM) & (lane < (h + 1) * NA_HEAD_DIM), 1.0, 0.0) for h in range(NA_HEADS)]


def _stacked_scores(q, keys, biases, hms):
    qs = jnp.concatenate([q * hm.astype(BF16) for hm in hms], axis=0)
    scores = []
    for k, b in zip(keys, biases):
        s = _dot_nt(qs, k)
        scores.append(s if b is None else s + b)
    return scores


def _stacked_attend(scores, values, hms):
    m_rows = scores[0].shape[0] // NA_HEADS
    m = scores[0].max(axis=-1, keepdims=True)
    for s in scores[1:]:
        m = jnp.maximum(m, s.max(axis=-1, keepdims=True))
    l = None
    o = None
    for s, v in zip(scores, values):
        p = jnp.exp2(s - m)
        ls = p.sum(axis=-1, keepdims=True)
        os_ = _dot(p.astype(BF16), v)
        l = ls if l is None else l + ls
        o = os_ if o is None else o + os_
    o = o * (1.0 / l)
    out = o[0:m_rows] * hms[0]
    for h in range(1, NA_HEADS):
        out = out + o[h * m_rows:(h + 1) * m_rows] * hms[h]
    return out


def _na_kernel(q_ref, k_ref, v_ref, kc_ref, vc_ref, bias_ref, o_ref, *, rows_per_step, n_rows):
    blk = pl.program_id(1)
    kc = kc_ref[0]
    vc = vc_ref[0]
    n_win = NA_WIN_ROWS * GRID_W
    hms = _head_masks()

    def window_start(i):
        r = blk * rows_per_step + i
        rs = jnp.clip(r - NA_WIN_ROWS // 2, 0, n_rows - NA_WIN_ROWS)
        var = jnp.where(r < NA_WIN_ROWS // 2, r,
                        jnp.where(r > n_rows - NA_WIN_ROWS // 2, r - (n_rows - NA_WIN_ROWS), NA_WIN_ROWS // 2))
        return pl.multiple_of(rs * GRID_W, GRID_W), var

    def row_scores(i):
        start, var = window_start(i)
        q = q_ref[0, i * GRID_W:(i + 1) * GRID_W, :]
        return _stacked_scores(q, [k_ref[0, pl.ds(start, n_win), :], kc], [bias_ref[0, var], None], hms)

    nxt = row_scores(0)
    for i in range(rows_per_step):
        cur = nxt
        if i + 1 < rows_per_step:
            nxt = row_scores(i + 1)
        start, _ = window_start(i)
        out = _stacked_attend(cur, [v_ref[0, pl.ds(start, n_win), :], vc], hms)
        o_ref[0, i * GRID_W:(i + 1) * GRID_W, :] = out.astype(BF16)


def _na_attention(qkv_l, qkv_c, bias_all, layer, rows_per_step=NA_ROWS_PER_STEP):
    b, s, _ = qkv_l.shape
    ctx_len = qkv_c.shape[1]
    n_rows = s // GRID_W
    tq = rows_per_step * GRID_W
    return pl.pallas_call(
        functools.partial(_na_kernel, rows_per_step=rows_per_step, n_rows=n_rows),
        grid=(b, n_rows // rows_per_step),
        in_specs=[
            pl.BlockSpec((1, tq, 256), lambda bi, ti: (bi, ti, 0)),
            pl.BlockSpec((1, s, 256), lambda bi, ti: (bi, 0, 1)),
            pl.BlockSpec((1, s, 256), lambda bi, ti: (bi, 0, 2)),
            pl.BlockSpec((1, ctx_len, 256), lambda bi, ti: (bi, 0, 1)),
            pl.BlockSpec((1, ctx_len, 256), lambda bi, ti: (bi, 0, 2)),
            pl.BlockSpec((1,) + bias_all.shape[1:], lambda bi, ti: (layer, 0, 0, 0)),
        ],
        out_specs=pl.BlockSpec((1, tq, 256), lambda bi, ti: (bi, ti, 0)),
        out_shape=jax.ShapeDtypeStruct((b, s, NA_WIDTH), BF16),
        compiler_params=_cparams(("parallel", "parallel")),
        name="na_attn",
    )(qkv_l, qkv_l, qkv_l, qkv_c, qkv_c, bias_all)


def _na_ctx_kernel(q_ref, k_ref, v_ref, o_ref):
    hms = _head_masks()
    scores = _stacked_scores(q_ref[0], [k_ref[0]], [None], hms)
    o_ref[0] = _stacked_attend(scores, [v_ref[0]], hms).astype(BF16)


def _na_ctx_attention(qkv_c):
    b, c, _ = qkv_c.shape
    spec = lambda j: pl.BlockSpec((1, c, 256), lambda bi: (bi, 0, j))
    return pl.pallas_call(
        _na_ctx_kernel,
        grid=(b,),
        in_specs=[spec(0), spec(1), spec(2)],
        out_specs=spec(0),
        out_shape=jax.ShapeDtypeStruct((b, c, NA_WIDTH), BF16),
        compiler_params=_cparams(("parallel",)),
        name="na_ctx_attn",
    )(qkv_c, qkv_c, qkv_c)


def _ssd_kernel(z_l, xbc_l, dt_l, z_c, xbc_c, dt_c,
                convw_ref, convb_ref, bias_row_ref, alog_row_ref, dskip_ref, normw_ref, out_l, out_c,
                xs_s, bc_s, yf_s, yb_s, stf_s, stb_s, dt_s, acf_s, acb_s, g_s, *, n_lat, n_ctx, need_ctx):
    q = SSD_CHUNK
    lat_row0 = n_ctx * q

    convw = convw_ref[0]
    convb = convb_ref[0]
    win = q + 32
    wi = lax.broadcasted_iota(jnp.int32, (q, win), 0)
    wr = lax.broadcasted_iota(jnp.int32, (q, win), 1)
    mid = SSD_CONV // 2

    side_taps = [j for j in range(SSD_CONV) if j != mid]

    def shift_mats(off):
        return jnp.concatenate([jnp.where(wr == wi + (off + j - mid), 1.0, 0.0).astype(BF16) for j in side_taps], axis=0)

    def conv_chunk(xref, lo, off, smats, dst):
        w_in = xref[0, pl.ds(lo, win), :]
        acc = convb + w_in[off:off + q].astype(F32) * convw[mid:mid + 1, :]
        shifted = _dot(smats, w_in)
        for n, j in enumerate(side_taps):
            acc = acc + shifted[n * q:(n + 1) * q] * convw[j:j + 1, :]
        act = _silu(acc)
        xs_s[pl.ds(dst, q), :] = act[:, :SSD_WIDTH]
        bc_s[pl.ds(dst, q), :] = act[:, SSD_WIDTH:].astype(BF16)

    def conv_seq(xref, n, row0):
        conv_chunk(xref, 0, 0, shift_mats(0), row0)
        conv_chunk(xref, (n - 1) * q - 32, 32, shift_mats(32), row0 + (n - 1) * q)
        if n > 2:
            smats = shift_mats(16)

            def body(c, carry):
                conv_chunk(xref, pl.multiple_of(c * q - 16, 16), 16, smats, pl.multiple_of(row0 + c * q, q))
                return carry
            lax.fori_loop(1, n - 1, body, 0, unroll=2)

    conv_seq(xbc_c, n_ctx, 0)
    conv_seq(xbc_l, n_lat, lat_row0)

    lane = lax.broadcasted_iota(jnp.int32, (1, LANES), 1)
    dmask = (lane >= DT_LANE0) & (lane < DT_LANE0 + 16 * DT_COPIES)
    bias_row = bias_row_ref[0]
    a_row = jnp.where(dmask, -jnp.exp(alog_row_ref[0]) * LOG2E, 0.0)
    ri = lax.broadcasted_iota(jnp.int32, (q, q), 0)
    ci = lax.broadcasted_iota(jnp.int32, (q, q), 1)
    lo_mask = ci <= ri
    up_mask = ri <= ci
    tri_lo = jnp.where(lo_mask, 1.0, 0.0).astype(BF16)
    tri_up = jnp.where(up_mask, 1.0, 0.0).astype(BF16)
    sel_r = lax.broadcasted_iota(jnp.int32, (LANES, SSD_HEADS * LANES), 0)
    sel_c = lax.broadcasted_iota(jnp.int32, (LANES, SSD_HEADS * LANES), 1)
    sel64_r = lax.broadcasted_iota(jnp.int32, (LANES, SSD_WIDTH), 0)
    sel64_c = lax.broadcasted_iota(jnp.int32, (LANES, SSD_WIDTH), 1)
    lane_lt64 = lax.broadcasted_iota(jnp.int32, (1, LANES), 1) < SSD_HEAD_DIM
    pair_lo = jnp.where(lane_lt64, 1.0, 0.0).astype(BF16)
    pair_hi = jnp.where(lane_lt64, 0.0, 1.0).astype(BF16)

    def sel_matrices(direction):
        in3 = (sel_r >= DT_LANE0) & (sel_r < DT_LANE0 + 48)
        sel128 = jnp.where(in3 & (((sel_r - DT_LANE0) & 15) == direction * SSD_HEADS + (sel_c >> 7)), 1.0, 0.0)
        in2 = (sel64_r >= DT_LANE0) & (sel64_r < DT_LANE0 + 32)
        sel64 = jnp.where(in2 & (((sel64_r - DT_LANE0) & 15) == direction * SSD_HEADS + (sel64_c >> 6)), 1.0, 0.0)
        return sel128.astype(BF16), sel64.astype(BF16)

    sels = (sel_matrices(0), sel_matrices(1))
    tri3_both = jnp.concatenate([jnp.concatenate([tri_lo] * 3, axis=1), jnp.concatenate([tri_up] * 3, axis=1)], axis=0)
    masks = (lo_mask, up_mask)

    y_refs = (yf_s, yb_s)
    st_refs = (stf_s, stb_s)
    stf_s[...] = jnp.zeros_like(stf_s)
    stb_s[...] = jnp.zeros_like(stb_s)

    ac_refs = (acf_s, acb_s)

    def decay_chunk(dt_ref, c, row0):
        rows = pl.ds(_aligned(row0 + c * q, q), q)
        raw = dt_ref[0, pl.ds(_aligned(c * q, q), q), :]
        dt = jnp.where(dmask, _softplus(raw + bias_row), 0.0)
        a0, a1, a2 = _split3(dt * a_row)
        both = _dot(tri3_both, jnp.concatenate([a0, a1, a2], axis=0))
        dt_s[rows, :] = dt
        acf_s[rows, :] = both[0:q]
        acb_s[rows, :] = both[q:2 * q]
        bc = bc_s[rows, :]
        for g in range(SSD_GROUPS):
            g_s[rows, g * q:(g + 1) * q] = _dot_nt(bc[:, (SSD_GROUPS + g) * SSD_STATE:(SSD_GROUPS + g + 1) * SSD_STATE],
                                                   bc[:, g * SSD_STATE:(g + 1) * SSD_STATE])

    for i in range(n_ctx):
        decay_chunk(dt_c, i, 0)

    def decay_body(c, carry):
        decay_chunk(dt_l, c, lat_row0)
        return carry

    lax.fori_loop(0, n_lat, decay_body, 0, unroll=2)

    def chunk(c, row0, direction, compute_y):
        rows = pl.ds(_aligned(row0 + c * q, q), q)
        x = xs_s[rows, :]
        bc = bc_s[rows, :]
        sel128, sel64 = sels[direction]
        dt = dt_s[rows, :]
        acum = ac_refs[direction][rows, :]
        a_exp = _dot(_lane_parts(acum, lane), sel128)
        dt64 = _dot(_lane_parts(dt, lane), sel64)
        st = st_refs[direction][...]
        grp_w = SSD_WIDTH // SSD_GROUPS
        b_g = [bc[:, g * SSD_STATE:(g + 1) * SSD_STATE] for g in range(SSD_GROUPS)]
        c_g = [bc[:, (SSD_GROUPS + g) * SSD_STATE:(SSD_GROUPS + g + 1) * SSD_STATE] for g in range(SSD_GROUPS)]
        if compute_y:
            stb = st.astype(BF16)
            gmats = [g_s[rows, g * q:(g + 1) * q] for g in range(SSD_GROUPS)]
            offs = [_dot(c_g[g], stb[:, g * grp_w:(g + 1) * grp_w]) for g in range(SSD_GROUPS)]
        yield

        ac64 = jnp.concatenate(
            [jnp.where(lane_lt64, a_exp[:, (2 * j) * LANES:(2 * j + 1) * LANES],
                       a_exp[:, (2 * j + 1) * LANES:(2 * j + 2) * LANES]) for j in range(SSD_HEADS // 2)], axis=1)
        acum_t = acum.T
        end = q - 1 if direction == 0 else 0
        ac_end = ac64[end:end + 1, :]
        xd = x * dt64
        xdb = xd.astype(BF16)
        xdw = (xd * jnp.exp2(ac_end - ac64)).astype(BF16)
        yield

        if compute_y:
            parts = []
            for pair in range(SSD_HEADS // 2):
                h0 = 2 * pair
                gmat = gmats[h0 // (SSD_HEADS // SSD_GROUPS)]
                xp = xdb[:, h0 * SSD_HEAD_DIM:(h0 + 2) * SSD_HEAD_DIM]
                mats = []
                for hh in (h0, h0 + 1):
                    row = DT_LANE0 + direction * SSD_HEADS + hh
                    dmat = a_exp[:, hh * LANES:(hh + 1) * LANES] - acum_t[row:row + 1, :]
                    lmat = jnp.exp2(jnp.where(masks[direction], dmat, -jnp.inf))
                    mats.append((gmat * lmat).astype(BF16))
                parts.append(_dot(jnp.concatenate(mats, axis=1),
                                  jnp.concatenate([xp * pair_lo, xp * pair_hi], axis=0)))
                yield
            y = jnp.concatenate(parts, axis=1) + jnp.concatenate(offs, axis=1) * jnp.exp2(ac64)
            y_refs[direction][rows, :] = y

        new = [_dot_tn(b_g[g], xdw[:, g * grp_w:(g + 1) * grp_w]) for g in range(SSD_GROUPS)]
        st_refs[direction][...] = st * jnp.exp2(ac_end) + jnp.concatenate(new, axis=1)
        yield

    def run_interleaved(*gens):
        live = list(gens)
        while live:
            for g in list(live):
                try:
                    next(g)
                except StopIteration:
                    live.remove(g)

    for i in range(n_ctx):
        run_interleaved(chunk(i, 0, 0, need_ctx), chunk(n_ctx - 1 - i, 0, 1, need_ctx))

    def lat_body(i, carry):
        run_interleaved(chunk(i, lat_row0, 0, True), chunk(n_lat - 1 - i, lat_row0, 1, True))
        return carry

    lax.fori_loop(0, n_lat, lat_body, 0, unroll=2)

    dskip = dskip_ref[0]
    normw = normw_ref[0]

    def finish(z_ref, o_ref, c, row0):
        rows = pl.ds(_aligned(row0 + c * q, q), q)
        crow = pl.ds(_aligned(c * q, q), q)
        y = yf_s[rows, :] + yb_s[rows, :] + dskip * xs_s[rows, :]
        y = y * _silu(z_ref[0, crow, :].astype(F32))
        ms = jnp.mean(y * y, axis=-1, keepdims=True)
        o_ref[0, crow, :] = (y * lax.rsqrt(ms + NORM_EPS) * normw).astype(BF16)

    if need_ctx:
        for i in range(n_ctx):
            finish(z_c, out_c, i, 0)
    else:
        out_c[...] = jnp.zeros_like(out_c)

    def fin_body(c, carry):
        finish(z_l, out_l, c, lat_row0)
        return carry

    lax.fori_loop(0, n_lat, fin_body, 0, unroll=2)


def _ssd(z_l, xbc_l, dt_l, z_c, xbc_c, dt_c, params, layer, need_ctx):
    b, s, _ = z_l.shape
    c = z_c.shape[1]
    n_lat, n_ctx = s // SSD_CHUNK, c // SSD_CHUNK
    t = s + c
    convw, convb, bias_row, alog_row, dskip, normw = params

    def seq_specs(n):
        return [
            pl.BlockSpec((1, n, SSD_WIDTH), lambda bi: (bi, 0, 0)),
            pl.BlockSpec((1, n, SSD_CONV_CH), lambda bi: (bi, 0, 0)),
            pl.BlockSpec((1, n, LANES), lambda bi: (bi, 0, 0)),
        ]

    def par(shape):
        return pl.BlockSpec((1,) + shape, lambda bi: (layer,) + (0,) * len(shape))

    return pl.pallas_call(
        functools.partial(_ssd_kernel, n_lat=n_lat, n_ctx=n_ctx, need_ctx=need_ctx),
        grid=(b,),
        in_specs=seq_specs(s) + seq_specs(c) + [
            par((8, SSD_CONV_CH)), par((1, SSD_CONV_CH)), par((1, LANES)), par((1, LANES)),
            par((1, SSD_WIDTH)), par((1, SSD_WIDTH)),
        ],
        out_specs=[
            pl.BlockSpec((1, s, SSD_WIDTH), lambda bi: (bi, 0, 0)),
            pl.BlockSpec((1, c, SSD_WIDTH), lambda bi: (bi, 0, 0)),
        ],
        out_shape=(jax.ShapeDtypeStruct((b, s, SSD_WIDTH), BF16), jax.ShapeDtypeStruct((b, c, SSD_WIDTH), BF16)),
        scratch_shapes=[
            pltpu.VMEM((t, SSD_WIDTH), F32),
            pltpu.VMEM((t, SSD_WIDTH), BF16),
            pltpu.VMEM((t, SSD_WIDTH), F32),
            pltpu.VMEM((t, SSD_WIDTH), F32),
            pltpu.VMEM((SSD_STATE, SSD_WIDTH), F32),
            pltpu.VMEM((SSD_STATE, SSD_WIDTH), F32),
            pltpu.VMEM((t, LANES), F32),
            pltpu.VMEM((t, LANES), F32),
            pltpu.VMEM((t, LANES), F32),
            pltpu.VMEM((t, SSD_GROUPS * SSD_CHUNK), F32),
        ],
        compiler_params=_cparams(("parallel",)),
        name="ssd",
    )(z_l, xbc_l, dt_l, z_c, xbc_c, dt_c, convw, convb, bias_row, alog_row, dskip, normw)


def _mla_t_kernel(*refs, with_latent):
    if with_latent:
        q_ref, kl_ref, vtl_ref, kc_ref, vtc_ref, o_ref, s_scr = refs
        n_lat = kl_ref.shape[1]
    else:
        q_ref, kc_ref, vtc_ref, o_ref, s_scr = refs
        kl_ref = vtl_ref = None
        n_lat = 0
    n_ctx = kc_ref.shape[1]
    kc_ = min(MLA_KEY_CHUNK, n_ctx)
    chunks = [(kl_ref, vtl_ref, j * kc_, j * kc_) for j in range(n_lat // kc_)]
    chunks += [(kc_ref, vtc_ref, j * kc_, n_lat + j * kc_) for j in range(n_ctx // kc_)]
    key_blocks = ([(kl_ref, 0)] if with_latent else []) + [(kc_ref, n_lat)]

    def scores(h):
        sl = slice(h * MLA_HEAD_PAD, (h + 1) * MLA_HEAD_PAD)
        qh = q_ref[0, :, sl]
        m = None
        for k_ref, dst in key_blocks:
            n = k_ref.shape[1]
            s = _dot_nt(k_ref[0, :, sl], qh)
            s_scr[h % 2, dst:dst + n, :] = s
            mc = s.max(axis=0, keepdims=True)
            m = mc if m is None else jnp.maximum(m, mc)
        return m

    def attend(h, m):
        vrows = slice(h * MLA_V, (h + 1) * MLA_V)
        l = None
        o = None
        for _, vt_ref, src, dst in chunks:
            p = jnp.exp2(s_scr[h % 2, dst:dst + kc_, :] - m)
            lc = p.sum(axis=0, keepdims=True)
            oc = _dot(vt_ref[0, vrows, src:src + kc_], p.astype(BF16))
            l = lc if l is None else l + lc
            o = oc if o is None else o + oc
        return o * (1.0 / l)

    outs = []
    m_next = scores(0)
    for h in range(MLA_HEADS):
        m_cur = m_next
        if h + 1 < MLA_HEADS:
            m_next = scores(h + 1)
        outs.append(attend(h, m_cur))
    o_ref[0] = jnp.concatenate(outs, axis=0).T.astype(BF16)


def _mla_attention_t(q, k_l, vt_l, k_c, vt_c, tq):
    b, sq, wq = q.shape
    c = k_c.shape[1]
    with_latent = k_l is not None
    s = k_l.shape[1] if with_latent else 0
    in_specs = [pl.BlockSpec((1, tq, wq), lambda bi, ti: (bi, ti, 0))]
    args = [q]
    if with_latent:
        in_specs += [pl.BlockSpec((1, s, wq), lambda bi, ti: (bi, 0, 0)),
                     pl.BlockSpec((1, MLA_WIDTH, s), lambda bi, ti: (bi, 0, 0))]
        args += [k_l, vt_l]
    in_specs += [pl.BlockSpec((1, c, wq), lambda bi, ti: (bi, 0, 0)),
                 pl.BlockSpec((1, MLA_WIDTH, c), lambda bi, ti: (bi, 0, 0))]
    args += [k_c, vt_c]
    return pl.pallas_call(
        functools.partial(_mla_t_kernel, with_latent=with_latent),
        grid=(b, sq // tq),
        in_specs=in_specs,
        out_specs=pl.BlockSpec((1, tq, MLA_WIDTH), lambda bi, ti: (bi, ti, 0)),
        out_shape=jax.ShapeDtypeStruct((b, sq, MLA_WIDTH), BF16),
        scratch_shapes=[pltpu.VMEM((2, s + c, tq), F32)],
        compiler_params=_cparams(("parallel", "parallel")),
        name="mla_attn" if with_latent else "mla_ctx_attn",
    )(*args)


def _outffn_resident_kernel(x_ref, na_ref, ssd_ref, mla_ref, mod_ref, nw_ref, wo_ref, w1_ref, w2_ref, o_ref, *, tf):
    mod = mod_ref[0, 0]
    mix = (_dot(na_ref[0], wo_ref[0, 0:NA_WIDTH, :])
           + _dot(ssd_ref[0], wo_ref[0, NA_WIDTH:NA_WIDTH + SSD_WIDTH, :])
           + _dot(mla_ref[0], wo_ref[0, NA_WIDTH + SSD_WIDTH:, :]))
    x1 = x_ref[0] + mod[2:3, :] * mix
    ms = jnp.mean(x1 * x1, axis=-1, keepdims=True)
    xn = x1 * lax.rsqrt(ms + NORM_EPS) * nw_ref[0]
    xn = (xn * (1.0 + mod[4:5, :]) + mod[3:4, :]).astype(BF16)
    acc = None
    for f in range(D_FF // tf):
        hmid = _dot(xn, w1_ref[0, :, f * tf:(f + 1) * tf])
        hmid = jnp.square(jnp.maximum(hmid, 0.0)).astype(BF16)
        part = _dot(hmid, w2_ref[0, f * tf:(f + 1) * tf, :])
        acc = part if acc is None else acc + part
    o_ref[0] = x1 + mod[5:6, :] * acc


def _outffn_resident(xs, na, ssd, mla, mod, layer, mod_row_fn, norm_w, w_out, w1, w2, tm, tf):
    b, s, d = xs.shape
    depth = w_out.shape[0]
    tile = lambda w: pl.BlockSpec((1, tm, w), lambda bi, ti: (bi, ti, 0))
    once = lambda shape: pl.BlockSpec((1,) + shape, lambda bi, ti: (layer, 0, 0), pipeline_mode=pl.Buffered(1))
    return pl.pallas_call(
        functools.partial(_outffn_resident_kernel, tf=tf),
        grid=(b, s // tm),
        in_specs=[
            tile(d), tile(NA_WIDTH), tile(SSD_WIDTH), tile(MLA_WIDTH),
            pl.BlockSpec((1, 1, 6, d), lambda bi, ti: (layer, mod_row_fn(bi), 0, 0)),
            pl.BlockSpec((1, 1, d), lambda bi, ti: (layer, 0, 0)),
            once((MIX_WIDTH, d)), once((d, D_FF)), once((D_FF, d)),
        ],
        out_specs=tile(d),
        out_shape=jax.ShapeDtypeStruct((b, s, d), F32),
        compiler_params=_cparams(("parallel", "parallel")),
        name="outproj_mlp_res",
    )(xs, na, ssd, mla, mod, norm_w.reshape(depth, 1, d), w_out, w1, w2)


def _rope_tables(n_tokens):
    pos = jnp.arange(n_tokens)
    axes = jnp.stack([pos // GRID_W, pos % GRID_W], axis=-1).astype(F32)
    n_freq = MLA_ROPE // 4
    inv_freq = ROPE_THETA ** (-jnp.arange(n_freq, dtype=F32) / n_freq)
    ang = axes[:, :, None] * inv_freq
    cos, sin = jnp.cos(ang), jnp.sin(ang)
    cos_l = jnp.stack([cos, cos], axis=2).reshape(n_tokens, MLA_ROPE)
    sin_l = jnp.stack([-sin, sin], axis=2).reshape(n_tokens, MLA_ROPE)

    def place(t, fill):
        left = jnp.full((n_tokens, MLA_NOPE), fill, F32)
        right = jnp.full((n_tokens, LANES - MLA_QK), fill, F32)
        return jnp.concatenate([left, t, right], axis=1)

    return place(cos_l, 1.0), place(sin_l, 0.0)


def _prep_params(w_in, na_qn_w, na_kn_w, ssd_conv_w, ssd_conv_b, ssd_dt_bias, ssd_a_log, ssd_d, ssd_norm_w,
                 mla_cq_norm_w, mla_ckv_norm_w, mla_w_uq, mla_w_ukv, mla_qn_w, mla_kn_w):
    depth = w_in.shape[0]
    d = w_in.shape[1]
    dtw = w_in[..., 2304:2320]
    w_in_p = jnp.concatenate(
        [w_in[..., :2304], w_in[..., 2320:2576], w_in[..., 2576:2704], w_in[..., 2704:2736]]
        + [dtw] * DT_COPIES + [jnp.zeros((depth, d, IN_WIDTH_PAD - 2720 - 16 * DT_COPIES), F32)], axis=-1).astype(BF16)
    qnw = jnp.tile(na_qn_w, (1, NA_HEADS))[:, None, :]
    knw = jnp.tile(na_kn_w, (1, NA_HEADS))[:, None, :]
    gi = np.arange(NA_WIDTH) // NA_HEAD_DIM
    grp = jnp.asarray((gi[:, None] == gi[None, :]).astype(np.float32), BF16)

    convw = jnp.pad(ssd_conv_w, ((0, 0), (0, 8 - SSD_CONV), (0, 0)))
    convb = ssd_conv_b[:, None, :]

    def dt_row(t):
        flat = jnp.tile(t.reshape(depth, 16), (1, DT_COPIES))
        return jnp.pad(flat, ((0, 0), (DT_LANE0, LANES - DT_LANE0 - 16 * DT_COPIES)))[:, None, :]

    bias_row = dt_row(ssd_dt_bias)
    alog_row = dt_row(ssd_a_log)
    dskip = jnp.repeat(ssd_d, SSD_HEAD_DIM, axis=-1)[:, None, :]
    normw = ssd_norm_w[:, None, :]
    ssd_params = (convw, convb, bias_row, alog_row, dskip, normw)

    wuq = mla_w_uq.reshape(depth, MLA_Q_RANK, MLA_HEADS, MLA_QK)
    wuq_p = jnp.pad(wuq, ((0, 0), (0, 0), (0, 0), (0, MLA_HEAD_PAD - MLA_QK))).reshape(depth, MLA_Q_RANK, -1).astype(BF16)
    wukv = mla_w_ukv.reshape(depth, MLA_KV_RANK, MLA_HEADS, MLA_NOPE + MLA_V)
    wk = jnp.pad(wukv[..., :MLA_NOPE], ((0, 0), (0, 0), (0, 0), (0, MLA_HEAD_PAD - MLA_NOPE))).reshape(depth, MLA_KV_RANK, -1)
    wuvt = jnp.swapaxes(wukv[..., MLA_NOPE:].reshape(depth, MLA_KV_RANK, -1), 1, 2).astype(BF16)
    pad_head = lambda t: jnp.tile(jnp.pad(t, ((0, 0), (0, MLA_HEAD_PAD - MLA_QK))), (1, MLA_HEADS))[:, None, :]
    li = np.arange(MLA_HEADS * MLA_HEAD_PAD)
    g96 = jnp.asarray((li[:, None] // MLA_HEAD_PAD == li[None, :] // MLA_HEAD_PAD).astype(np.float32), BF16)
    in_rope = (li % MLA_HEAD_PAD >= MLA_NOPE) & (li % MLA_HEAD_PAD < MLA_QK)
    swp = jnp.asarray(((li[:, None] == (li[None, :] ^ 8)) & in_rope[None, :]).astype(np.float32), BF16)
    mla_weights = (mla_cq_norm_w[:, None, :], mla_ckv_norm_w[:, None, :], wuq_p, wk.astype(BF16), wuvt,
                   pad_head(mla_qn_w), pad_head(mla_kn_w), g96, swp)
    return w_in_p, qnw, knw, grp, ssd_params, mla_weights


def kernel(x, c, ctx, c_ctx, w_ada, b_ada, norm1_w, norm2_w, w_in, w_out, na_qn_w, na_kn_w, na_rpb, ssd_conv_w, ssd_conv_b, ssd_dt_bias, ssd_a_log, ssd_d, ssd_norm_w, mla_cq_norm_w, mla_ckv_norm_w, mla_w_uq, mla_w_ukv, mla_qn_w, mla_kn_w, w_ff1, w_ff2):
    b, s, d = x.shape
    ctx_len = ctx.shape[1]
    depth = w_in.shape[0]
    n_rows = s // GRID_W
    assert d == D_MODEL and n_rows % NA_ROWS_PER_STEP == 0 and n_rows > NA_WIN_ROWS and ctx_len % SSD_CHUNK == 0
    assert s % ROW_TILE == 0 and s % MLA_QUERY_TILE == 0 and ctx_len % MLA_KEY_CHUNK == 0

    mod_rows = ((b + 1 + 7) // 8) * 8
    cc = jnp.concatenate([c, c_ctx[None, :], jnp.zeros((mod_rows - b - 1, d), F32)], axis=0)
    mod = _adaln(cc, w_ada, b_ada).reshape(depth, mod_rows, 6, d)

    w_in_p, qnw, knw, grp, ssd_params, mla_weights = _prep_params(
        w_in, na_qn_w, na_kn_w, ssd_conv_w, ssd_conv_b, ssd_dt_bias, ssd_a_log, ssd_d, ssd_norm_w,
        mla_cq_norm_w, mla_ckv_norm_w, mla_w_uq, mla_w_ukv, mla_qn_w, mla_kn_w)
    w_out_b = w_out.astype(BF16)
    w1_b = w_ff1.astype(BF16)
    w2_b = w_ff2.astype(BF16)
    bias_all = _na_bias(na_rpb, n_rows)
    rope_tabs = _rope_tables(s)

    lat_row = lambda bi: bi
    ctx_row = lambda bi: b
    tm = ROW_TILE
    tm_c = min(tm, b * ctx_len)
    ctx_flat = ctx.reshape(1, b * ctx_len, d)

    for i in range(depth):
        need_ctx = i < depth - 1
        qkv_l, z_l, xbc_l, dt_l, qm_l, km_l, vt_l = _inproj(
            x, mod, i, lat_row, norm1_w, w_in_p, qnw, knw, grp, mla_weights, rope_tabs, tm)
        qkv_c, z_c, xbc_c, dt_c, qm_c, km_c, vt_c = _inproj(
            ctx_flat.reshape(b, ctx_len, d), mod, i, ctx_row, norm1_w, w_in_p, qnw, knw, grp, mla_weights, None, ctx_len)

        na_l = _na_attention(qkv_l, qkv_c, bias_all, i)
        ssd_l, ssd_c = _ssd(z_l, xbc_l, dt_l, z_c, xbc_c, dt_c, ssd_params, i, need_ctx)
        mla_l = _mla_attention_t(qm_l, km_l, vt_l, km_c, vt_c, MLA_QUERY_TILE)

        x = _outffn_resident(x, na_l, ssd_l, mla_l, mod, i, lat_row, norm2_w, w_out_b, w1_b, w2_b, tm, FF_CHUNK)
        if need_ctx:
            na_c = _na_ctx_attention(qkv_c)
            mla_c = _mla_attention_t(qm_c, None, None, km_c, vt_c, ctx_len)
            flat = lambda t: t.reshape(1, b * ctx_len, t.shape[-1])
            ctx_flat = _outffn_resident(ctx_flat, flat(na_c), flat(ssd_c), flat(mla_c), mod, i, ctx_row, norm2_w,
                                        w_out_b, w1_b, w2_b, tm_c, FF_CHUNK)
    return x
```

```python
import functools

import jax
import jax.numpy as jnp
import numpy as np
from jax import lax
from jax.experimental import pallas as pl
from jax.experimental.pallas import tpu as pltpu

F32 = jnp.float32
BF16 = jnp.bfloat16

D_MODEL = 1024
GRID_W = 64
D_FF = 4 * D_MODEL
NORM_EPS = 1e-6
ROPE_THETA = 10000.0
LOG2E = 1.4426950408889634
NA_HEADS = 4
NA_HEAD_DIM = 64
NA_WIDTH = 256
NA_WIN_ROWS = 8
NA_WIN_COLS = 16
SSD_HEADS = 8
SSD_HEAD_DIM = 64
SSD_WIDTH = 512
SSD_GROUPS = 2
SSD_STATE = 128
SSD_CONV = 5
SSD_CHUNK = 128
SSD_CONV_CH = 1024
MLA_HEADS = 4
MLA_NOPE = 64
MLA_ROPE = 32
MLA_V = 64
MLA_QK = 96
MLA_Q_RANK = 256
MLA_KV_RANK = 128
MLA_WIDTH = 256
MIX_WIDTH = 1024
IN_WIDTH = 2736

LANES = 128
MLA_HEAD_PAD = 128
IN_WIDTH_PAD = 2816
DT_LANE0 = 32
DT_COPIES = 3
VMEM_LIMIT = 56 * 1024 * 1024

ROW_TILE = 512
INPROJ_ROW_TILE = 1024
FF_CHUNK = 1024
ADALN_COLS = 1536
NA_ROWS_PER_STEP = 32
MLA_QUERY_TILE = 512
MLA_KEY_CHUNK = 256

NT_DIMS = (((1,), (1,)), ((), ()))
TN_DIMS = (((0,), (0,)), ((), ()))


def _dot(a, b):
    return jnp.dot(a, b, preferred_element_type=F32)


def _dot_nt(a, b):
    return lax.dot_general(a, b, NT_DIMS, preferred_element_type=F32)


def _dot_tn(a, b):
    return lax.dot_general(a, b, TN_DIMS, preferred_element_type=F32)


def _sigmoid(x):
    return 1.0 / (1.0 + jnp.exp(-x))


def _silu(x):
    return x * _sigmoid(x)


def _softplus(x):
    return jnp.maximum(x, 0.0) + jnp.log1p(jnp.exp(-jnp.abs(x)))


def _split3(x):
    x0 = x.astype(BF16)
    r = x - x0.astype(F32)
    x1 = r.astype(BF16)
    r = r - x1.astype(F32)
    return x0, x1, r.astype(BF16)


def _lane_parts(x, lane):
    p0 = x.astype(BF16).astype(F32)
    r = x - p0
    p1 = r.astype(BF16).astype(F32)
    p2 = r - p1
    return jnp.where(lane < DT_LANE0 + 16, p0, jnp.where(lane < DT_LANE0 + 32, p1, p2)).astype(BF16)


def _aligned(x, m):
    return x if isinstance(x, int) else pl.multiple_of(x, m)


def _cparams(sem):
    return pltpu.CompilerParams(dimension_semantics=sem, vmem_limit_bytes=VMEM_LIMIT)


def _adaln_kernel(c_ref, w_ref, b_ref, o_ref):
    act = _silu(c_ref[...]).astype(BF16)
    o_ref[0] = _dot(act, w_ref[0].astype(BF16)) + b_ref[0]


def _adaln(cc, w_ada, b_ada):
    depth, d, n = w_ada.shape
    rows = cc.shape[0]
    tn = ADALN_COLS
    return pl.pallas_call(
        _adaln_kernel,
        grid=(depth, n // tn),
        in_specs=[
            pl.BlockSpec((rows, d), lambda l, j: (0, 0)),
            pl.BlockSpec((1, d, tn), lambda l, j: (l, 0, j)),
            pl.BlockSpec((1, 1, tn), lambda l, j: (l, 0, j)),
        ],
        out_specs=pl.BlockSpec((1, rows, tn), lambda l, j: (l, 0, j)),
        out_shape=jax.ShapeDtypeStruct((depth, rows, n), F32),
        compiler_params=_cparams(("parallel", "parallel")),
        name="adaln",
    )(cc, w_ada, b_ada.reshape(depth, 1, n))


def _inproj_kernel(x_ref, mod_ref, nw_ref, w_ref, qnw_ref, knw_ref, grp_ref,
                   cqn_ref, ckvn_ref, wuq_ref, wuk_ref, wuvt_ref, mqnw_ref, mknw_ref, g96_ref, swp_ref, *rest, rope):
    if rope:
        cos_ref, sin_ref = rest[:2]
        rest = rest[2:]
    qkv_ref, z_ref, xbc_ref, dt_ref, qm_ref, km_ref, vt_ref = rest

    def rms(t, w):
        ms = jnp.mean(t * t, axis=-1, keepdims=True)
        return t * lax.rsqrt(ms + NORM_EPS) * w

    mod = mod_ref[0, 0]
    h = (rms(x_ref[0], nw_ref[0]) * (1.0 + mod[1:2, :]) + mod[0:1, :]).astype(BF16)
    u = _dot(h, w_ref[0])

    grp = grp_ref[...]

    def head_norm(t, w):
        ss = _dot((t * t).astype(BF16), grp)
        return t * lax.rsqrt(ss * (1.0 / NA_HEAD_DIM) + NORM_EPS) * w

    q = head_norm(u[:, 0:256], qnw_ref[0]) * (NA_HEAD_DIM ** -0.5 * LOG2E)
    k = head_norm(u[:, 256:512], knw_ref[0])
    qkv_ref[0, :, 0:256] = q.astype(BF16)
    qkv_ref[0, :, 256:512] = k.astype(BF16)
    qkv_ref[0, :, 512:768] = u[:, 512:768].astype(BF16)
    z_ref[0] = u[:, 768:1280].astype(BF16)
    xbc_ref[0] = u[:, 1280:2304].astype(BF16)
    tail = u[:, 2688:2816]
    dt_ref[0] = tail

    g96 = g96_ref[...]
    swp = swp_ref[...]
    nh = MLA_HEADS

    def inv_rms96(ss):
        return lax.rsqrt(ss * (1.0 / MLA_QK) + NORM_EPS)

    def rotate(t):
        if not rope:
            return t
        reps = t.shape[1] // LANES
        sw = swp[:t.shape[1], :t.shape[1]]
        partner = _dot(t.astype(BF16), sw)
        return t * jnp.concatenate([cos_ref[...]] * reps, axis=1) + partner * jnp.concatenate([sin_ref[...]] * reps, axis=1)

    qf = _dot(rms(u[:, 2304:2560], cqn_ref[0]).astype(BF16), wuq_ref[0])
    ckvn = rms(u[:, 2560:2688], ckvn_ref[0]).astype(BF16)
    kn = _dot(ckvn, wuk_ref[0])
    vt_ref[0] = _dot_nt(wuvt_ref[0], ckvn).astype(BF16)
    lane = lax.broadcasted_iota(jnp.int32, (1, LANES), 1)
    kr = jnp.where((lane >= MLA_NOPE) & (lane < MLA_QK), pltpu.roll(tail, MLA_NOPE, axis=1), 0.0)
    mqnw = mqnw_ref[0]
    mknw = mknw_ref[0]

    q_ss = _dot((qf * qf).astype(BF16), g96)
    qn = rotate(qf * inv_rms96(q_ss) * mqnw)
    qm_ref[0] = (qn * (MLA_QK ** -0.5 * LOG2E)).astype(BF16)
    kr_ss = jnp.sum(kr * kr, axis=-1, keepdims=True)
    kr_rot = rotate(kr * mknw[:, :LANES])
    k_ss = _dot((kn * kn).astype(BF16), g96) + kr_ss
    km_ref[0] = ((kn * mknw + jnp.concatenate([kr_rot] * nh, axis=1)) * inv_rms96(k_ss)).astype(BF16)


def _inproj(xs, mod, layer, mod_row_fn, norm_w, w_in_p, qnw, knw, grp, mla_weights, rope_tabs, tm):
    b, s, d = xs.shape
    depth = w_in_p.shape[0]
    cqn_w, ckvn_w, wuq_p, wuk_p, wuvt_p, mqnw_p, mknw_p, g96, swp = mla_weights
    rope = rope_tabs is not None
    hw = MLA_HEADS * MLA_HEAD_PAD
    outs = (
        jax.ShapeDtypeStruct((b, s, 768), BF16),
        jax.ShapeDtypeStruct((b, s, SSD_WIDTH), BF16),
        jax.ShapeDtypeStruct((b, s, SSD_CONV_CH), BF16),
        jax.ShapeDtypeStruct((b, s, LANES), F32),
        jax.ShapeDtypeStruct((b, s, hw), BF16),
        jax.ShapeDtypeStruct((b, s, hw), BF16),
        jax.ShapeDtypeStruct((b, MLA_WIDTH, s), BF16),
    )
    tile = lambda w: pl.BlockSpec((1, tm, w), lambda bi, ti: (bi, ti, 0))
    tile_t = lambda r: pl.BlockSpec((1, r, tm), lambda bi, ti: (bi, 0, ti))

    def par(shape):
        return pl.BlockSpec((1,) + shape, lambda bi, ti: (layer,) + (0,) * len(shape))

    in_specs = [
        tile(d),
        pl.BlockSpec((1, 1, 6, d), lambda bi, ti: (layer, mod_row_fn(bi), 0, 0)),
        par((1, d)), par((d, IN_WIDTH_PAD)), par((1, 256)), par((1, 256)),
        pl.BlockSpec((256, 256), lambda bi, ti: (0, 0)),
        par((1, MLA_Q_RANK)), par((1, MLA_KV_RANK)), par((MLA_Q_RANK, hw)), par((MLA_KV_RANK, hw)),
        par((MLA_WIDTH, MLA_KV_RANK)),
        par((1, hw)), par((1, hw)),
        pl.BlockSpec((hw, hw), lambda bi, ti: (0, 0)), pl.BlockSpec((hw, hw), lambda bi, ti: (0, 0)),
    ]
    args = [xs, mod, norm_w.reshape(depth, 1, d), w_in_p, qnw, knw, grp, cqn_w, ckvn_w, wuq_p, wuk_p, wuvt_p, mqnw_p,
            mknw_p, g96, swp]
    if rope:
        in_specs += [pl.BlockSpec((tm, LANES), lambda bi, ti: (ti, 0))] * 2
        args += list(rope_tabs)
    return pl.pallas_call(
        functools.partial(_inproj_kernel, rope=rope),
        grid=(b, s // tm),
        in_specs=in_specs,
        out_specs=[tile(768), tile(SSD_WIDTH), tile(SSD_CONV_CH), tile(LANES), tile(hw), tile(hw), tile_t(MLA_WIDTH)],
        out_shape=outs,
        compiler_params=_cparams(("parallel", "parallel")),
        name="inproj",
    )(*args)


NA_VARIANT_ROWS = (0, 1, 2, 3, 4, -3, -2, -1)


def _na_bias_kernel(rpb_ref, o_ref, t_ref, *, n_rows):
    nrow_off = 2 * NA_WIN_ROWS - 1
    ncol_off = 2 * NA_WIN_COLS - 1
    base = (pl.program_id(0) * NA_HEADS + pl.program_id(1)) * (nrow_off * ncol_off)
    qi = lax.broadcasted_iota(jnp.int32, (GRID_W, LANES), 0)
    li = lax.broadcasted_iota(jnp.int32, (GRID_W, LANES), 1)
    kcol = li & (GRID_W - 1)
    colidx = jnp.clip(kcol - qi, -(NA_WIN_COLS - 1), NA_WIN_COLS - 1) + (NA_WIN_COLS - 1)
    cstart = jnp.clip(qi - NA_WIN_COLS // 2, 0, GRID_W - NA_WIN_COLS)
    valid = (kcol >= cstart) & (kcol < cstart + NA_WIN_COLS)
    for d in range(nrow_off):
        acc = jnp.zeros((GRID_W, LANES), F32)
        for j in range(ncol_off):
            acc = jnp.where(colidx == j, rpb_ref[base + d * ncol_off + j], acc)
        t_ref[d] = jnp.where(valid, acc * LOG2E, -jnp.inf)
    for v, r_rep in enumerate(NA_VARIANT_ROWS):
        r = r_rep if r_rep >= 0 else n_rows + r_rep
        rs = min(max(r - NA_WIN_ROWS // 2, 0), n_rows - NA_WIN_ROWS)
        for p in range(NA_WIN_ROWS // 2):
            d0 = rs + 2 * p - r + (NA_WIN_ROWS - 1)
            o_ref[0, v, :, p * LANES:(p + 1) * LANES] = jnp.where(li < GRID_W, t_ref[d0], t_ref[d0 + 1])


def _na_bias(rpb, n_rows):
    depth, heads = rpb.shape[0], rpb.shape[1]
    n_win = NA_WIN_ROWS * GRID_W
    return pl.pallas_call(
        functools.partial(_na_bias_kernel, n_rows=n_rows),
        grid=(depth, heads),
        in_specs=[pl.BlockSpec(memory_space=pltpu.SMEM)],
        out_specs=pl.BlockSpec((1, 8, GRID_W, n_win), lambda l, h: (l, 0, h, 0)),
        out_shape=jax.ShapeDtypeStruct((depth, 8, heads * GRID_W, n_win), F32),
        scratch_shapes=[pltpu.VMEM((2 * NA_WIN_ROWS - 1, GRID_W, LANES), F32)],
        compiler_params=_cparams(("parallel", "parallel")),
        name="na_bias",
    )(rpb.reshape(-1))


def _head_masks():
    lane = lax.broadcasted_iota(jnp.int32, (1, NA_WIDTH), 1)
    return [jnp.where((lane >= h * NA_HEAD_DIM) & (lane < (h + 1) * NA_HEAD_DIM), 1.0, 0.0) for h in range(NA_HEADS)]


def _stacked_scores(q, keys, biases, hms):
    qs = jnp.concatenate([q * hm.astype(BF16) for hm in hms], axis=0)
    scores = []
    for k, b in zip(keys, biases):
        s = _dot_nt(qs, k)
        scores.append(s if b is None else s + b)
    return scores


def _stacked_attend(scores, values, hms):
    m_rows = scores[0].shape[0] // NA_HEADS
    m = scores[0].max(axis=-1, keepdims=True)
    for s in scores[1:]:
        m = jnp.maximum(m, s.max(axis=-1, keepdims=True))
    l = None
    o = None
    for s, v in zip(scores, values):
        p = jnp.exp2(s - m)
        ls = p.sum(axis=-1, keepdims=True)
        os_ = _dot(p.astype(BF16), v)
        l = ls if l is None else l + ls
        o = os_ if o is None else o + os_
    o = o * (1.0 / l)
    out = o[0:m_rows] * hms[0]
    for h in range(1, NA_HEADS):
        out = out + o[h * m_rows:(h + 1) * m_rows] * hms[h]
    return out


def _na_kernel(q_ref, k_ref, v_ref, kc_ref, vc_ref, bias_ref, o_ref, *, rows_per_step, n_rows):
    blk = pl.program_id(1)
    kc = kc_ref[0]
    vc = vc_ref[0]
    n_win = NA_WIN_ROWS * GRID_W
    hms = _head_masks()

    def window_start(i):
        r = blk * rows_per_step + i
        rs = jnp.clip(r - NA_WIN_ROWS // 2, 0, n_rows - NA_WIN_ROWS)
        var = jnp.where(r < NA_WIN_ROWS // 2, r,
                        jnp.where(r > n_rows - NA_WIN_ROWS // 2, r - (n_rows - NA_WIN_ROWS), NA_WIN_ROWS // 2))
        return pl.multiple_of(rs * GRID_W, GRID_W), var

    def row_scores(i):
        start, var = window_start(i)
        q = q_ref[0, i * GRID_W:(i + 1) * GRID_W, :]
        return _stacked_scores(q, [k_ref[0, pl.ds(start, n_win), :], kc], [bias_ref[0, var], None], hms)

    nxt = row_scores(0)
    for i in range(rows_per_step):
        cur = nxt
        if i + 1 < rows_per_step:
            nxt = row_scores(i + 1)
        start, _ = window_start(i)
        out = _stacked_attend(cur, [v_ref[0, pl.ds(start, n_win), :], vc], hms)
        o_ref[0, i * GRID_W:(i + 1) * GRID_W, :] = out.astype(BF16)


def _na_attention(qkv_l, qkv_c, bias_all, layer, rows_per_step=NA_ROWS_PER_STEP):
    b, s, _ = qkv_l.shape
    ctx_len = qkv_c.shape[1]
    n_rows = s // GRID_W
    tq = rows_per_step * GRID_W
    return pl.pallas_call(
        functools.partial(_na_kernel, rows_per_step=rows_per_step, n_rows=n_rows),
        grid=(b, n_rows // rows_per_step),
        in_specs=[
            pl.BlockSpec((1, tq, 256), lambda bi, ti: (bi, ti, 0)),
            pl.BlockSpec((1, s, 256), lambda bi, ti: (bi, 0, 1)),
            pl.BlockSpec((1, s, 256), lambda bi, ti: (bi, 0, 2)),
            pl.BlockSpec((1, ctx_len, 256), lambda bi, ti: (bi, 0, 1)),
            pl.BlockSpec((1, ctx_len, 256), lambda bi, ti: (bi, 0, 2)),
            pl.BlockSpec((1,) + bias_all.shape[1:], lambda bi, ti: (layer, 0, 0, 0)),
        ],
        out_specs=pl.BlockSpec((1, tq, 256), lambda bi, ti: (bi, ti, 0)),
        out_shape=jax.ShapeDtypeStruct((b, s, NA_WIDTH), BF16),
        compiler_params=_cparams(("parallel", "parallel")),
        name="na_attn",
    )(qkv_l, qkv_l, qkv_l, qkv_c, qkv_c, bias_all)


def _na_ctx_kernel(q_ref, k_ref, v_ref, o_ref):
    hms = _head_masks()
    scores = _stacked_scores(q_ref[0], [k_ref[0]], [None], hms)
    o_ref[0] = _stacked_attend(scores, [v_ref[0]], hms).astype(BF16)


def _na_ctx_attention(qkv_c):
    b, c, _ = qkv_c.shape
    spec = lambda j: pl.BlockSpec((1, c, 256), lambda bi: (bi, 0, j))
    return pl.pallas_call(
        _na_ctx_kernel,
        grid=(b,),
        in_specs=[spec(0), spec(1), spec(2)],
        out_specs=spec(0),
        out_shape=jax.ShapeDtypeStruct((b, c, NA_WIDTH), BF16),
        compiler_params=_cparams(("parallel",)),
        name="na_ctx_attn",
    )(qkv_c, qkv_c, qkv_c)


def _ssd_kernel(z_l, xbc_l, dt_l, z_c, xbc_c, dt_c,
                convw_ref, convb_ref, bias_row_ref, alog_row_ref, dskip_ref, normw_ref, out_l, out_c,
                xs_s, bc_s, yf_s, yb_s, stf_s, stb_s, dt_s, acf_s, acb_s, g_s, *, n_lat, n_ctx, need_ctx):
    q = SSD_CHUNK
    lat_row0 = n_ctx * q

    convw = convw_ref[0]
    convb = convb_ref[0]
    win = q + 32
    wi = lax.broadcasted_iota(jnp.int32, (q, win), 0)
    wr = lax.broadcasted_iota(jnp.int32, (q, win), 1)
    mid = SSD_CONV // 2

    side_taps = [j for j in range(SSD_CONV) if j != mid]

    def shift_mats(off):
        return jnp.concatenate([jnp.where(wr == wi + (off + j - mid), 1.0, 0.0).astype(BF16) for j in side_taps], axis=0)

    def conv_chunk(xref, lo, off, smats, dst):
        w_in = xref[0, pl.ds(lo, win), :]
        acc = convb + w_in[off:off + q].astype(F32) * convw[mid:mid + 1, :]
        shifted = _dot(smats, w_in)
        for n, j in enumerate(side_taps):
            acc = acc + shifted[n * q:(n + 1) * q] * convw[j:j + 1, :]
        act = _silu(acc)
        xs_s[pl.ds(dst, q), :] = act[:, :SSD_WIDTH]
        bc_s[pl.ds(dst, q), :] = act[:, SSD_WIDTH:].astype(BF16)

    def conv_seq(xref, n, row0):
        conv_chunk(xref, 0, 0, shift_mats(0), row0)
        conv_chunk(xref, (n - 1) * q - 32, 32, shift_mats(32), row0 + (n - 1) * q)
        if n > 2:
            smats = shift_mats(16)

            def body(c, carry):
                conv_chunk(xref, pl.multiple_of(c * q - 16, 16), 16, smats, pl.multiple_of(row0 + c * q, q))
                return carry
            lax.fori_loop(1, n - 1, body, 0, unroll=2)

    conv_seq(xbc_c, n_ctx, 0)
    conv_seq(xbc_l, n_lat, lat_row0)

    lane = lax.broadcasted_iota(jnp.int32, (1, LANES), 1)
    dmask = (lane >= DT_LANE0) & (lane < DT_LANE0 + 16 * DT_COPIES)
    bias_row = bias_row_ref[0]
    a_row = jnp.where(dmask, -jnp.exp(alog_row_ref[0]) * LOG2E, 0.0)
    ri = lax.broadcasted_iota(jnp.int32, (q, q), 0)
    ci = lax.broadcasted_iota(jnp.int32, (q, q), 1)
    lo_mask = ci <= ri
    up_mask = ri <= ci
    tri_lo = jnp.where(lo_mask, 1.0, 0.0).astype(BF16)
    tri_up = jnp.where(up_mask, 1.0, 0.0).astype(BF16)
    sel_r = lax.broadcasted_iota(jnp.int32, (LANES, SSD_HEADS * LANES), 0)
    sel_c = lax.broadcasted_iota(jnp.int32, (LANES, SSD_HEADS * LANES), 1)
    sel64_r = lax.broadcasted_iota(jnp.int32, (LANES, SSD_WIDTH), 0)
    sel64_c = lax.broadcasted_iota(jnp.int32, (LANES, SSD_WIDTH), 1)
    lane_lt64 = lax.broadcasted_iota(jnp.int32, (1, LANES), 1) < SSD_HEAD_DIM
    pair_lo = jnp.where(lane_lt64, 1.0, 0.0).astype(BF16)
    pair_hi = jnp.where(lane_lt64, 0.0, 1.0).astype(BF16)

    def sel_matrices(direction):
        in3 = (sel_r >= DT_LANE0) & (sel_r < DT_LANE0 + 48)
        sel128 = jnp.where(in3 & (((sel_r - DT_LANE0) & 15) == direction * SSD_HEADS + (sel_c >> 7)), 1.0, 0.0)
        in2 = (sel64_r >= DT_LANE0) & (sel64_r < DT_LANE0 + 32)
        sel64 = jnp.where(in2 & (((sel64_r - DT_LANE0) & 15) == direction * SSD_HEADS + (sel64_c >> 6)), 1.0, 0.0)
        return sel128.astype(BF16), sel64.astype(BF16)

    sels = (sel_matrices(0), sel_matrices(1))
    tri3_both = jnp.concatenate([jnp.concatenate([tri_lo] * 3, axis=1), jnp.concatenate([tri_up] * 3, axis=1)], axis=0)
    masks = (lo_mask, up_mask)

    y_refs = (yf_s, yb_s)
    st_refs = (stf_s, stb_s)
    stf_s[...] = jnp.zeros_like(stf_s)
    stb_s[...] = jnp.zeros_like(stb_s)

    ac_refs = (acf_s, acb_s)

    def decay_chunk(dt_ref, c, row0):
        rows = pl.ds(_aligned(row0 + c * q, q), q)
        raw = dt_ref[0, pl.ds(_aligned(c * q, q), q), :]
        dt = jnp.where(dmask, _softplus(raw + bias_row), 0.0)
        a0, a1, a2 = _split3(dt * a_row)
        both = _dot(tri3_both, jnp.concatenate([a0, a1, a2], axis=0))
        dt_s[rows, :] = dt
        acf_s[rows, :] = both[0:q]
        acb_s[rows, :] = both[q:2 * q]
        bc = bc_s[rows, :]
        for g in range(SSD_GROUPS):
            g_s[rows, g * q:(g + 1) * q] = _dot_nt(bc[:, (SSD_GROUPS + g) * SSD_STATE:(SSD_GROUPS + g + 1) * SSD_STATE],
                                                   bc[:, g * SSD_STATE:(g + 1) * SSD_STATE])

    for i in range(n_ctx):
        decay_chunk(dt_c, i, 0)

    def decay_body(c, carry):
        decay_chunk(dt_l, c, lat_row0)
        return carry

    lax.fori_loop(0, n_lat, decay_body, 0, unroll=2)

    def chunk(c, row0, direction, compute_y):
        rows = pl.ds(_aligned(row0 + c * q, q), q)
        x = xs_s[rows, :]
        bc = bc_s[rows, :]
        sel128, sel64 = sels[direction]
        dt = dt_s[rows, :]
        acum = ac_refs[direction][rows, :]
        a_exp = _dot(_lane_parts(acum, lane), sel128)
        dt64 = _dot(_lane_parts(dt, lane), sel64)
        st = st_refs[direction][...]
        grp_w = SSD_WIDTH // SSD_GROUPS
        b_g = [bc[:, g * SSD_STATE:(g + 1) * SSD_STATE] for g in range(SSD_GROUPS)]
        c_g = [bc[:, (SSD_GROUPS + g) * SSD_STATE:(SSD_GROUPS + g + 1) * SSD_STATE] for g in range(SSD_GROUPS)]
        if compute_y:
            stb = st.astype(BF16)
            gmats = [g_s[rows, g * q:(g + 1) * q] for g in range(SSD_GROUPS)]
            offs = [_dot(c_g[g], stb[:, g * grp_w:(g + 1) * grp_w]) for g in range(SSD_GROUPS)]
        yield

        ac64 = jnp.concatenate(
            [jnp.where(lane_lt64, a_exp[:, (2 * j) * LANES:(2 * j + 1) * LANES],
                       a_exp[:, (2 * j + 1) * LANES:(2 * j + 2) * LANES]) for j in range(SSD_HEADS // 2)], axis=1)
        acum_t = acum.T
        end = q - 1 if direction == 0 else 0
        ac_end = ac64[end:end + 1, :]
        xd = x * dt64
        xdb = xd.astype(BF16)
        xdw = (xd * jnp.exp2(ac_end - ac64)).astype(BF16)
        yield

        if compute_y:
            parts = []
            for pair in range(SSD_HEADS // 2):
                h0 = 2 * pair
                gmat = gmats[h0 // (SSD_HEADS // SSD_GROUPS)]
                xp = xdb[:, h0 * SSD_HEAD_DIM:(h0 + 2) * SSD_HEAD_DIM]
                mats = []
                for hh in (h0, h0 + 1):
                    row = DT_LANE0 + direction * SSD_HEADS + hh
                    dmat = a_exp[:, hh * LANES:(hh + 1) * LANES] - acum_t[row:row + 1, :]
                    lmat = jnp.exp2(jnp.where(masks[direction], dmat, -jnp.inf))
                    mats.append((gmat * lmat).astype(BF16))
                parts.append(_dot(jnp.concatenate(mats, axis=1),
                                  jnp.concatenate([xp * pair_lo, xp * pair_hi], axis=0)))
                yield
            y = jnp.concatenate(parts, axis=1) + jnp.concatenate(offs, axis=1) * jnp.exp2(ac64)
            y_refs[direction][rows, :] = y

        new = [_dot_tn(b_g[g], xdw[:, g * grp_w:(g + 1) * grp_w]) for g in range(SSD_GROUPS)]
        st_refs[direction][...] = st * jnp.exp2(ac_end) + jnp.concatenate(new, axis=1)
        yield

    def run_interleaved(*gens):
        live = list(gens)
        while live:
            for g in list(live):
                try:
                    next(g)
                except StopIteration:
                    live.remove(g)

    for i in range(n_ctx):
        run_interleaved(chunk(i, 0, 0, need_ctx), chunk(n_ctx - 1 - i, 0, 1, need_ctx))

    def lat_body(i, carry):
        run_interleaved(chunk(i, lat_row0, 0, True), chunk(n_lat - 1 - i, lat_row0, 1, True))
        return carry

    lax.fori_loop(0, n_lat, lat_body, 0, unroll=2)

    dskip = dskip_ref[0]
    normw = normw_ref[0]

    def finish(z_ref, o_ref, c, row0):
        rows = pl.ds(_aligned(row0 + c * q, q), q)
        crow = pl.ds(_aligned(c * q, q), q)
        y = yf_s[rows, :] + yb_s[rows, :] + dskip * xs_s[rows, :]
        y = y * _silu(z_ref[0, crow, :].astype(F32))
        ms = jnp.mean(y * y, axis=-1, keepdims=True)
        o_ref[0, crow, :] = (y * lax.rsqrt(ms + NORM_EPS) * normw).astype(BF16)

    if need_ctx:
        for i in range(n_ctx):
            finish(z_c, out_c, i, 0)
    else:
        out_c[...] = jnp.zeros_like(out_c)

    def fin_body(c, carry):
        finish(z_l, out_l, c, lat_row0)
        return carry

    lax.fori_loop(0, n_lat, fin_body, 0, unroll=2)


def _ssd(z_l, xbc_l, dt_l, z_c, xbc_c, dt_c, params, layer, need_ctx):
    b, s, _ = z_l.shape
    c = z_c.shape[1]
    n_lat, n_ctx = s // SSD_CHUNK, c // SSD_CHUNK
    t = s + c
    convw, convb, bias_row, alog_row, dskip, normw = params

    def seq_specs(n):
        return [
            pl.BlockSpec((1, n, SSD_WIDTH), lambda bi: (bi, 0, 0)),
            pl.BlockSpec((1, n, SSD_CONV_CH), lambda bi: (bi, 0, 0)),
            pl.BlockSpec((1, n, LANES), lambda bi: (bi, 0, 0)),
        ]

    def par(shape):
        return pl.BlockSpec((1,) + shape, lambda bi: (layer,) + (0,) * len(shape))

    return pl.pallas_call(
        functools.partial(_ssd_kernel, n_lat=n_lat, n_ctx=n_ctx, need_ctx=need_ctx),
        grid=(b,),
        in_specs=seq_specs(s) + seq_specs(c) + [
            par((8, SSD_CONV_CH)), par((1, SSD_CONV_CH)), par((1, LANES)), par((1, LANES)),
            par((1, SSD_WIDTH)), par((1, SSD_WIDTH)),
        ],
        out_specs=[
            pl.BlockSpec((1, s, SSD_WIDTH), lambda bi: (bi, 0, 0)),
            pl.BlockSpec((1, c, SSD_WIDTH), lambda bi: (bi, 0, 0)),
        ],
        out_shape=(jax.ShapeDtypeStruct((b, s, SSD_WIDTH), BF16), jax.ShapeDtypeStruct((b, c, SSD_WIDTH), BF16)),
        scratch_shapes=[
            pltpu.VMEM((t, SSD_WIDTH), F32),
            pltpu.VMEM((t, SSD_WIDTH), BF16),
            pltpu.VMEM((t, SSD_WIDTH), F32),
            pltpu.VMEM((t, SSD_WIDTH), F32),
            pltpu.VMEM((SSD_STATE, SSD_WIDTH), F32),
            pltpu.VMEM((SSD_STATE, SSD_WIDTH), F32),
            pltpu.VMEM((t, LANES), F32),
            pltpu.VMEM((t, LANES), F32),
            pltpu.VMEM((t, LANES), F32),
            pltpu.VMEM((t, SSD_GROUPS * SSD_CHUNK), F32),
        ],
        compiler_params=_cparams(("parallel",)),
        name="ssd",
    )(z_l, xbc_l, dt_l, z_c, xbc_c, dt_c, convw, convb, bias_row, alog_row, dskip, normw)


def _mla_t_kernel(*refs, with_latent):
    if with_latent:
        q_ref, kl_ref, vtl_ref, kc_ref, vtc_ref, o_ref, s_scr = refs
        n_lat = kl_ref.shape[1]
    else:
        q_ref, kc_ref, vtc_ref, o_ref, s_scr = refs
        kl_ref = vtl_ref = None
        n_lat = 0
    n_ctx = kc_ref.shape[1]
    kc_ = min(MLA_KEY_CHUNK, n_ctx)
    chunks = [(kl_ref, vtl_ref, j * kc_, j * kc_) for j in range(n_lat // kc_)]
    chunks += [(kc_ref, vtc_ref, j * kc_, n_lat + j * kc_) for j in range(n_ctx // kc_)]
    key_blocks = ([(kl_ref, 0)] if with_latent else []) + [(kc_ref, n_lat)]

    def scores(h):
        sl = slice(h * MLA_HEAD_PAD, (h + 1) * MLA_HEAD_PAD)
        qh = q_ref[0, :, sl]
        m = None
        for k_ref, dst in key_blocks:
            n = k_ref.shape[1]
            s = _dot_nt(k_ref[0, :, sl], qh)
            s_scr[h % 2, dst:dst + n, :] = s
            mc = s.max(axis=0, keepdims=True)
            m = mc if m is None else jnp.maximum(m, mc)
        return m

    def attend(h, m):
        vrows = slice(h * MLA_V, (h + 1) * MLA_V)
        l = None
        o = None
        for _, vt_ref, src, dst in chunks:
            p = jnp.exp2(s_scr[h % 2, dst:dst + kc_, :] - m)
            lc = p.sum(axis=0, keepdims=True)
            oc = _dot(vt_ref[0, vrows, src:src + kc_], p.astype(BF16))
            l = lc if l is None else l + lc
            o = oc if o is None else o + oc
        return o * (1.0 / l)

    outs = []
    m_next = scores(0)
    for h in range(MLA_HEADS):
        m_cur = m_next
        if h + 1 < MLA_HEADS:
            m_next = scores(h + 1)
        outs.append(attend(h, m_cur))
    o_ref[0] = jnp.concatenate(outs, axis=0).T.astype(BF16)


def _mla_attention_t(q, k_l, vt_l, k_c, vt_c, tq):
    b, sq, wq = q.shape
    c = k_c.shape[1]
    with_latent = k_l is not None
    s = k_l.shape[1] if with_latent else 0
    in_specs = [pl.BlockSpec((1, tq, wq), lambda bi, ti: (bi, ti, 0))]
    args = [q]
    if with_latent:
        in_specs += [pl.BlockSpec((1, s, wq), lambda bi, ti: (bi, 0, 0)),
                     pl.BlockSpec((1, MLA_WIDTH, s), lambda bi, ti: (bi, 0, 0))]
        args += [k_l, vt_l]
    in_specs += [pl.BlockSpec((1, c, wq), lambda bi, ti: (bi, 0, 0)),
                 pl.BlockSpec((1, MLA_WIDTH, c), lambda bi, ti: (bi, 0, 0))]
    args += [k_c, vt_c]
    return pl.pallas_call(
        functools.partial(_mla_t_kernel, with_latent=with_latent),
        grid=(b, sq // tq),
        in_specs=in_specs,
        out_specs=pl.BlockSpec((1, tq, MLA_WIDTH), lambda bi, ti: (bi, ti, 0)),
        out_shape=jax.ShapeDtypeStruct((b, sq, MLA_WIDTH), BF16),
        scratch_shapes=[pltpu.VMEM((2, s + c, tq), F32)],
        compiler_params=_cparams(("parallel", "parallel")),
        name="mla_attn" if with_latent else "mla_ctx_attn",
    )(*args)


def _outffn_resident_kernel(x_ref, na_ref, ssd_ref, mla_ref, mod_ref, nw_ref, wo_ref, w1_ref, w2_ref, o_ref, *, tf):
    mod = mod_ref[0, 0]
    mix = (_dot(na_ref[0], wo_ref[0, 0:NA_WIDTH, :])
           + _dot(ssd_ref[0], wo_ref[0, NA_WIDTH:NA_WIDTH + SSD_WIDTH, :])
           + _dot(mla_ref[0], wo_ref[0, NA_WIDTH + SSD_WIDTH:, :]))
    x1 = x_ref[0] + mod[2:3, :] * mix
    ms = jnp.mean(x1 * x1, axis=-1, keepdims=True)
    xn = x1 * lax.rsqrt(ms + NORM_EPS) * nw_ref[0]
    xn = (xn * (1.0 + mod[4:5, :]) + mod[3:4, :]).astype(BF16)
    acc = None
    for f in range(D_FF // tf):
        hmid = _dot(xn, w1_ref[0, :, f * tf:(f + 1) * tf])
        hmid = jnp.square(jnp.maximum(hmid, 0.0)).astype(BF16)
        part = _dot(hmid, w2_ref[0, f * tf:(f + 1) * tf, :])
        acc = part if acc is None else acc + part
    o_ref[0] = x1 + mod[5:6, :] * acc


def _outffn_resident(xs, na, ssd, mla, mod, layer, mod_row_fn, norm_w, w_out, w1, w2, tm, tf):
    b, s, d = xs.shape
    depth = w_out.shape[0]
    tile = lambda w: pl.BlockSpec((1, tm, w), lambda bi, ti: (bi, ti, 0))
    once = lambda shape: pl.BlockSpec((1,) + shape, lambda bi, ti: (layer, 0, 0), pipeline_mode=pl.Buffered(1))
    return pl.pallas_call(
        functools.partial(_outffn_resident_kernel, tf=tf),
        grid=(b, s // tm),
        in_specs=[
            tile(d), tile(NA_WIDTH), tile(SSD_WIDTH), tile(MLA_WIDTH),
            pl.BlockSpec((1, 1, 6, d), lambda bi, ti: (layer, mod_row_fn(bi), 0, 0)),
            pl.BlockSpec((1, 1, d), lambda bi, ti: (layer, 0, 0)),
            once((MIX_WIDTH, d)), once((d, D_FF)), once((D_FF, d)),
        ],
        out_specs=tile(d),
        out_shape=jax.ShapeDtypeStruct((b, s, d), F32),
        compiler_params=_cparams(("parallel", "parallel")),
        name="outproj_mlp_res",
    )(xs, na, ssd, mla, mod, norm_w.reshape(depth, 1, d), w_out, w1, w2)


def _rope_tables(n_tokens):
    pos = jnp.arange(n_tokens)
    axes = jnp.stack([pos // GRID_W, pos % GRID_W], axis=-1).astype(F32)
    n_freq = MLA_ROPE // 4
    inv_freq = ROPE_THETA ** (-jnp.arange(n_freq, dtype=F32) / n_freq)
    ang = axes[:, :, None] * inv_freq
    cos, sin = jnp.cos(ang), jnp.sin(ang)
    cos_l = jnp.stack([cos, cos], axis=2).reshape(n_tokens, MLA_ROPE)
    sin_l = jnp.stack([-sin, sin], axis=2).reshape(n_tokens, MLA_ROPE)

    def place(t, fill):
        left = jnp.full((n_tokens, MLA_NOPE), fill, F32)
        right = jnp.full((n_tokens, LANES - MLA_QK), fill, F32)
        return jnp.concatenate([left, t, right], axis=1)

    return place(cos_l, 1.0), place(sin_l, 0.0)


def _prep_params(w_in, na_qn_w, na_kn_w, ssd_conv_w, ssd_conv_b, ssd_dt_bias, ssd_a_log, ssd_d, ssd_norm_w,
                 mla_cq_norm_w, mla_ckv_norm_w, mla_w_uq, mla_w_ukv, mla_qn_w, mla_kn_w):
    depth = w_in.shape[0]
    d = w_in.shape[1]
    dtw = w_in[..., 2304:2320]
    w_in_p = jnp.concatenate(
        [w_in[..., :2304], w_in[..., 2320:2576], w_in[..., 2576:2704], w_in[..., 2704:2736]]
        + [dtw] * DT_COPIES + [jnp.zeros((depth, d, IN_WIDTH_PAD - 2720 - 16 * DT_COPIES), F32)], axis=-1).astype(BF16)
    qnw = jnp.tile(na_qn_w, (1, NA_HEADS))[:, None, :]
    knw = jnp.tile(na_kn_w, (1, NA_HEADS))[:, None, :]
    gi = np.arange(NA_WIDTH) // NA_HEAD_DIM
    grp = jnp.asarray((gi[:, None] == gi[None, :]).astype(np.float32), BF16)

    convw = jnp.pad(ssd_conv_w, ((0, 0), (0, 8 - SSD_CONV), (0, 0)))
    convb = ssd_conv_b[:, None, :]

    def dt_row(t):
        flat = jnp.tile(t.reshape(depth, 16), (1, DT_COPIES))
        return jnp.pad(flat, ((0, 0), (DT_LANE0, LANES - DT_LANE0 - 16 * DT_COPIES)))[:, None, :]

    bias_row = dt_row(ssd_dt_bias)
    alog_row = dt_row(ssd_a_log)
    dskip = jnp.repeat(ssd_d, SSD_HEAD_DIM, axis=-1)[:, None, :]
    normw = ssd_norm_w[:, None, :]
    ssd_params = (convw, convb, bias_row, alog_row, dskip, normw)

    wuq = mla_w_uq.reshape(depth, MLA_Q_RANK, MLA_HEADS, MLA_QK)
    wuq_p = jnp.pad(wuq, ((0, 0), (0, 0), (0, 0), (0, MLA_HEAD_PAD - MLA_QK))).reshape(depth, MLA_Q_RANK, -1).astype(BF16)
    wukv = mla_w_ukv.reshape(depth, MLA_KV_RANK, MLA_HEADS, MLA_NOPE + MLA_V)
    wk = jnp.pad(wukv[..., :MLA_NOPE], ((0, 0), (0, 0), (0, 0), (0, MLA_HEAD_PAD - MLA_NOPE))).reshape(depth, MLA_KV_RANK, -1)
    wuvt = jnp.swapaxes(wukv[..., MLA_NOPE:].reshape(depth, MLA_KV_RANK, -1), 1, 2).astype(BF16)
    pad_head = lambda t: jnp.tile(jnp.pad(t, ((0, 0), (0, MLA_HEAD_PAD - MLA_QK))), (1, MLA_HEADS))[:, None, :]
    li = np.arange(MLA_HEADS * MLA_HEAD_PAD)
    g96 = jnp.asarray((li[:, None] // MLA_HEAD_PAD == li[None, :] // MLA_HEAD_PAD).astype(np.float32), BF16)
    in_rope = (li % MLA_HEAD_PAD >= MLA_NOPE) & (li % MLA_HEAD_PAD < MLA_QK)
    swp = jnp.asarray(((li[:, None] == (li[None, :] ^ 8)) & in_rope[None, :]).astype(np.float32), BF16)
    mla_weights = (mla_cq_norm_w[:, None, :], mla_ckv_norm_w[:, None, :], wuq_p, wk.astype(BF16), wuvt,
                   pad_head(mla_qn_w), pad_head(mla_kn_w), g96, swp)
    return w_in_p, qnw, knw, grp, ssd_params, mla_weights


def kernel(x, c, ctx, c_ctx, w_ada, b_ada, norm1_w, norm2_w, w_in, w_out, na_qn_w, na_kn_w, na_rpb, ssd_conv_w, ssd_conv_b, ssd_dt_bias, ssd_a_log, ssd_d, ssd_norm_w, mla_cq_norm_w, mla_ckv_norm_w, mla_w_uq, mla_w_ukv, mla_qn_w, mla_kn_w, w_ff1, w_ff2):
    b, s, d = x.shape
    ctx_len = ctx.shape[1]
    depth = w_in.shape[0]
    n_rows = s // GRID_W
    assert d == D_MODEL and n_rows % NA_ROWS_PER_STEP == 0 and n_rows > NA_WIN_ROWS and ctx_len % SSD_CHUNK == 0
    assert s % ROW_TILE == 0 and s % INPROJ_ROW_TILE == 0 and s % MLA_QUERY_TILE == 0 and ctx_len % MLA_KEY_CHUNK == 0

    mod_rows = ((b + 1 + 7) // 8) * 8
    cc = jnp.concatenate([c, c_ctx[None, :], jnp.zeros((mod_rows - b - 1, d), F32)], axis=0)
    mod = _adaln(cc, w_ada, b_ada).reshape(depth, mod_rows, 6, d)

    w_in_p, qnw, knw, grp, ssd_params, mla_weights = _prep_params(
        w_in, na_qn_w, na_kn_w, ssd_conv_w, ssd_conv_b, ssd_dt_bias, ssd_a_log, ssd_d, ssd_norm_w,
        mla_cq_norm_w, mla_ckv_norm_w, mla_w_uq, mla_w_ukv, mla_qn_w, mla_kn_w)
    w_out_b = w_out.astype(BF16)
    w1_b = w_ff1.astype(BF16)
    w2_b = w_ff2.astype(BF16)
    bias_all = _na_bias(na_rpb, n_rows)
    rope_tabs = _rope_tables(s)

    lat_row = lambda bi: bi
    ctx_row = lambda bi: b
    tm = ROW_TILE
    tm_c = min(tm, b * ctx_len)
    ctx_flat = ctx.reshape(1, b * ctx_len, d)

    for i in range(depth):
        need_ctx = i < depth - 1
        qkv_l, z_l, xbc_l, dt_l, qm_l, km_l, vt_l = _inproj(
            x, mod, i, lat_row, norm1_w, w_in_p, qnw, knw, grp, mla_weights, rope_tabs, INPROJ_ROW_TILE)
        qkv_c, z_c, xbc_c, dt_c, qm_c, km_c, vt_c = _inproj(
            ctx_flat.reshape(b, ctx_len, d), mod, i, ctx_row, norm1_w, w_in_p, qnw, knw, grp, mla_weights, None, ctx_len)

        na_l = _na_attention(qkv_l, qkv_c, bias_all, i)
        ssd_l, ssd_c = _ssd(z_l, xbc_l, dt_l, z_c, xbc_c, dt_c, ssd_params, i, need_ctx)
        mla_l = _mla_attention_t(qm_l, km_l, vt_l, km_c, vt_c, MLA_QUERY_TILE)

        x = _outffn_resident(x, na_l, ssd_l, mla_l, mod, i, lat_row, norm2_w, w_out_b, w1_b, w2_b, tm, FF_CHUNK)
        if need_ctx:
            na_c = _na_ctx_attention(qkv_c)
            mla_c = _mla_attention_t(qm_c, None, None, km_c, vt_c, ctx_len)
            flat = lambda t: t.reshape(1, b * ctx_len, t.shape[-1])
            ctx_flat = _outffn_resident(ctx_flat, flat(na_c), flat(ssd_c), flat(mla_c), mod, i, ctx_row, norm2_w,
                                        w_out_b, w1_b, w2_b, tm_c, FF_CHUNK)
    return x
```
